```python
import jax, jax.numpy as jnp
from jax import lax
import numpy as np

D_MODEL = 1024
BATCH = 2
SEQ = 8192
DEPTH = 2

HEAD_DIM = 64
BLK = 128
ROPE_THETA = 500000.0
ROT_DIM = HEAD_DIM // 4
EPS = 1e-6
SGU_GROUPS = 8
SGU_WIDTH = SGU_GROUPS * HEAD_DIM
SGU_CHUNK = 128
DIL_GROUPS = ((128, 1), (512, 4), (2048, 16))
DIL_HPG = 4
DIL_HEADS = DIL_HPG * len(DIL_GROUPS)
DIL_WIDTH = DIL_HEADS * HEAD_DIM
SWA_Q_HEADS = 16
SWA_KV_HEADS = 2
SWA_WINDOW = 128
N_GROUPS = 4
EXPERTS_PER_GROUP = 8
N_EXPERTS = N_GROUPS * EXPERTS_PER_GROUP
EXPERT_FF = 256
TOP_K = 2
MOE_BLOCK = 128
N_EVEN = (DEPTH + 1) // 2
N_ODD = DEPTH // 2
IN_EVEN = 2 * SGU_WIDTH + 3 * DIL_WIDTH
OUT_EVEN = SGU_WIDTH + DIL_HPG * HEAD_DIM
EVEN_SPLITS = (SGU_WIDTH, 2 * SGU_WIDTH, 2 * SGU_WIDTH + DIL_WIDTH, 2 * SGU_WIDTH + 2 * DIL_WIDTH)
IN_ODD = (SWA_Q_HEADS + 2 * SWA_KV_HEADS) * HEAD_DIM
OUT_ODD = SWA_Q_HEADS * HEAD_DIM
ODD_SPLITS = (SWA_Q_HEADS * HEAD_DIM, (SWA_Q_HEADS + SWA_KV_HEADS) * HEAD_DIM)

kernel_name = "hybrid_sgu_dilated_swa_sink_hmoe"


def rms_norm(x, g):
    xf = x.astype(jnp.float32)
    y = xf * lax.rsqrt(jnp.mean(xf * xf, axis=-1, keepdims=True) + EPS)
    return (y * g.astype(jnp.float32)).astype(x.dtype)


def rope_tables(positions):
    inv = ROPE_THETA ** (-jnp.arange(0, ROT_DIM, 2, dtype=jnp.float32) / ROT_DIM)
    ang = positions.astype(jnp.float32)[..., None] * inv
    return jnp.cos(ang)[:, :, None, :], jnp.sin(ang)[:, :, None, :]


def partial_rope(x, cos, sin):
    xf = x.astype(jnp.float32)
    half = ROT_DIM // 2
    x1, x2, rest = xf[..., :half], xf[..., half:ROT_DIM], xf[..., ROT_DIM:]
    out = jnp.concatenate([x1 * cos - x2 * sin, x2 * cos + x1 * sin, rest], axis=-1)
    return out.astype(x.dtype)


def qk_prep(t, g, cos, sin):
    return partial_rope(rms_norm(t, g), cos, sin)


def banded_attention(q, k, v, max_rel, sinks=None):
    B, L, H, Dh = q.shape
    Hk = k.shape[2]
    rep = H // Hk
    nb = L // BLK
    qb = q.reshape(B, nb, BLK, Hk, rep, Dh)
    kb = k.reshape(B, nb, BLK, Hk, Dh)
    vb = v.reshape(B, nb, BLK, Hk, Dh)
    pad = ((0, 0), (1, 0), (0, 0), (0, 0), (0, 0))
    kk = jnp.concatenate([jnp.pad(kb, pad)[:, :-1], kb], axis=2)
    vv = jnp.concatenate([jnp.pad(vb, pad)[:, :-1], vb], axis=2)
    s = jnp.einsum('bnqgrd,bnkgd->bngrqk', qb, kk,
                   preferred_element_type=jnp.float32) * (Dh ** -0.5)
    i = jnp.arange(BLK)[:, None]
    j = jnp.arange(2 * BLK)[None, :]
    rel = BLK + i - j
    kpos = jnp.arange(nb)[:, None, None] * BLK + j[None] - BLK
    mask = ((rel >= 0) & (rel <= max_rel))[None] & (kpos >= 0)
    s = jnp.where(mask[None, :, None, None], s, -jnp.inf)
    m = jnp.max(s, axis=-1, keepdims=True)
    if sinks is not None:
        sk = sinks.astype(jnp.float32).reshape(1, 1, Hk, rep, 1, 1)
        m = jnp.maximum(m, sk)
    p = jnp.exp(s - m)
    denom = jnp.sum(p, axis=-1, keepdims=True)
    if sinks is not None:
        denom = denom + jnp.exp(sk - m)
    o = jnp.einsum('bngrqk,bnkgd->bnqgrd', (p / denom).astype(v.dtype), vv)
    lse = (m + jnp.log(denom))[..., 0]
    lse = lse.transpose(0, 1, 4, 2, 3).reshape(B, L, H)
    return o.reshape(B, L, H, Dh), lse


def dilated_attention(q, k, v):
    B, S, _, Dh = q.shape
    outs, lses = [], []
    for g, (window, dil) in enumerate(DIL_GROUPS):
        hs = slice(g * DIL_HPG, (g + 1) * DIL_HPG)
        L = S // dil
        Lp = -(-L // BLK) * BLK

        def to_sub(t):
            t = t[:, :, hs].reshape(B, L, dil, DIL_HPG, Dh).transpose(0, 2, 1, 3, 4)
            t = t.reshape(B * dil, L, DIL_HPG, Dh)
            return jnp.pad(t, ((0, 0), (0, Lp - L), (0, 0), (0, 0)))

        o, lse = banded_attention(to_sub(q), to_sub(k), to_sub(v), window // dil)
        o = o[:, :L].reshape(B, dil, L, DIL_HPG, Dh).transpose(0, 2, 1, 3, 4).reshape(B, S, DIL_HPG, Dh)
        lse = lse[:, :L].reshape(B, dil, L, DIL_HPG).transpose(0, 2, 1, 3).reshape(B, S, DIL_HPG)
        outs.append(o)
        lses.append(lse)
    alpha = jax.nn.softmax(jnp.stack(lses), axis=0)
    return jnp.einsum('gbsh,gbshd->bshd', alpha.astype(q.dtype), jnp.stack(outs))


def spatial_gating(u, v, g_norm, w_s, b_s):
    B, S, _ = u.shape
    u = jax.nn.gelu(u)
    v = rms_norm(jax.nn.gelu(v), g_norm)
    vc = v.reshape(B, S // SGU_CHUNK, SGU_CHUNK, SGU_GROUPS, HEAD_DIM)
    w = w_s * jnp.tril(jnp.ones((SGU_CHUNK, SGU_CHUNK), dtype=w_s.dtype))
    s = jnp.einsum('gts,bcsgd->bctgd', w, vc) + b_s.T[:, :, None]
    return u * s.reshape(B, S, SGU_WIDTH)


def even_mixer(h, cos, sin, w_in, w_out, sgu_norm, sgu_w, sgu_b, qn, kn):
    B, S, _ = h.shape
    u, v, q, k, val = jnp.split(h @ w_in, EVEN_SPLITS, axis=-1)
    a = spatial_gating(u, v, sgu_norm, sgu_w, sgu_b)
    q = qk_prep(q.reshape(B, S, DIL_HEADS, HEAD_DIM), qn, cos, sin)
    k = qk_prep(k.reshape(B, S, DIL_HEADS, HEAD_DIM), kn, cos, sin)
    b = dilated_attention(q, k, val.reshape(B, S, DIL_HEADS, HEAD_DIM)).reshape(B, S, -1)
    return jnp.concatenate([a, b], axis=-1) @ w_out


def odd_mixer(h, cos, sin, w_in, w_out, qn, kn, sinks):
    B, S, _ = h.shape
    q, k, v = jnp.split(h @ w_in, ODD_SPLITS, axis=-1)
    q = qk_prep(q.reshape(B, S, SWA_Q_HEADS, HEAD_DIM), qn, cos, sin)
    k = qk_prep(k.reshape(B, S, SWA_KV_HEADS, HEAD_DIM), kn, cos, sin)
    v = v.reshape(B, S, SWA_KV_HEADS, HEAD_DIM)
    o, _ = banded_attention(q, k, v, SWA_WINDOW - 1, sinks)
    return o.reshape(B, S, -1) @ w_out


def hier_moe(h, w_rg, b_rg, w_re, b_re, w_gate, w_up, w_down):
    B, S, D = h.shape
    T = B * S
    x = h.reshape(T, D)
    xf = x.astype(jnp.float32)
    lg = xf @ w_rg.astype(jnp.float32) + b_rg.astype(jnp.float32)
    gi = jnp.argmax(lg, axis=-1)
    p_top = jnp.take_along_axis(jax.nn.softmax(lg, axis=-1), gi[:, None], axis=-1)
    le = jnp.einsum('td,gde->tge', xf, w_re.astype(jnp.float32)) + b_re.astype(jnp.float32)
    idx = jnp.broadcast_to(gi[:, None, None], (T, 1, EXPERTS_PER_GROUP))
    le = jnp.take_along_axis(le, idx, axis=1)[:, 0]
    top_p, top_i = lax.top_k(jax.nn.softmax(le, axis=-1), TOP_K)
    gate = p_top * top_p / jnp.sum(top_p, axis=-1, keepdims=True)
    eid = gi[:, None] * EXPERTS_PER_GROUP + top_i

    P = T * TOP_K
    flat_e = eid.reshape(-1)
    flat_tok = jnp.repeat(jnp.arange(T, dtype=jnp.int32), TOP_K)
    order = jnp.argsort(flat_e)
    se, stok, sw = flat_e[order], flat_tok[order], gate.reshape(-1)[order]
    counts = jnp.bincount(flat_e, length=N_EXPERTS)
    padded = (counts + MOE_BLOCK - 1) // MOE_BLOCK * MOE_BLOCK
    pad_end = jnp.cumsum(padded)
    pad_start = pad_end - padded
    start = jnp.cumsum(counts) - counts
    dest = pad_start[se] + (jnp.arange(P) - start[se])
    R = P + N_EXPERTS * MOE_BLOCK
    nblk = R // MOE_BLOCK
    buf = jnp.zeros((R, D), x.dtype).at[dest].set(x[stok])
    blk_e = jnp.clip(jnp.searchsorted(pad_end, jnp.arange(nblk) * MOE_BLOCK, side='right'),
                     0, N_EXPERTS - 1)

    def run_block(args):
        xb, e = args
        hid = jax.nn.silu(xb @ w_gate[e]) * (xb @ w_up[e])
        return hid @ w_down[e]

    out = lax.map(run_block, (buf.reshape(nblk, MOE_BLOCK, D), blk_e)).reshape(R, D)
    y = jax.ops.segment_sum(out[dest] * sw[:, None].astype(out.dtype), stok, num_segments=T)
    return y.reshape(B, S, D)


def setup_inputs(seed: int = 0) -> dict:
    key = jax.random.key(seed)
    ks = jax.random.split(key, 24)
    f32 = jnp.float32

    def nrm(k, shape, scale):
        return jax.random.normal(k, shape, f32) * scale

    def gain(k, shape):
        return 1.0 + 0.02 * jax.random.normal(k, shape, f32)

    x = jax.random.normal(ks[0], (BATCH, SEQ, D_MODEL), f32)
    positions = jnp.broadcast_to(jnp.arange(SEQ, dtype=jnp.int32), (BATCH, SEQ))
    return {
        "x": x,
        "positions": positions,
        "norm_mix": gain(ks[1], (DEPTH, D_MODEL)),
        "norm_ffn": gain(ks[2], (DEPTH, D_MODEL)),
        "w_in_even": nrm(ks[3], (N_EVEN, D_MODEL, IN_EVEN), D_MODEL ** -0.5),
        "w_out_even": nrm(ks[4], (N_EVEN, OUT_EVEN, D_MODEL), OUT_EVEN ** -0.5),
        "sgu_norm": gain(ks[5], (N_EVEN, SGU_WIDTH)),
        "sgu_w": nrm(ks[6], (N_EVEN, SGU_GROUPS, SGU_CHUNK, SGU_CHUNK), SGU_CHUNK ** -0.5),
        "sgu_b": 1.0 + 0.1 * jax.random.normal(ks[7], (N_EVEN, SGU_GROUPS, SGU_CHUNK), f32),
        "qn_dil": gain(ks[8], (N_EVEN, HEAD_DIM)),
        "kn_dil": gain(ks[9], (N_EVEN, HEAD_DIM)),
        "w_in_odd": nrm(ks[10], (N_ODD, D_MODEL, IN_ODD), D_MODEL ** -0.5),
        "w_out_odd": nrm(ks[11], (N_ODD, OUT_ODD, D_MODEL), OUT_ODD ** -0.5),
        "qn_swa": gain(ks[12], (N_ODD, HEAD_DIM)),
        "kn_swa": gain(ks[13], (N_ODD, HEAD_DIM)),
        "sinks": nrm(ks[14], (N_ODD, SWA_Q_HEADS), 0.5),
        "w_router_g": nrm(ks[15], (DEPTH, D_MODEL, N_GROUPS), D_MODEL ** -0.5),
        "b_router_g": nrm(ks[16], (DEPTH, N_GROUPS), 0.01),
        "w_router_e": nrm(ks[17], (DEPTH, N_GROUPS, D_MODEL, EXPERTS_PER_GROUP), D_MODEL ** -0.5),
        "b_router_e": nrm(ks[18], (DEPTH, N_GROUPS, EXPERTS_PER_GROUP), 0.01),
        "w_gate": nrm(ks[19], (DEPTH, N_EXPERTS, D_MODEL, EXPERT_FF), D_MODEL ** -0.5),
        "w_up": nrm(ks[20], (DEPTH, N_EXPERTS, D_MODEL, EXPERT_FF), D_MODEL ** -0.5),
        "w_down": nrm(ks[21], (DEPTH, N_EXPERTS, EXPERT_FF, D_MODEL), EXPERT_FF ** -0.5),
    }


def reference(x, positions, norm_mix, norm_ffn, w_in_even, w_out_even, sgu_norm, sgu_w, sgu_b,
              qn_dil, kn_dil, w_in_odd, w_out_odd, qn_swa, kn_swa, sinks,
              w_router_g, b_router_g, w_router_e, b_router_e, w_gate, w_up, w_down):
    cos, sin = rope_tables(positions)
    h = x
    for layer in range(DEPTH):
        i = layer // 2
        hn = rms_norm(h, norm_mix[layer])
        if layer % 2 == 0:
            h = h + even_mixer(hn, cos, sin, w_in_even[i], w_out_even[i], sgu_norm[i],
                               sgu_w[i], sgu_b[i], qn_dil[i], kn_dil[i])
        else:
            h = h + odd_mixer(hn, cos, sin, w_in_odd[i], w_out_odd[i], qn_swa[i], kn_swa[i],
                              sinks[i])
        h = h + hier_moe(rms_norm(h, norm_ffn[layer]), w_router_g[layer], b_router_g[layer],
                         w_router_e[layer], b_router_e[layer], w_gate[layer], w_up[layer],
                         w_down[layer])
    return h
```

```python
import functools

import jax
import jax.numpy as jnp
from jax import lax
from jax.experimental import pallas as pl
from jax.experimental.pallas import tpu as pltpu

D_MODEL = 1024
HEAD_DIM = 64
BLK = 128
ROPE_THETA = 500000.0
ROT_DIM = HEAD_DIM // 4
HALF_ROT = ROT_DIM // 2
EPS = 1e-6
SGU_GROUPS = 8
SGU_WIDTH = SGU_GROUPS * HEAD_DIM
DIL_GROUPS = ((128, 1), (512, 4), (2048, 16))
DIL_HPG = 4
DIL_GW = DIL_HPG * HEAD_DIM
DIL_WIDTH = DIL_GW * len(DIL_GROUPS)
SWA_Q_HEADS = 16
SWA_KV_HEADS = 2
SWA_WINDOW = 128
N_GROUPS = 4
EPG = 8
N_EXPERTS = N_GROUPS * EPG
EXPERT_FF = 256
TOP_K = 2

LANES = 128
SUBLANES = 8
VMEM_LIMIT_BYTES = 48 * 1024 * 1024

TM = 256
TQ = 256
MOE_BM = 256
NEG = -1e30

ROUTER_COLS = LANES
ROUTER_E0 = SUBLANES


def _cparams(sem):
    return pltpu.CompilerParams(dimension_semantics=sem, vmem_limit_bytes=VMEM_LIMIT_BYTES)


def _rms(x, gain_row):
    return x * lax.rsqrt(jnp.mean(x * x, axis=-1, keepdims=True) + EPS) * gain_row


def _dot(a, b):
    return jnp.dot(a, b, preferred_element_type=jnp.float32)


def _dot_nt(a, b):
    return lax.dot_general(a, b, (((1,), (1,)), ((), ())), preferred_element_type=jnp.float32)


def _rope_kernel(pos_ref, inv_ref, sign_ref, cos_ref, sin_ref):
    ang = pos_ref[...].astype(jnp.float32) * inv_ref[...]
    cos_ref[...] = jnp.cos(ang)
    sin_ref[...] = jnp.sin(ang) * sign_ref[...]


def _rope_tables(positions):
    t = positions.size
    pos = positions.reshape(t, 1)
    inv = ROPE_THETA ** (-jnp.arange(0, ROT_DIM, 2, dtype=jnp.float32) / ROT_DIM)
    lane = jnp.arange(LANES) % HEAD_DIM
    inv_row = jnp.where(lane < ROT_DIM, inv[lane % HALF_ROT], 0.0).reshape(1, LANES)
    sign_row = jnp.where(lane < HALF_ROT, -1.0, 1.0).astype(jnp.float32).reshape(1, LANES)
    tm = 512
    return pl.pallas_call(
        _rope_kernel,
        grid=(t // tm,),
        in_specs=[pl.BlockSpec((tm, 1), lambda i: (i, 0)),
                  pl.BlockSpec((1, LANES), lambda i: (0, 0)),
                  pl.BlockSpec((1, LANES), lambda i: (0, 0))],
        out_specs=[pl.BlockSpec((tm, LANES), lambda i: (i, 0)),
                   pl.BlockSpec((tm, LANES), lambda i: (i, 0))],
        out_shape=[jax.ShapeDtypeStruct((t, LANES), jnp.float32)] * 2,
        compiler_params=_cparams(("arbitrary",)),
        name="rope_tables",
    )(pos, inv_row, sign_row)


def _head_norm_rope(x, ones_bd, gain_row, cos, sin, scale):
    tm, w = x.shape
    sq = x * x
    sq_hi = sq.astype(jnp.bfloat16)
    sq_lo = (sq - sq_hi.astype(jnp.float32)).astype(jnp.bfloat16)
    outs = []
    bd = ones_bd.shape[0]
    for c in range(w // bd):
        sl = slice(c * bd, (c + 1) * bd)
        ss = _dot(sq_hi[:, sl], ones_bd) + _dot(sq_lo[:, sl], ones_bd)
        xn = x[:, sl] * lax.rsqrt(ss * (1.0 / HEAD_DIM) + EPS) * gain_row[:, sl]
        for j in range(bd // LANES):
            xj = xn[:, j * LANES:(j + 1) * LANES]
            lane = lax.broadcasted_iota(jnp.int32, xj.shape, 1) % HEAD_DIM
            rot = jnp.where(lane < HALF_ROT,
                            pltpu.roll(xj, LANES - HALF_ROT, axis=1),
                            pltpu.roll(xj, HALF_ROT, axis=1))
            outs.append((xj * cos + rot * sin) * scale)
    return jnp.concatenate(outs, axis=1)


def _even_in_kernel(x_ref, g_ref, w_ref, cos_ref, sin_ref, sgn_ref, sgw_ref, sgb_ref,
                    qn_ref, kn_ref, bd_ref, a_ref, q0_ref, q1_ref, q2_ref):
    x = x_ref[...]
    hn = _rms(x, g_ref[...]).astype(jnp.bfloat16)
    tm = x.shape[0]

    u = _dot(hn, w_ref[:, 0:SGU_WIDTH])
    v = _dot(hn, w_ref[:, SGU_WIDTH:2 * SGU_WIDTH])
    gu = jax.nn.gelu(u)
    vn = _rms(jax.nn.gelu(v), sgn_ref[...]).astype(jnp.bfloat16)

    row = lax.broadcasted_iota(jnp.int32, (BLK, BLK), 0)
    col = lax.broadcasted_iota(jnp.int32, (BLK, BLK), 1)
    tril = row >= col
    lane_lo = lax.broadcasted_iota(jnp.int32, (BLK, LANES), 1) < HEAD_DIM
    wts = [jnp.where(tril, sgw_ref[g], 0.0).astype(jnp.bfloat16) for g in range(SGU_GROUPS)]
    for c in range(tm // BLK):
        rows = slice(c * BLK, (c + 1) * BLK)
        for p in range(SGU_GROUPS // 2):
            lanes = slice(p * LANES, (p + 1) * LANES)
            vp = vn[rows, lanes]
            s = jnp.where(lane_lo, _dot(wts[2 * p], vp), _dot(wts[2 * p + 1], vp))
            a_ref[rows, lanes] = (gu[rows, lanes] * (s + sgb_ref[:, lanes])).astype(a_ref.dtype)

    cos = cos_ref[...]
    sin = sin_ref[...]
    o0 = 2 * SGU_WIDTH
    q = _dot(hn, w_ref[:, o0:o0 + DIL_WIDTH])
    q = _head_norm_rope(q, bd_ref[...], qn_ref[...], cos, sin, HEAD_DIM ** -0.5)
    k = _dot(hn, w_ref[:, o0 + DIL_WIDTH:o0 + 2 * DIL_WIDTH])
    k = _head_norm_rope(k, bd_ref[...], kn_ref[...], cos, sin, 1.0)
    val = _dot(hn, w_ref[:, o0 + 2 * DIL_WIDTH:o0 + 3 * DIL_WIDTH])
    for g, o_ref in enumerate((q0_ref, q1_ref, q2_ref)):
        sl = slice(g * DIL_GW, (g + 1) * DIL_GW)
        o_ref[:, 0:DIL_GW] = q[:, sl].astype(o_ref.dtype)
        o_ref[:, DIL_GW:2 * DIL_GW] = k[:, sl].astype(o_ref.dtype)
        o_ref[:, 2 * DIL_GW:3 * DIL_GW] = val[:, sl].astype(o_ref.dtype)


def _const_spec(shape):
    nd = len(shape)
    return pl.BlockSpec(shape, lambda i: (0,) * nd)


def _even_in(x2, gain, w_in, cos, sin, sgu_norm, sgu_w, sgu_b, qn, kn, ones_bd):
    t = x2.shape[0]
    in_w = w_in.shape[1]
    sgb = jnp.repeat(sgu_b.T, HEAD_DIM, axis=1)
    qn_row = jnp.tile(qn, DIL_WIDTH // HEAD_DIM).reshape(1, DIL_WIDTH)
    kn_row = jnp.tile(kn, DIL_WIDTH // HEAD_DIM).reshape(1, DIL_WIDTH)
    row = lambda n: pl.BlockSpec((TM, n), lambda i: (i, 0))
    return pl.pallas_call(
        _even_in_kernel,
        grid=(t // TM,),
        in_specs=[row(D_MODEL), _const_spec((1, D_MODEL)), _const_spec((D_MODEL, in_w)),
                  row(LANES), row(LANES), _const_spec((1, SGU_WIDTH)),
                  _const_spec((SGU_GROUPS, BLK, BLK)), _const_spec((BLK, SGU_WIDTH)),
                  _const_spec((1, DIL_WIDTH)), _const_spec((1, DIL_WIDTH)),
                  _const_spec(ones_bd.shape)],
        out_specs=[row(SGU_WIDTH), row(3 * DIL_GW), row(3 * DIL_GW), row(3 * DIL_GW)],
        out_shape=[jax.ShapeDtypeStruct((t, SGU_WIDTH), jnp.bfloat16)]
        + [jax.ShapeDtypeStruct((t, 3 * DIL_GW), jnp.bfloat16)] * 3,
        compiler_params=_cparams(("arbitrary",)),
        name="even_in_proj",
    )(x2, gain.reshape(1, D_MODEL), w_in.astype(jnp.bfloat16), cos, sin,
      sgu_norm.reshape(1, SGU_WIDTH), sgu_w, sgb, qn_row, kn_row, ones_bd)


def _band_mask(max_rel, first):
    i = lax.broadcasted_iota(jnp.int32, (BLK, 2 * BLK), 0)
    j = lax.broadcasted_iota(jnp.int32, (BLK, 2 * BLK), 1)
    rel = BLK + i - j
    first_key = jnp.where(first, BLK, 0)
    return (rel >= 0) & (rel <= max_rel) & (j >= first_key)


def _softmax_pv(s, vv, sink_rows):
    m = jnp.max(s, axis=-1, keepdims=True)
    if sink_rows is not None:
        m = jnp.maximum(m, sink_rows)
    p = jnp.exp(s - m)
    denom = jnp.sum(p, axis=-1, keepdims=True)
    if sink_rows is not None:
        denom = denom + jnp.exp(sink_rows - m)
    o = _dot(p.astype(jnp.bfloat16), vv)
    return o, m, denom


def _dil_attn_kernel(q_ref, kc_ref, kp_ref, vc_ref, vp_ref, o_ref, lse_ref, *, max_rel):
    first_tile = pl.program_id(2) == 0
    lane_lo = lax.broadcasted_iota(jnp.int32, (BLK, LANES), 1) < HEAD_DIM
    zero = jnp.zeros((), jnp.bfloat16)
    for j in range(TQ // BLK):
        rows = slice(j * BLK, (j + 1) * BLK)
        if j == 0:
            kprev, vprev = kp_ref[...], vp_ref[...]
            mask = _band_mask(max_rel, first_tile)
        else:
            prows = slice((j - 1) * BLK, j * BLK)
            kprev, vprev = kc_ref[prows, :], vc_ref[prows, :]
            mask = _band_mask(max_rel, False)
        kk = jnp.concatenate([kprev, kc_ref[rows, :]], axis=0)
        vv = jnp.concatenate([vprev, vc_ref[rows, :]], axis=0)
        mask2 = jnp.concatenate([mask, mask], axis=0)
        for p in range(DIL_GW // LANES):
            lanes = slice(p * LANES, (p + 1) * LANES)
            qp = q_ref[rows, lanes]
            qs = jnp.concatenate([jnp.where(lane_lo, qp, zero), jnp.where(lane_lo, zero, qp)],
                                 axis=0)
            s = jnp.where(mask2, _dot_nt(qs, kk[:, lanes]), NEG)
            o, m, denom = _softmax_pv(s, vv[:, lanes], None)
            on = o / denom
            lse = m + jnp.log(denom)
            o_ref[rows, lanes] = jnp.where(lane_lo, on[:BLK], on[BLK:]).astype(o_ref.dtype)
            lse_ref[rows, lanes] = jnp.where(lane_lo, lse[:BLK], lse[BLK:])


def _dilated_group_attention(qkv, batch, seq, dil, window):
    sub = seq // dil
    a = qkv.reshape(batch, sub, dil * 3 * DIL_GW)
    nq = TQ // BLK
    cur = lambda part: pl.BlockSpec((None, TQ, DIL_GW), lambda b, r, i: (b, i, r * 3 + part))
    prev = lambda part: pl.BlockSpec(
        (None, BLK, DIL_GW), lambda b, r, i: (b, jnp.maximum(i * nq - 1, 0), r * 3 + part))
    out = pl.BlockSpec((None, TQ, DIL_GW), lambda b, r, i: (b, i, r))
    o, lse = pl.pallas_call(
        functools.partial(_dil_attn_kernel, max_rel=window // dil),
        grid=(batch, dil, sub // TQ),
        in_specs=[cur(0), cur(1), prev(1), cur(2), prev(2)],
        out_specs=[out, out],
        out_shape=[jax.ShapeDtypeStruct((batch, sub, dil * DIL_GW), jnp.bfloat16),
                   jax.ShapeDtypeStruct((batch, sub, dil * DIL_GW), jnp.float32)],
        compiler_params=_cparams(("arbitrary",) * 3),
        name=f"dilated_attn_d{dil}",
    )(a, a, a, a, a)
    t = batch * seq
    return o.reshape(t, DIL_GW), lse.reshape(t, DIL_GW)


def _swa_kernel(sink_ref, q_ref, kvc_ref, kvp_ref, o_ref):
    first_tile = pl.program_id(1) == 0
    lane_lo = lax.broadcasted_iota(jnp.int32, (BLK, LANES), 1) < HEAD_DIM
    zero = jnp.zeros((), jnp.bfloat16)
    rep = SWA_Q_HEADS // SWA_KV_HEADS
    kw = SWA_KV_HEADS * LANES
    for j in range(TQ // BLK):
        rows = slice(j * BLK, (j + 1) * BLK)
        if j == 0:
            kvprev = kvp_ref[...]
            mask = _band_mask(SWA_WINDOW - 1, first_tile)
        else:
            kvprev = kvc_ref[(j - 1) * BLK:j * BLK, :]
            mask = _band_mask(SWA_WINDOW - 1, False)
        kv = jnp.concatenate([kvprev, kvc_ref[rows, :]], axis=0)
        mask_all = jnp.concatenate([mask] * rep, axis=0)
        for g in range(SWA_KV_HEADS):
            kk = kv[:, g * LANES:(g + 1) * LANES]
            vv = kv[:, kw + g * LANES:kw + (g + 1) * LANES]
            parts, sinks = [], []
            for p in range(rep // 2):
                h0 = g * rep + 2 * p
                qp = q_ref[rows, h0 * HEAD_DIM:(h0 + 2) * HEAD_DIM]
                parts += [jnp.where(lane_lo, qp, zero), jnp.where(lane_lo, zero, qp)]
                sinks += [jnp.full((BLK, 1), sink_ref[h0], jnp.float32),
                          jnp.full((BLK, 1), sink_ref[h0 + 1], jnp.float32)]
            qs = jnp.concatenate(parts, axis=0)
            s = jnp.where(mask_all, _dot_nt(qs, kk), NEG)
            o, _, denom = _softmax_pv(s, vv, jnp.concatenate(sinks, axis=0))
            on = o / denom
            for p in range(rep // 2):
                h0 = g * rep + 2 * p
                o_ref[rows, h0 * HEAD_DIM:(h0 + 2) * HEAD_DIM] = jnp.where(
                    lane_lo, on[2 * p * BLK:(2 * p + 1) * BLK],
                    on[(2 * p + 1) * BLK:(2 * p + 2) * BLK]).astype(o_ref.dtype)


def _swa_attention(q, kv, sinks, batch, seq):
    q3 = q.reshape(batch, seq, q.shape[1])
    kv3 = kv.reshape(batch, seq, kv.shape[1])
    nq = TQ // BLK
    o = pl.pallas_call(
        _swa_kernel,
        grid_spec=pltpu.PrefetchScalarGridSpec(
            num_scalar_prefetch=1,
            grid=(batch, seq // TQ),
            in_specs=[pl.BlockSpec((None, TQ, q.shape[1]), lambda b, i, s: (b, i, 0)),
                      pl.BlockSpec((None, TQ, kv.shape[1]), lambda b, i, s: (b, i, 0)),
                      pl.BlockSpec((None, BLK, kv.shape[1]),
                                   lambda b, i, s: (b, jnp.maximum(i * nq - 1, 0), 0))],
            out_specs=pl.BlockSpec((None, TQ, q.shape[1]), lambda b, i, s: (b, i, 0))),
        out_shape=jax.ShapeDtypeStruct(q3.shape, jnp.bfloat16),
        compiler_params=_cparams(("arbitrary",) * 2),
        name="swa_attn",
    )(sinks.astype(jnp.float32), q3, kv3, kv3)
    return o.reshape(batch * seq, q.shape[1])


def _odd_in_kernel(x_ref, g_ref, w_ref, cos_ref, sin_ref, qn_ref, kn_ref, bd_ref, q_ref, kv_ref):
    hn = _rms(x_ref[...], g_ref[...]).astype(jnp.bfloat16)
    qw = SWA_Q_HEADS * HEAD_DIM
    kw = SWA_KV_HEADS * LANES
    cos = cos_ref[...]
    sin = sin_ref[...]
    q = _dot(hn, w_ref[:, 0:qw])
    q_ref[...] = _head_norm_rope(q, bd_ref[...], qn_ref[...], cos, sin,
                                 HEAD_DIM ** -0.5).astype(q_ref.dtype)
    k = _dot(hn, w_ref[:, qw:qw + kw])
    kv_ref[:, 0:kw] = _head_norm_rope(k, bd_ref[...], kn_ref[...], cos, sin,
                                      1.0).astype(kv_ref.dtype)
    kv_ref[:, kw:2 * kw] = _dot(hn, w_ref[:, qw + kw:qw + 2 * kw]).astype(kv_ref.dtype)


def _odd_in(h2, gain, w_in, cos, sin, qn, kn, ones_bd):
    t = h2.shape[0]
    qw = SWA_Q_HEADS * HEAD_DIM
    kvw = SWA_KV_HEADS * HEAD_DIM
    dup = lambda w: jnp.concatenate(
        [w[:, h * HEAD_DIM:(h + 1) * HEAD_DIM] for h in range(SWA_KV_HEADS) for _ in range(2)],
        axis=1)
    w_all = jnp.concatenate([w_in[:, :qw], dup(w_in[:, qw:qw + kvw]),
                             dup(w_in[:, qw + kvw:qw + 2 * kvw])], axis=1).astype(jnp.bfloat16)
    kw = SWA_KV_HEADS * LANES
    qn_row = jnp.tile(qn, qw // HEAD_DIM).reshape(1, qw)
    kn_row = jnp.tile(kn, kw // HEAD_DIM).reshape(1, kw)
    row = lambda n: pl.BlockSpec((TM, n), lambda i: (i, 0))
    return pl.pallas_call(
        _odd_in_kernel,
        grid=(t // TM,),
        in_specs=[row(D_MODEL), _const_spec((1, D_MODEL)), _const_spec(w_all.shape),
                  row(LANES), row(LANES), _const_spec((1, qw)), _const_spec((1, kw)),
                  _const_spec(ones_bd.shape)],
        out_specs=[row(qw), row(2 * kw)],
        out_shape=[jax.ShapeDtypeStruct((t, qw), jnp.bfloat16),
                   jax.ShapeDtypeStruct((t, 2 * kw), jnp.bfloat16)],
        compiler_params=_cparams(("arbitrary",)),
        name="odd_in_proj",
    )(h2, gain.reshape(1, D_MODEL), w_all, cos, sin, qn_row, kn_row, ones_bd)


def _route(hn, wr_hi_ref, wr_lo_ref, br_ref, carry_ref, idx_ref, gate_ref, cnt_ref):
    tm = hn.shape[0]
    h_hi = hn.astype(jnp.bfloat16)
    h_lo = (hn - h_hi.astype(jnp.float32)).astype(jnp.bfloat16)
    logits = (_dot(h_hi, wr_hi_ref[...]) + _dot(h_lo, wr_hi_ref[...])
              + _dot(h_hi, wr_lo_ref[...]) + br_ref[...])
    lt = logits.T
    rowf = lax.broadcasted_iota(jnp.int32, (SUBLANES, tm), 0).astype(jnp.float32)

    def first_argmax(x):
        m = jnp.max(x, axis=0, keepdims=True)
        idx = jnp.min(jnp.where(x == m, rowf, float(SUBLANES)), axis=0, keepdims=True)
        return m, idx

    lg = jnp.where(rowf < N_GROUPS, lt[0:SUBLANES], NEG)
    mg, gi = first_argmax(lg)
    p_top = 1.0 / jnp.sum(jnp.exp(lg - mg), axis=0, keepdims=True)
    le = lt[ROUTER_E0:ROUTER_E0 + EPG]
    for g in range(1, N_GROUPS):
        le = jnp.where(gi == float(g), lt[ROUTER_E0 + g * EPG:ROUTER_E0 + (g + 1) * EPG], le)
    m1, i1 = first_argmax(le)
    m2, i2 = first_argmax(jnp.where(rowf == i1, NEG, le))
    e2 = jnp.exp(m2 - m1)
    gate1 = p_top / (1.0 + e2)
    gate2 = p_top * e2 / (1.0 + e2)
    eid1 = gi * EPG + i1
    eid2 = gi * EPG + i2

    erow = lax.broadcasted_iota(jnp.int32, (N_EXPERTS, tm), 0).astype(jnp.float32)
    oh1 = erow == eid1
    oh2 = erow == eid2
    member = jnp.where(oh1 | oh2, 1.0, 0.0)
    s_idx = lax.broadcasted_iota(jnp.int32, (tm, tm), 0)
    t_idx = lax.broadcasted_iota(jnp.int32, (tm, tm), 1)
    upper = jnp.where(s_idx < t_idx, 1.0, 0.0).astype(jnp.bfloat16)
    before = carry_ref[:, 0:1] + _dot(member.astype(jnp.bfloat16), upper)
    rank1 = jnp.sum(jnp.where(oh1, before, 0.0), axis=0, keepdims=True)
    rank2 = jnp.sum(jnp.where(oh2, before, 0.0), axis=0, keepdims=True)
    carry_ref[...] = carry_ref[...] + jnp.sum(member, axis=1, keepdims=True)
    cnt_ref[...] = carry_ref[...].astype(jnp.int32)

    idx = jnp.where(rowf == 0.0, eid1, jnp.where(rowf == 1.0, eid2,
          jnp.where(rowf == 2.0, rank1, jnp.where(rowf == 3.0, rank2, 0.0))))
    idx_ref[...] = idx.astype(jnp.int32)
    gate_ref[...] = jnp.where(rowf == 0.0, gate1, jnp.where(rowf == 1.0, gate2, 0.0))


def _post_proj(y, x_ref, gf_ref, wr_hi_ref, wr_lo_ref, br_ref, carry_ref,
               h_ref, hn_ref, idx_ref, gate_ref, cnt_ref):
    @pl.when(pl.program_id(0) == 0)
    def _():
        carry_ref[...] = jnp.zeros_like(carry_ref)

    h = x_ref[...] + y
    h_ref[...] = h
    hn = _rms(h, gf_ref[...])
    hn_ref[...] = hn
    _route(hn, wr_hi_ref, wr_lo_ref, br_ref, carry_ref, idx_ref, gate_ref, cnt_ref)


def _even_out_kernel(a_ref, o0_ref, o1_ref, o2_ref, l0_ref, l1_ref, l2_ref, w_ref, x_ref, gf_ref,
                     wr_hi_ref, wr_lo_ref, br_ref, h_ref, hn_ref, idx_ref, gate_ref, cnt_ref,
                     carry_ref):
    l0, l1, l2 = l0_ref[...], l1_ref[...], l2_ref[...]
    m = jnp.maximum(jnp.maximum(l0, l1), l2)
    e0, e1, e2 = jnp.exp(l0 - m), jnp.exp(l1 - m), jnp.exp(l2 - m)
    b = (e0 * o0_ref[...].astype(jnp.float32) + e1 * o1_ref[...].astype(jnp.float32)
         + e2 * o2_ref[...].astype(jnp.float32)) / (e0 + e1 + e2)
    y = _dot(a_ref[...], w_ref[0:SGU_WIDTH, :]) + _dot(b.astype(jnp.bfloat16),
                                                       w_ref[SGU_WIDTH:, :])
    _post_proj(y, x_ref, gf_ref, wr_hi_ref, wr_lo_ref, br_ref, carry_ref,
               h_ref, hn_ref, idx_ref, gate_ref, cnt_ref)


def _odd_out_kernel(o_ref, w_ref, x_ref, gf_ref, wr_hi_ref, wr_lo_ref, br_ref,
                    h_ref, hn_ref, idx_ref, gate_ref, cnt_ref, carry_ref):
    y = _dot(o_ref[...], w_ref[...])
    _post_proj(y, x_ref, gf_ref, wr_hi_ref, wr_lo_ref, br_ref, carry_ref,
               h_ref, hn_ref, idx_ref, gate_ref, cnt_ref)


def _router_weights(w_rg, b_rg, w_re, b_re):
    d = w_rg.shape[0]
    w = jnp.zeros((d, ROUTER_COLS), jnp.float32)
    w = w.at[:, 0:N_GROUPS].set(w_rg)
    w = w.at[:, ROUTER_E0:ROUTER_E0 + N_EXPERTS].set(
        jnp.transpose(w_re, (1, 0, 2)).reshape(d, N_EXPERTS))
    b = jnp.zeros((1, ROUTER_COLS), jnp.float32)
    b = b.at[0, 0:N_GROUPS].set(b_rg)
    b = b.at[0, ROUTER_E0:ROUTER_E0 + N_EXPERTS].set(b_re.reshape(N_EXPERTS))
    w_hi = w.astype(jnp.bfloat16)
    w_lo = (w - w_hi.astype(jnp.float32)).astype(jnp.bfloat16)
    return w_hi, w_lo, b


def _out_proj(kernel, acts, w_out, x2, gain_ffn, router):
    t = x2.shape[0]
    w_hi, w_lo, b = router
    row = lambda n: pl.BlockSpec((TM, n), lambda i: (i, 0))
    colblk = pl.BlockSpec((SUBLANES, TM), lambda i: (0, i))
    return pl.pallas_call(
        kernel,
        grid=(t // TM,),
        in_specs=[row(a.shape[1]) for a in acts]
        + [_const_spec(w_out.shape), row(D_MODEL), _const_spec((1, D_MODEL)),
           _const_spec(w_hi.shape), _const_spec(w_lo.shape), _const_spec(b.shape)],
        out_specs=[row(D_MODEL), row(D_MODEL), colblk, colblk,
                   _const_spec((N_EXPERTS, LANES))],
        out_shape=[jax.ShapeDtypeStruct((t, D_MODEL), jnp.float32),
                   jax.ShapeDtypeStruct((t, D_MODEL), jnp.float32),
                   jax.ShapeDtypeStruct((SUBLANES, t), jnp.int32),
                   jax.ShapeDtypeStruct((SUBLANES, t), jnp.float32),
                   jax.ShapeDtypeStruct((N_EXPERTS, LANES), jnp.int32)],
        scratch_shapes=[pltpu.VMEM((N_EXPERTS, LANES), jnp.float32)],
        compiler_params=_cparams(("arbitrary",)),
        name=kernel.__name__.strip("_"),
    )(*acts, w_out.astype(jnp.bfloat16), x2, gain_ffn.reshape(1, D_MODEL), w_hi, w_lo, b)


def _row_copy(src, src_row, dst, dst_row, sem):
    return pltpu.make_async_copy(src.at[pl.ds(src_row, 1), :], dst.at[pl.ds(dst_row, 1), :], sem)


def _dispatch_kernel(pstart_ref, pend_ref, idx_ref, hn_ref, xs_ref, zero_ref, sem, zsem):
    def zero_block(row0):
        return pltpu.make_async_copy(
            zero_ref, xs_ref.at[pl.ds(pl.multiple_of(row0, MOE_BM), MOE_BM), :], zsem)

    @pl.when(pl.program_id(0) == 0)
    def _():
        zero_ref[...] = jnp.zeros_like(zero_ref)
        for e in range(N_EXPERTS):
            @pl.when(pend_ref[e] > pstart_ref[e])
            def _():
                zero_block(pend_ref[e] - MOE_BM).start()
        for e in range(N_EXPERTS):
            @pl.when(pend_ref[e] > pstart_ref[e])
            def _():
                zero_block(pend_ref[e] - MOE_BM).wait()
        first_unused = lax.div(pend_ref[N_EXPERTS - 1], MOE_BM)
        n_blocks = xs_ref.shape[0] // MOE_BM

        def start_tail(b, _):
            zero_block(b * MOE_BM).start()
            return 0

        def wait_tail(b, _):
            zero_block(b * MOE_BM).wait()
            return 0

        lax.fori_loop(first_unused, n_blocks, start_tail, 0)
        lax.fori_loop(first_unused, n_blocks, wait_tail, 0)

    tm = hn_ref.shape[0]

    def issue(t, _):
        for k in range(TOP_K):
            dst = pstart_ref[idx_ref[k, t]] + idx_ref[TOP_K + k, t]
            _row_copy(hn_ref, t, xs_ref, dst, sem).start()
        return 0

    lax.fori_loop(0, tm, issue, 0, unroll=8)

    def drain(t, _):
        for k in range(TOP_K):
            _row_copy(hn_ref, 0, xs_ref, 0, sem).wait()
        return 0

    lax.fori_loop(0, tm, drain, 0, unroll=8)


def _dispatch(hn, idx_tiles, pstart, pend, rows):
    t = hn.shape[0]
    return pl.pallas_call(
        _dispatch_kernel,
        grid_spec=pltpu.PrefetchScalarGridSpec(
            num_scalar_prefetch=2,
            grid=(t // TM,),
            in_specs=[pl.BlockSpec((None, SUBLANES, TM), lambda i, ps, pe: (i, 0, 0),
                                   memory_space=pltpu.SMEM),
                      pl.BlockSpec((TM, D_MODEL), lambda i, ps, pe: (i, 0))],
            out_specs=pl.BlockSpec(memory_space=pl.ANY),
            scratch_shapes=[pltpu.VMEM((MOE_BM, D_MODEL), jnp.float32),
                            pltpu.SemaphoreType.DMA(()), pltpu.SemaphoreType.DMA(())]),
        out_shape=jax.ShapeDtypeStruct((rows, D_MODEL), jnp.float32),
        compiler_params=_cparams(("arbitrary",)),
        name="moe_dispatch",
    )(pstart, pend, idx_tiles, hn)


def _expert_kernel(blk_e_ref, nused_ref, xs_ref, wg_ref, wu_ref, wd_ref, out_ref):
    i = pl.program_id(0)

    @pl.when(i < nused_ref[0])
    def _():
        x = xs_ref[...].astype(jnp.bfloat16)
        hid = jax.nn.silu(_dot(x, wg_ref[...])) * _dot(x, wu_ref[...])
        out_ref[...] = _dot(hid.astype(jnp.bfloat16), wd_ref[...])

    @pl.when(i >= nused_ref[0])
    def _():
        out_ref[...] = jnp.zeros_like(out_ref)


def _expert_ffn(xs, blk_e, nused, w_gate, w_up, w_down):
    rows = xs.shape[0]
    wspec = lambda shape: pl.BlockSpec((None,) + shape, lambda i, be, nu: (be[i], 0, 0))
    return pl.pallas_call(
        _expert_kernel,
        grid_spec=pltpu.PrefetchScalarGridSpec(
            num_scalar_prefetch=2,
            grid=(rows // MOE_BM,),
            in_specs=[pl.BlockSpec((MOE_BM, D_MODEL),
                                   lambda i, be, nu: (jnp.minimum(i, nu[0] - 1), 0)),
                      wspec((D_MODEL, EXPERT_FF)), wspec((D_MODEL, EXPERT_FF)),
                      wspec((EXPERT_FF, D_MODEL))],
            out_specs=pl.BlockSpec((MOE_BM, D_MODEL), lambda i, be, nu: (i, 0))),
        out_shape=jax.ShapeDtypeStruct((rows, D_MODEL), jnp.float32),
        compiler_params=_cparams(("arbitrary",)),
        name="moe_expert_ffn",
    )(blk_e, nused, xs, w_gate.astype(jnp.bfloat16), w_up.astype(jnp.bfloat16),
      w_down.astype(jnp.bfloat16))


def _combine_kernel(pstart_ref, idx_ref, h_ref, gate_ref, ys_ref, out_ref, buf_ref, sem):
    tm = h_ref.shape[0]

    def issue(t, _):
        for k in range(TOP_K):
            src = pstart_ref[idx_ref[k, t]] + idx_ref[TOP_K + k, t]
            _row_copy(ys_ref, src, buf_ref.at[k], t, sem).start()
        return 0

    lax.fori_loop(0, tm, issue, 0, unroll=8)

    def drain(t, _):
        for k in range(TOP_K):
            _row_copy(ys_ref, 0, buf_ref.at[k], 0, sem).wait()
        return 0

    lax.fori_loop(0, tm, drain, 0, unroll=8)
    g = gate_ref[...]
    out_ref[...] = h_ref[...] + (buf_ref[0] * g[:, 0:1] + buf_ref[1] * g[:, 1:2])


def _combine(h, gate_cols, ys, idx_tiles, pstart):
    t = h.shape[0]
    return pl.pallas_call(
        _combine_kernel,
        grid_spec=pltpu.PrefetchScalarGridSpec(
            num_scalar_prefetch=1,
            grid=(t // TM,),
            in_specs=[pl.BlockSpec((None, SUBLANES, TM), lambda i, ps: (i, 0, 0),
                                   memory_space=pltpu.SMEM),
                      pl.BlockSpec((TM, D_MODEL), lambda i, ps: (i, 0)),
                      pl.BlockSpec((TM, SUBLANES), lambda i, ps: (i, 0)),
                      pl.BlockSpec(memory_space=pl.ANY)],
            out_specs=pl.BlockSpec((TM, D_MODEL), lambda i, ps: (i, 0)),
            scratch_shapes=[pltpu.VMEM((TOP_K, TM, D_MODEL), jnp.float32),
                            pltpu.SemaphoreType.DMA(())]),
        out_shape=jax.ShapeDtypeStruct((t, D_MODEL), jnp.float32),
        compiler_params=_cparams(("arbitrary",)),
        name="moe_combine",
    )(pstart, idx_tiles, h, gate_cols, ys)


def _moe(h, hn, idx, gates, counts, w_gate, w_up, w_down):
    t = h.shape[0]
    rows = t * TOP_K + N_EXPERTS * MOE_BM
    nblk = rows // MOE_BM
    cnt = counts[:, 0]
    padded = (cnt + MOE_BM - 1) // MOE_BM * MOE_BM
    pend = jnp.cumsum(padded).astype(jnp.int32)
    pstart = pend - padded
    blk_row0 = jnp.arange(nblk, dtype=jnp.int32) * MOE_BM
    blk_e = jnp.minimum(jnp.sum(pend[None, :] <= blk_row0[:, None], axis=1),
                        N_EXPERTS - 1).astype(jnp.int32)
    nused = (pend[-1:] // MOE_BM).astype(jnp.int32)
    idx_tiles = idx.reshape(SUBLANES, t // TM, TM).transpose(1, 0, 2)
    xs = _dispatch(hn, idx_tiles, pstart, pend, rows)
    ys = _expert_ffn(xs, blk_e, nused, w_gate, w_up, w_down)
    return _combine(h, gates.T, ys, idx_tiles, pstart)


def kernel(x, positions, norm_mix, norm_ffn, w_in_even, w_out_even, sgu_norm, sgu_w, sgu_b,
           qn_dil, kn_dil, w_in_odd, w_out_odd, qn_swa, kn_swa, sinks,
           w_router_g, b_router_g, w_router_e, b_router_e, w_gate, w_up, w_down):
    batch, seq, d = x.shape
    t = batch * seq
    depth = norm_mix.shape[0]
    cos, sin = _rope_tables(positions)
    bd = 2 * LANES
    ones_bd = (jnp.arange(bd)[:, None] // HEAD_DIM == jnp.arange(bd)[None, :] // HEAD_DIM
               ).astype(jnp.bfloat16)
    h = x.reshape(t, d)
    for layer in range(depth):
        i = layer // 2
        router = _router_weights(w_router_g[layer], b_router_g[layer], w_router_e[layer],
                                 b_router_e[layer])
        if layer % 2 == 0:
            a, *qkv = _even_in(h, norm_mix[layer], w_in_even[i], cos, sin, sgu_norm[i],
                               sgu_w[i], sgu_b[i], qn_dil[i], kn_dil[i], ones_bd)
            outs, lses = [], []
            for g, (window, dil) in enumerate(DIL_GROUPS):
                o, lse = _dilated_group_attention(qkv[g], batch, seq, dil, window)
                outs.append(o)
                lses.append(lse)
            res = _out_proj(_even_out_kernel, [a] + outs + lses, w_out_even[i], h,
                            norm_ffn[layer], router)
        else:
            q, kv = _odd_in(h, norm_mix[layer], w_in_odd[i], cos, sin, qn_swa[i], kn_swa[i],
                            ones_bd)
            o = _swa_attention(q, kv, sinks[i], batch, seq)
            res = _out_proj(_odd_out_kernel, [o], w_out_odd[i], h, norm_ffn[layer], router)
        h_mid, hn, idx, gates, counts = res
        h = _moe(h_mid, hn, idx, gates, counts, w_gate[layer], w_up[layer], w_down[layer])
    return h.reshape(batch, seq, d)
```

```python
import functools

import jax
import jax.numpy as jnp
from jax import lax
from jax.experimental import pallas as pl
from jax.experimental.pallas import tpu as pltpu

D_MODEL = 1024
HEAD_DIM = 64
BLK = 128
ROPE_THETA = 500000.0
ROT_DIM = HEAD_DIM // 4
HALF_ROT = ROT_DIM // 2
EPS = 1e-6
SGU_GROUPS = 8
SGU_WIDTH = SGU_GROUPS * HEAD_DIM
DIL_GROUPS = ((128, 1), (512, 4), (2048, 16))
DIL_HPG = 4
DIL_GW = DIL_HPG * HEAD_DIM
DIL_WIDTH = DIL_GW * len(DIL_GROUPS)
SWA_Q_HEADS = 16
SWA_KV_HEADS = 2
SWA_WINDOW = 128
N_GROUPS = 4
EPG = 8
N_EXPERTS = N_GROUPS * EPG
EXPERT_FF = 256
TOP_K = 2

LANES = 128
SUBLANES = 8
VMEM_LIMIT_BYTES = 48 * 1024 * 1024

TM = 256
TQ = 256
MOE_BM = 256
NEG = -1e30

ROW_CHUNKS = D_MODEL // LANES
ROUTER_COLS = LANES
ROUTER_E0 = SUBLANES


def _cparams(sem):
    return pltpu.CompilerParams(dimension_semantics=sem, vmem_limit_bytes=VMEM_LIMIT_BYTES)


def _rms(x, gain_row):
    return x * lax.rsqrt(jnp.mean(x * x, axis=-1, keepdims=True) + EPS) * gain_row


def _dot(a, b):
    return jnp.dot(a, b, preferred_element_type=jnp.float32)


def _dot_nt(a, b):
    return lax.dot_general(a, b, (((1,), (1,)), ((), ())), preferred_element_type=jnp.float32)


def _rope_kernel(pos_ref, inv_ref, sign_ref, cos_ref, sin_ref):
    ang = pos_ref[...].astype(jnp.float32) * inv_ref[...]
    cos_ref[...] = jnp.cos(ang)
    sin_ref[...] = jnp.sin(ang) * sign_ref[...]


def _rope_tables(positions):
    t = positions.size
    pos = positions.reshape(t, 1)
    inv = ROPE_THETA ** (-jnp.arange(0, ROT_DIM, 2, dtype=jnp.float32) / ROT_DIM)
    lane = jnp.arange(LANES) % HEAD_DIM
    inv_row = jnp.where(lane < ROT_DIM, inv[lane % HALF_ROT], 0.0).reshape(1, LANES)
    sign_row = jnp.where(lane < HALF_ROT, -1.0, 1.0).astype(jnp.float32).reshape(1, LANES)
    tm = 512
    return pl.pallas_call(
        _rope_kernel,
        grid=(t // tm,),
        in_specs=[pl.BlockSpec((tm, 1), lambda i: (i, 0)),
                  pl.BlockSpec((1, LANES), lambda i: (0, 0)),
                  pl.BlockSpec((1, LANES), lambda i: (0, 0))],
        out_specs=[pl.BlockSpec((tm, LANES), lambda i: (i, 0)),
                   pl.BlockSpec((tm, LANES), lambda i: (i, 0))],
        out_shape=[jax.ShapeDtypeStruct((t, LANES), jnp.float32)] * 2,
        compiler_params=_cparams(("arbitrary",)),
        name="rope_tables",
    )(pos, inv_row, sign_row)


def _head_norm_rope(x, ones_bd, gain_row, cos, sin, scale):
    tm, w = x.shape
    sq = x * x
    sq_hi = sq.astype(jnp.bfloat16)
    sq_lo = (sq - sq_hi.astype(jnp.float32)).astype(jnp.bfloat16)
    outs = []
    bd = ones_bd.shape[0]
    for c in range(w // bd):
        sl = slice(c * bd, (c + 1) * bd)
        ss = _dot(sq_hi[:, sl], ones_bd) + _dot(sq_lo[:, sl], ones_bd)
        xn = x[:, sl] * lax.rsqrt(ss * (1.0 / HEAD_DIM) + EPS) * gain_row[:, sl]
        for j in range(bd // LANES):
            xj = xn[:, j * LANES:(j + 1) * LANES]
            lane = lax.broadcasted_iota(jnp.int32, xj.shape, 1) % HEAD_DIM
            rot = jnp.where(lane < HALF_ROT,
                            pltpu.roll(xj, LANES - HALF_ROT, axis=1),
                            pltpu.roll(xj, HALF_ROT, axis=1))
            outs.append((xj * cos + rot * sin) * scale)
    return jnp.concatenate(outs, axis=1)


def _even_in_kernel(x_ref, g_ref, w_ref, cos_ref, sin_ref, sgn_ref, sgw_ref, sgb_ref,
                    qn_ref, kn_ref, bd_ref, a_ref, q0_ref, q1_ref, q2_ref):
    x = x_ref[...]
    hn = _rms(x, g_ref[...]).astype(jnp.bfloat16)
    tm = x.shape[0]

    u = _dot(hn, w_ref[:, 0:SGU_WIDTH])
    v = _dot(hn, w_ref[:, SGU_WIDTH:2 * SGU_WIDTH])
    gu = jax.nn.gelu(u)
    vn = _rms(jax.nn.gelu(v), sgn_ref[...]).astype(jnp.bfloat16)

    row = lax.broadcasted_iota(jnp.int32, (BLK, BLK), 0)
    col = lax.broadcasted_iota(jnp.int32, (BLK, BLK), 1)
    tril = row >= col
    lane_lo = lax.broadcasted_iota(jnp.int32, (BLK, LANES), 1) < HEAD_DIM
    wts = [jnp.where(tril, sgw_ref[g], 0.0).astype(jnp.bfloat16) for g in range(SGU_GROUPS)]
    for c in range(tm // BLK):
        rows = slice(c * BLK, (c + 1) * BLK)
        for p in range(SGU_GROUPS // 2):
            lanes = slice(p * LANES, (p + 1) * LANES)
            vp = vn[rows, lanes]
            s = jnp.where(lane_lo, _dot(wts[2 * p], vp), _dot(wts[2 * p + 1], vp))
            a_ref[rows, lanes] = (gu[rows, lanes] * (s + sgb_ref[:, lanes])).astype(a_ref.dtype)

    cos = cos_ref[...]
    sin = sin_ref[...]
    o0 = 2 * SGU_WIDTH
    q = _dot(hn, w_ref[:, o0:o0 + DIL_WIDTH])
    q = _head_norm_rope(q, bd_ref[...], qn_ref[...], cos, sin, HEAD_DIM ** -0.5)
    k = _dot(hn, w_ref[:, o0 + DIL_WIDTH:o0 + 2 * DIL_WIDTH])
    k = _head_norm_rope(k, bd_ref[...], kn_ref[...], cos, sin, 1.0)
    val = _dot(hn, w_ref[:, o0 + 2 * DIL_WIDTH:o0 + 3 * DIL_WIDTH])
    for g, o_ref in enumerate((q0_ref, q1_ref, q2_ref)):
        sl = slice(g * DIL_GW, (g + 1) * DIL_GW)
        o_ref[:, 0:DIL_GW] = q[:, sl].astype(o_ref.dtype)
        o_ref[:, DIL_GW:2 * DIL_GW] = k[:, sl].astype(o_ref.dtype)
        o_ref[:, 2 * DIL_GW:3 * DIL_GW] = val[:, sl].astype(o_ref.dtype)


def _const_spec(shape):
    nd = len(shape)
    return pl.BlockSpec(shape, lambda i: (0,) * nd)


def _even_in(x2, gain, w_in, cos, sin, sgu_norm, sgu_w, sgu_b, qn, kn, ones_bd):
    t = x2.shape[0]
    in_w = w_in.shape[1]
    sgb = jnp.repeat(sgu_b.T, HEAD_DIM, axis=1)
    qn_row = jnp.tile(qn, DIL_WIDTH // HEAD_DIM).reshape(1, DIL_WIDTH)
    kn_row = jnp.tile(kn, DIL_WIDTH // HEAD_DIM).reshape(1, DIL_WIDTH)
    row = lambda n: pl.BlockSpec((TM, n), lambda i: (i, 0))
    return pl.pallas_call(
        _even_in_kernel,
        grid=(t // TM,),
        in_specs=[row(D_MODEL), _const_spec((1, D_MODEL)), _const_spec((D_MODEL, in_w)),
                  row(LANES), row(LANES), _const_spec((1, SGU_WIDTH)),
                  _const_spec((SGU_GROUPS, BLK, BLK)), _const_spec((BLK, SGU_WIDTH)),
                  _const_spec((1, DIL_WIDTH)), _const_spec((1, DIL_WIDTH)),
                  _const_spec(ones_bd.shape)],
        out_specs=[row(SGU_WIDTH), row(3 * DIL_GW), row(3 * DIL_GW), row(3 * DIL_GW)],
        out_shape=[jax.ShapeDtypeStruct((t, SGU_WIDTH), jnp.bfloat16)]
        + [jax.ShapeDtypeStruct((t, 3 * DIL_GW), jnp.bfloat16)] * 3,
        compiler_params=_cparams(("arbitrary",)),
        name="even_in_proj",
    )(x2, gain.reshape(1, D_MODEL), w_in.astype(jnp.bfloat16), cos, sin,
      sgu_norm.reshape(1, SGU_WIDTH), sgu_w, sgb, qn_row, kn_row, ones_bd)


def _band_mask(max_rel, first):
    i = lax.broadcasted_iota(jnp.int32, (BLK, 2 * BLK), 0)
    j = lax.broadcasted_iota(jnp.int32, (BLK, 2 * BLK), 1)
    rel = BLK + i - j
    first_key = jnp.where(first, BLK, 0)
    return (rel >= 0) & (rel <= max_rel) & (j >= first_key)


def _softmax_pv(s, vv, sink_rows):
    m = jnp.max(s, axis=-1, keepdims=True)
    if sink_rows is not None:
        m = jnp.maximum(m, sink_rows)
    p = jnp.exp(s - m)
    denom = jnp.sum(p, axis=-1, keepdims=True)
    if sink_rows is not None:
        denom = denom + jnp.exp(sink_rows - m)
    o = _dot(p.astype(jnp.bfloat16), vv)
    return o, m, denom


def _dil_attn_kernel(q_ref, kc_ref, kp_ref, vc_ref, vp_ref, o_ref, lse_ref, *, max_rel):
    first_tile = pl.program_id(2) == 0
    lane_lo = lax.broadcasted_iota(jnp.int32, (BLK, LANES), 1) < HEAD_DIM
    zero = jnp.zeros((), jnp.bfloat16)
    for j in range(TQ // BLK):
        rows = slice(j * BLK, (j + 1) * BLK)
        if j == 0:
            kprev, vprev = kp_ref[...], vp_ref[...]
            mask = _band_mask(max_rel, first_tile)
        else:
            prows = slice((j - 1) * BLK, j * BLK)
            kprev, vprev = kc_ref[prows, :], vc_ref[prows, :]
            mask = _band_mask(max_rel, False)
        kk = jnp.concatenate([kprev, kc_ref[rows, :]], axis=0)
        vv = jnp.concatenate([vprev, vc_ref[rows, :]], axis=0)
        mask2 = jnp.concatenate([mask, mask], axis=0)
        for p in range(DIL_GW // LANES):
            lanes = slice(p * LANES, (p + 1) * LANES)
            qp = q_ref[rows, lanes]
            qs = jnp.concatenate([jnp.where(lane_lo, qp, zero), jnp.where(lane_lo, zero, qp)],
                                 axis=0)
            s = jnp.where(mask2, _dot_nt(qs, kk[:, lanes]), NEG)
            o, m, denom = _softmax_pv(s, vv[:, lanes], None)
            on = o / denom
            lse = m + jnp.log(denom)
            o_ref[rows, lanes] = jnp.where(lane_lo, on[:BLK], on[BLK:]).astype(o_ref.dtype)
            lse_ref[rows, lanes] = jnp.where(lane_lo, lse[:BLK], lse[BLK:])


def _dilated_group_attention(qkv, batch, seq, dil, window):
    sub = seq // dil
    a = qkv.reshape(batch, sub, dil * 3 * DIL_GW)
    nq = TQ // BLK
    cur = lambda part: pl.BlockSpec((None, TQ, DIL_GW), lambda b, r, i: (b, i, r * 3 + part))
    prev = lambda part: pl.BlockSpec(
        (None, BLK, DIL_GW), lambda b, r, i: (b, jnp.maximum(i * nq - 1, 0), r * 3 + part))
    out = pl.BlockSpec((None, TQ, DIL_GW), lambda b, r, i: (b, i, r))
    o, lse = pl.pallas_call(
        functools.partial(_dil_attn_kernel, max_rel=window // dil),
        grid=(batch, dil, sub // TQ),
        in_specs=[cur(0), cur(1), prev(1), cur(2), prev(2)],
        out_specs=[out, out],
        out_shape=[jax.ShapeDtypeStruct((batch, sub, dil * DIL_GW), jnp.bfloat16),
                   jax.ShapeDtypeStruct((batch, sub, dil * DIL_GW), jnp.float32)],
        compiler_params=_cparams(("arbitrary",) * 3),
        name=f"dilated_attn_d{dil}",
    )(a, a, a, a, a)
    t = batch * seq
    return o.reshape(t, DIL_GW), lse.reshape(t, DIL_GW)


def _swa_kernel(sink_ref, q_ref, kvc_ref, kvp_ref, o_ref):
    first_tile = pl.program_id(1) == 0
    lane_lo = lax.broadcasted_iota(jnp.int32, (BLK, LANES), 1) < HEAD_DIM
    zero = jnp.zeros((), jnp.bfloat16)
    rep = SWA_Q_HEADS // SWA_KV_HEADS
    kw = SWA_KV_HEADS * LANES
    for j in range(TQ // BLK):
        rows = slice(j * BLK, (j + 1) * BLK)
        if j == 0:
            kvprev = kvp_ref[...]
            mask = _band_mask(SWA_WINDOW - 1, first_tile)
        else:
            kvprev = kvc_ref[(j - 1) * BLK:j * BLK, :]
            mask = _band_mask(SWA_WINDOW - 1, False)
        kv = jnp.concatenate([kvprev, kvc_ref[rows, :]], axis=0)
        mask_all = jnp.concatenate([mask] * rep, axis=0)
        for g in range(SWA_KV_HEADS):
            kk = kv[:, g * LANES:(g + 1) * LANES]
            vv = kv[:, kw + g * LANES:kw + (g + 1) * LANES]
            parts, sinks = [], []
            for p in range(rep // 2):
                h0 = g * rep + 2 * p
                qp = q_ref[rows, h0 * HEAD_DIM:(h0 + 2) * HEAD_DIM]
                parts += [jnp.where(lane_lo, qp, zero), jnp.where(lane_lo, zero, qp)]
                sinks += [jnp.full((BLK, 1), sink_ref[h0], jnp.float32),
                          jnp.full((BLK, 1), sink_ref[h0 + 1], jnp.float32)]
            qs = jnp.concatenate(parts, axis=0)
            s = jnp.where(mask_all, _dot_nt(qs, kk), NEG)
            o, _, denom = _softmax_pv(s, vv, jnp.concatenate(sinks, axis=0))
            on = o / denom
            for p in range(rep // 2):
                h0 = g * rep + 2 * p
                o_ref[rows, h0 * HEAD_DIM:(h0 + 2) * HEAD_DIM] = jnp.where(
                    lane_lo, on[2 * p * BLK:(2 * p + 1) * BLK],
                    on[(2 * p + 1) * BLK:(2 * p + 2) * BLK]).astype(o_ref.dtype)


def _swa_attention(q, kv, sinks, batch, seq):
    q3 = q.reshape(batch, seq, q.shape[1])
    kv3 = kv.reshape(batch, seq, kv.shape[1])
    nq = TQ // BLK
    o = pl.pallas_call(
        _swa_kernel,
        grid_spec=pltpu.PrefetchScalarGridSpec(
            num_scalar_prefetch=1,
            grid=(batch, seq // TQ),
            in_specs=[pl.BlockSpec((None, TQ, q.shape[1]), lambda b, i, s: (b, i, 0)),
                      pl.BlockSpec((None, TQ, kv.shape[1]), lambda b, i, s: (b, i, 0)),
                      pl.BlockSpec((None, BLK, kv.shape[1]),
                                   lambda b, i, s: (b, jnp.maximum(i * nq - 1, 0), 0))],
            out_specs=pl.BlockSpec((None, TQ, q.shape[1]), lambda b, i, s: (b, i, 0))),
        out_shape=jax.ShapeDtypeStruct(q3.shape, jnp.bfloat16),
        compiler_params=_cparams(("arbitrary",) * 2),
        name="swa_attn",
    )(sinks.astype(jnp.float32), q3, kv3, kv3)
    return o.reshape(batch * seq, q.shape[1])


def _odd_in_kernel(x_ref, g_ref, w_ref, cos_ref, sin_ref, qn_ref, kn_ref, bd_ref, q_ref, kv_ref):
    hn = _rms(x_ref[...], g_ref[...]).astype(jnp.bfloat16)
    qw = SWA_Q_HEADS * HEAD_DIM
    kw = SWA_KV_HEADS * LANES
    cos = cos_ref[...]
    sin = sin_ref[...]
    q = _dot(hn, w_ref[:, 0:qw])
    q_ref[...] = _head_norm_rope(q, bd_ref[...], qn_ref[...], cos, sin,
                                 HEAD_DIM ** -0.5).astype(q_ref.dtype)
    k = _dot(hn, w_ref[:, qw:qw + kw])
    kv_ref[:, 0:kw] = _head_norm_rope(k, bd_ref[...], kn_ref[...], cos, sin,
                                      1.0).astype(kv_ref.dtype)
    kv_ref[:, kw:2 * kw] = _dot(hn, w_ref[:, qw + kw:qw + 2 * kw]).astype(kv_ref.dtype)


def _odd_in(h2, gain, w_in, cos, sin, qn, kn, ones_bd):
    t = h2.shape[0]
    qw = SWA_Q_HEADS * HEAD_DIM
    kvw = SWA_KV_HEADS * HEAD_DIM
    dup = lambda w: jnp.concatenate(
        [w[:, h * HEAD_DIM:(h + 1) * HEAD_DIM] for h in range(SWA_KV_HEADS) for _ in range(2)],
        axis=1)
    w_all = jnp.concatenate([w_in[:, :qw], dup(w_in[:, qw:qw + kvw]),
                             dup(w_in[:, qw + kvw:qw + 2 * kvw])], axis=1).astype(jnp.bfloat16)
    kw = SWA_KV_HEADS * LANES
    qn_row = jnp.tile(qn, qw // HEAD_DIM).reshape(1, qw)
    kn_row = jnp.tile(kn, kw // HEAD_DIM).reshape(1, kw)
    row = lambda n: pl.BlockSpec((TM, n), lambda i: (i, 0))
    return pl.pallas_call(
        _odd_in_kernel,
        grid=(t // TM,),
        in_specs=[row(D_MODEL), _const_spec((1, D_MODEL)), _const_spec(w_all.shape),
                  row(LANES), row(LANES), _const_spec((1, qw)), _const_spec((1, kw)),
                  _const_spec(ones_bd.shape)],
        out_specs=[row(qw), row(2 * kw)],
        out_shape=[jax.ShapeDtypeStruct((t, qw), jnp.bfloat16),
                   jax.ShapeDtypeStruct((t, 2 * kw), jnp.bfloat16)],
        compiler_params=_cparams(("arbitrary",)),
        name="odd_in_proj",
    )(h2, gain.reshape(1, D_MODEL), w_all, cos, sin, qn_row, kn_row, ones_bd)


def _route(hn, wr_hi_ref, wr_lo_ref, br_ref, carry_ref, idx_ref, gate_ref, cnt_ref):
    tm = hn.shape[0]
    h_hi = hn.astype(jnp.bfloat16)
    h_lo = (hn - h_hi.astype(jnp.float32)).astype(jnp.bfloat16)
    logits = (_dot(h_hi, wr_hi_ref[...]) + _dot(h_lo, wr_hi_ref[...])
              + _dot(h_hi, wr_lo_ref[...]) + br_ref[...])
    lt = logits.T
    rowf = lax.broadcasted_iota(jnp.int32, (SUBLANES, tm), 0).astype(jnp.float32)

    def first_argmax(x):
        m = jnp.max(x, axis=0, keepdims=True)
        idx = jnp.min(jnp.where(x == m, rowf, float(SUBLANES)), axis=0, keepdims=True)
        return m, idx

    lg = jnp.where(rowf < N_GROUPS, lt[0:SUBLANES], NEG)
    mg, gi = first_argmax(lg)
    p_top = 1.0 / jnp.sum(jnp.exp(lg - mg), axis=0, keepdims=True)
    le = lt[ROUTER_E0:ROUTER_E0 + EPG]
    for g in range(1, N_GROUPS):
        le = jnp.where(gi == float(g), lt[ROUTER_E0 + g * EPG:ROUTER_E0 + (g + 1) * EPG], le)
    m1, i1 = first_argmax(le)
    m2, i2 = first_argmax(jnp.where(rowf == i1, NEG, le))
    e2 = jnp.exp(m2 - m1)
    gate1 = p_top / (1.0 + e2)
    gate2 = p_top * e2 / (1.0 + e2)
    eid1 = gi * EPG + i1
    eid2 = gi * EPG + i2

    erow = lax.broadcasted_iota(jnp.int32, (N_EXPERTS, tm), 0).astype(jnp.float32)
    oh1 = erow == eid1
    oh2 = erow == eid2
    member = jnp.where(oh1 | oh2, 1.0, 0.0)
    s_idx = lax.broadcasted_iota(jnp.int32, (tm, tm), 0)
    t_idx = lax.broadcasted_iota(jnp.int32, (tm, tm), 1)
    upper = jnp.where(s_idx < t_idx, 1.0, 0.0).astype(jnp.bfloat16)
    before = carry_ref[:, 0:1] + _dot(member.astype(jnp.bfloat16), upper)
    rank1 = jnp.sum(jnp.where(oh1, before, 0.0), axis=0, keepdims=True)
    rank2 = jnp.sum(jnp.where(oh2, before, 0.0), axis=0, keepdims=True)
    carry_ref[...] = carry_ref[...] + jnp.sum(member, axis=1, keepdims=True)
    cnt_ref[...] = carry_ref[...].astype(jnp.int32)

    idx = jnp.where(rowf == 0.0, eid1, jnp.where(rowf == 1.0, eid2,
          jnp.where(rowf == 2.0, rank1, jnp.where(rowf == 3.0, rank2, 0.0))))
    idx_ref[...] = idx.astype(jnp.int32)
    gate_ref[...] = jnp.where(rowf == 0.0, gate1, jnp.where(rowf == 1.0, gate2, 0.0))


def _store_row_tiles(ref, x):
    n = x.shape[0]
    for c in range(ROW_CHUNKS):
        ref[pl.ds(c, n, stride=ROW_CHUNKS), :] = x[:, c * LANES:(c + 1) * LANES]


def _load_row_tiles(ref, n):
    return jnp.concatenate([ref[pl.ds(c, n, stride=ROW_CHUNKS), :] for c in range(ROW_CHUNKS)],
                           axis=1)


def _post_proj(y, x_ref, gf_ref, wr_hi_ref, wr_lo_ref, br_ref, carry_ref,
               h_ref, hn_ref, idx_ref, gate_ref, cnt_ref):
    @pl.when(pl.program_id(0) == 0)
    def _():
        carry_ref[...] = jnp.zeros_like(carry_ref)

    h = x_ref[...] + y
    h_ref[...] = h
    hn = _rms(h, gf_ref[...])
    _store_row_tiles(hn_ref, hn)
    _route(hn, wr_hi_ref, wr_lo_ref, br_ref, carry_ref, idx_ref, gate_ref, cnt_ref)


def _even_out_kernel(a_ref, o0_ref, o1_ref, o2_ref, l0_ref, l1_ref, l2_ref, w_ref, x_ref, gf_ref,
                     wr_hi_ref, wr_lo_ref, br_ref, h_ref, hn_ref, idx_ref, gate_ref, cnt_ref,
                     carry_ref):
    l0, l1, l2 = l0_ref[...], l1_ref[...], l2_ref[...]
    m = jnp.maximum(jnp.maximum(l0, l1), l2)
    e0, e1, e2 = jnp.exp(l0 - m), jnp.exp(l1 - m), jnp.exp(l2 - m)
    b = (e0 * o0_ref[...].astype(jnp.float32) + e1 * o1_ref[...].astype(jnp.float32)
         + e2 * o2_ref[...].astype(jnp.float32)) / (e0 + e1 + e2)
    y = _dot(a_ref[...], w_ref[0:SGU_WIDTH, :]) + _dot(b.astype(jnp.bfloat16),
                                                       w_ref[SGU_WIDTH:, :])
    _post_proj(y, x_ref, gf_ref, wr_hi_ref, wr_lo_ref, br_ref, carry_ref,
               h_ref, hn_ref, idx_ref, gate_ref, cnt_ref)


def _odd_out_kernel(o_ref, w_ref, x_ref, gf_ref, wr_hi_ref, wr_lo_ref, br_ref,
                    h_ref, hn_ref, idx_ref, gate_ref, cnt_ref, carry_ref):
    y = _dot(o_ref[...], w_ref[...])
    _post_proj(y, x_ref, gf_ref, wr_hi_ref, wr_lo_ref, br_ref, carry_ref,
               h_ref, hn_ref, idx_ref, gate_ref, cnt_ref)


def _router_weights(w_rg, b_rg, w_re, b_re):
    d = w_rg.shape[0]
    w = jnp.zeros((d, ROUTER_COLS), jnp.float32)
    w = w.at[:, 0:N_GROUPS].set(w_rg)
    w = w.at[:, ROUTER_E0:ROUTER_E0 + N_EXPERTS].set(
        jnp.transpose(w_re, (1, 0, 2)).reshape(d, N_EXPERTS))
    b = jnp.zeros((1, ROUTER_COLS), jnp.float32)
    b = b.at[0, 0:N_GROUPS].set(b_rg)
    b = b.at[0, ROUTER_E0:ROUTER_E0 + N_EXPERTS].set(b_re.reshape(N_EXPERTS))
    w_hi = w.astype(jnp.bfloat16)
    w_lo = (w - w_hi.astype(jnp.float32)).astype(jnp.bfloat16)
    return w_hi, w_lo, b


def _out_proj(kernel, acts, w_out, x2, gain_ffn, router):
    t = x2.shape[0]
    w_hi, w_lo, b = router
    row = lambda n: pl.BlockSpec((TM, n), lambda i: (i, 0))
    colblk = pl.BlockSpec((SUBLANES, TM), lambda i: (0, i))
    return pl.pallas_call(
        kernel,
        grid=(t // TM,),
        in_specs=[row(a.shape[1]) for a in acts]
        + [_const_spec(w_out.shape), row(D_MODEL), _const_spec((1, D_MODEL)),
           _const_spec(w_hi.shape), _const_spec(w_lo.shape), _const_spec(b.shape)],
        out_specs=[row(D_MODEL), pl.BlockSpec((TM * ROW_CHUNKS, LANES), lambda i: (i, 0)),
                   colblk, colblk,
                   _const_spec((N_EXPERTS, LANES))],
        out_shape=[jax.ShapeDtypeStruct((t, D_MODEL), jnp.float32),
                   jax.ShapeDtypeStruct((t * ROW_CHUNKS, LANES), jnp.float32),
                   jax.ShapeDtypeStruct((SUBLANES, t), jnp.int32),
                   jax.ShapeDtypeStruct((SUBLANES, t), jnp.float32),
                   jax.ShapeDtypeStruct((N_EXPERTS, LANES), jnp.int32)],
        scratch_shapes=[pltpu.VMEM((N_EXPERTS, LANES), jnp.float32)],
        compiler_params=_cparams(("arbitrary",)),
        name=kernel.__name__.strip("_"),
    )(*acts, w_out.astype(jnp.bfloat16), x2, gain_ffn.reshape(1, D_MODEL), w_hi, w_lo, b)


ROWS_PER_TILE = TOP_K * TM
DISPATCH_TM = 1024
ISSUE_UNROLL = 8


def _row_tile(ref, row):
    return ref.at[pl.ds(pl.multiple_of(row * ROW_CHUNKS, ROW_CHUNKS), ROW_CHUNKS), :]


def _wait_rows(copy, n):
    def body(_, c):
        copy.wait()
        return c

    lax.fori_loop(0, n, body, 0, unroll=16)


def _dispatch_kernel(pstart_ref, pend_ref, dest_ref, hn_ref, xs_ref, zero_ref, sem, zsem):
    blk = MOE_BM * ROW_CHUNKS

    def zero_block(row0):
        return pltpu.make_async_copy(
            zero_ref, xs_ref.at[pl.ds(pl.multiple_of(row0 * ROW_CHUNKS, blk), blk), :], zsem)

    @pl.when(pl.program_id(0) == 0)
    def _():
        zero_ref[...] = jnp.zeros_like(zero_ref)
        for e in range(N_EXPERTS):
            @pl.when(pend_ref[e] > pstart_ref[e])
            def _():
                zero_block(pend_ref[e] - MOE_BM).start()
        for e in range(N_EXPERTS):
            @pl.when(pend_ref[e] > pstart_ref[e])
            def _():
                zero_block(pend_ref[e] - MOE_BM).wait()
        first_unused = lax.div(pend_ref[N_EXPERTS - 1], MOE_BM)
        n_blocks = xs_ref.shape[0] // blk

        def start_tail(b, _):
            zero_block(b * MOE_BM).start()
            return 0

        def wait_tail(b, _):
            zero_block(b * MOE_BM).wait()
            return 0

        lax.fori_loop(first_unused, n_blocks, start_tail, 0)
        lax.fori_loop(first_unused, n_blocks, wait_tail, 0)

    def row_copy(src, dst):
        return pltpu.make_async_copy(_row_tile(hn_ref, src), _row_tile(xs_ref, dst), sem)

    def issue(b, c):
        for j in range(ISSUE_UNROLL):
            for k in range(TOP_K):
                dst = dest_ref[0, b * (ISSUE_UNROLL * TOP_K) + j * TOP_K + k]
                row_copy(b * ISSUE_UNROLL + j, dst).start(priority=k)
        return c

    lax.fori_loop(0, DISPATCH_TM // ISSUE_UNROLL, issue, 0)
    _wait_rows(row_copy(0, 0), TOP_K * DISPATCH_TM)


def _dispatch(hn, dest, pstart, pend, rows):
    t = hn.shape[0] // ROW_CHUNKS
    n = t // DISPATCH_TM
    return pl.pallas_call(
        _dispatch_kernel,
        grid_spec=pltpu.PrefetchScalarGridSpec(
            num_scalar_prefetch=2,
            grid=(n,),
            in_specs=[pl.BlockSpec((None, 1, TOP_K * DISPATCH_TM), lambda i, ps, pe: (i, 0, 0),
                                   memory_space=pltpu.SMEM),
                      pl.BlockSpec((DISPATCH_TM * ROW_CHUNKS, LANES), lambda i, ps, pe: (i, 0))],
            out_specs=pl.BlockSpec(memory_space=pl.ANY),
            scratch_shapes=[pltpu.VMEM((MOE_BM * ROW_CHUNKS, LANES), jnp.float32),
                            pltpu.SemaphoreType.DMA(()), pltpu.SemaphoreType.DMA(())]),
        out_shape=jax.ShapeDtypeStruct((rows * ROW_CHUNKS, LANES), jnp.float32),
        compiler_params=_cparams(("arbitrary",)),
        name="moe_dispatch",
    )(pstart, pend, dest.reshape(n, 1, TOP_K * DISPATCH_TM), hn)


def _expert_kernel(blk_e_ref, nused_ref, xs_ref, wg_ref, wu_ref, wd_ref, out_ref,
                   wg_s, wu_s, wd_s):
    i = pl.program_id(0)

    @pl.when(i < nused_ref[0])
    def _():
        @pl.when((i == 0) | (blk_e_ref[i] != blk_e_ref[jnp.maximum(i - 1, 0)]))
        def _():
            wg_s[...] = wg_ref[...].astype(jnp.bfloat16)
            wu_s[...] = wu_ref[...].astype(jnp.bfloat16)
            wd_s[...] = wd_ref[...].astype(jnp.bfloat16)

        x = _load_row_tiles(xs_ref, MOE_BM).astype(jnp.bfloat16)
        hid = jax.nn.silu(_dot(x, wg_s[...])) * _dot(x, wu_s[...])
        _store_row_tiles(out_ref, _dot(hid.astype(jnp.bfloat16), wd_s[...]))

    @pl.when(i >= nused_ref[0])
    def _():
        out_ref[...] = jnp.zeros_like(out_ref)


def _expert_ffn(xs, blk_e, nused, layer, w_gate, w_up, w_down):
    rows = xs.shape[0] // ROW_CHUNKS
    blk = MOE_BM * ROW_CHUNKS
    wspec = lambda shape: pl.BlockSpec((None, None) + shape,
                                       lambda i, be, nu: (layer, be[i], 0, 0))
    return pl.pallas_call(
        _expert_kernel,
        grid_spec=pltpu.PrefetchScalarGridSpec(
            num_scalar_prefetch=2,
            grid=(rows // MOE_BM,),
            in_specs=[pl.BlockSpec((blk, LANES),
                                   lambda i, be, nu: (jnp.minimum(i, nu[0] - 1), 0)),
                      wspec((D_MODEL, EXPERT_FF)), wspec((D_MODEL, EXPERT_FF)),
                      wspec((EXPERT_FF, D_MODEL))],
            out_specs=pl.BlockSpec((blk, LANES), lambda i, be, nu: (i, 0)),
            scratch_shapes=[pltpu.VMEM((D_MODEL, EXPERT_FF), jnp.bfloat16),
                            pltpu.VMEM((D_MODEL, EXPERT_FF), jnp.bfloat16),
                            pltpu.VMEM((EXPERT_FF, D_MODEL), jnp.bfloat16)]),
        out_shape=jax.ShapeDtypeStruct((rows * ROW_CHUNKS, LANES), jnp.float32),
        compiler_params=_cparams(("arbitrary",)),
        name="moe_expert_ffn",
    )(blk_e, nused, xs, w_gate, w_up, w_down)


def _combine_kernel(dest_ref, dest_next_ref, h_ref, gate_ref, ys_ref, out_ref, buf_ref, sems):
    i = pl.program_id(0)
    n = pl.num_programs(0)

    def row_copy(src, slot, k, t):
        return pltpu.make_async_copy(_row_tile(ys_ref, src), _row_tile(buf_ref.at[slot, k], t),
                                     sems.at[slot])

    def issue_tile(d_ref, slot):
        def issue(b, c):
            for j in range(ISSUE_UNROLL):
                for k in range(TOP_K):
                    src = d_ref[0, b * (ISSUE_UNROLL * TOP_K) + j * TOP_K + k]
                    row_copy(src, slot, k, b * ISSUE_UNROLL + j).start(priority=k)
            return c

        lax.fori_loop(0, TM // ISSUE_UNROLL, issue, 0)

    slot = lax.rem(i, 2)

    @pl.when(i == 0)
    def _():
        issue_tile(dest_ref, 0)

    @pl.when(i + 1 < n)
    def _():
        issue_tile(dest_next_ref, 1 - slot)

    _wait_rows(row_copy(0, slot, 0, 0), ROWS_PER_TILE)
    g = gate_ref[...]
    r1 = _load_row_tiles(buf_ref.at[slot, 0], TM)
    r2 = _load_row_tiles(buf_ref.at[slot, 1], TM)
    out_ref[...] = h_ref[...] + (r1 * g[:, 0:1] + r2 * g[:, 1:2])


def _combine(h, gate_cols, ys, dest):
    t = h.shape[0]
    n = t // TM
    dest_tiles = dest.reshape(n, 1, ROWS_PER_TILE)
    dspec = lambda f: pl.BlockSpec((None, 1, ROWS_PER_TILE), f, memory_space=pltpu.SMEM)
    return pl.pallas_call(
        _combine_kernel,
        grid=(n,),
        in_specs=[dspec(lambda i: (i, 0, 0)), dspec(lambda i: (jnp.minimum(i + 1, n - 1), 0, 0)),
                  pl.BlockSpec((TM, D_MODEL), lambda i: (i, 0)),
                  pl.BlockSpec((TM, SUBLANES), lambda i: (i, 0)),
                  pl.BlockSpec(memory_space=pl.ANY)],
        out_specs=pl.BlockSpec((TM, D_MODEL), lambda i: (i, 0)),
        scratch_shapes=[pltpu.VMEM((2, TOP_K, TM * ROW_CHUNKS, LANES), jnp.float32),
                        pltpu.SemaphoreType.DMA((2,))],
        out_shape=jax.ShapeDtypeStruct((t, D_MODEL), jnp.float32),
        compiler_params=_cparams(("arbitrary",)),
        name="moe_combine",
    )(dest_tiles, dest_tiles, h, gate_cols, ys)


def _moe(h, hn, idx, gates, counts, layer, w_gate, w_up, w_down):
    t = h.shape[0]
    rows = t * TOP_K + N_EXPERTS * MOE_BM
    nblk = rows // MOE_BM
    cnt = counts[:, 0]
    padded = (cnt + MOE_BM - 1) // MOE_BM * MOE_BM
    pend = jnp.cumsum(padded).astype(jnp.int32)
    pstart = pend - padded
    blk_row0 = jnp.arange(nblk, dtype=jnp.int32) * MOE_BM
    blk_e = jnp.minimum(jnp.sum(pend[None, :] <= blk_row0[:, None], axis=1),
                        N_EXPERTS - 1).astype(jnp.int32)
    nused = (pend[-1:] // MOE_BM).astype(jnp.int32)
    eid, rank = idx[0:TOP_K], idx[TOP_K:2 * TOP_K]
    seg_start = jnp.sum(jnp.where(eid[..., None] == jnp.arange(N_EXPERTS), pstart, 0), axis=-1)
    dest = (seg_start + rank).T.reshape(t * TOP_K)
    xs = _dispatch(hn, dest, pstart, pend, rows)
    ys = _expert_ffn(xs, blk_e, nused, layer, w_gate, w_up, w_down)
    return _combine(h, gates.T, ys, dest)


def kernel(x, positions, norm_mix, norm_ffn, w_in_even, w_out_even, sgu_norm, sgu_w, sgu_b,
           qn_dil, kn_dil, w_in_odd, w_out_odd, qn_swa, kn_swa, sinks,
           w_router_g, b_router_g, w_router_e, b_router_e, w_gate, w_up, w_down):
    batch, seq, d = x.shape
    t = batch * seq
    depth = norm_mix.shape[0]
    cos, sin = _rope_tables(positions)
    bd = 2 * LANES
    ones_bd = (jnp.arange(bd)[:, None] // HEAD_DIM == jnp.arange(bd)[None, :] // HEAD_DIM
               ).astype(jnp.bfloat16)
    h = x.reshape(t, d)
    for layer in range(depth):
        i = layer // 2
        router = _router_weights(w_router_g[layer], b_router_g[layer], w_router_e[layer],
                                 b_router_e[layer])
        if layer % 2 == 0:
            a, *qkv = _even_in(h, norm_mix[layer], w_in_even[i], cos, sin, sgu_norm[i],
                               sgu_w[i], sgu_b[i], qn_dil[i], kn_dil[i], ones_bd)
            outs, lses = [], []
            for g, (window, dil) in enumerate(DIL_GROUPS):
                o, lse = _dilated_group_attention(qkv[g], batch, seq, dil, window)
                outs.append(o)
                lses.append(lse)
            res = _out_proj(_even_out_kernel, [a] + outs + lses, w_out_even[i], h,
                            norm_ffn[layer], router)
        else:
            q, kv = _odd_in(h, norm_mix[layer], w_in_odd[i], cos, sin, qn_swa[i], kn_swa[i],
                            ones_bd)
            o = _swa_attention(q, kv, sinks[i], batch, seq)
            res = _out_proj(_odd_out_kernel, [o], w_out_odd[i], h, norm_ffn[layer], router)
        h_mid, hn, idx, gates, counts = res
        h = _moe(h_mid, hn, idx, gates, counts, layer, w_gate, w_up, w_down)
    return h.reshape(batch, seq, d)
```

```python
import functools

import jax
import jax.numpy as jnp
from jax import lax
from jax.experimental import pallas as pl
from jax.experimental.pallas import tpu as pltpu

D_MODEL = 1024
HEAD_DIM = 64
BLK = 128
ROPE_THETA = 500000.0
ROT_DIM = HEAD_DIM // 4
HALF_ROT = ROT_DIM // 2
EPS = 1e-6
SGU_GROUPS = 8
SGU_WIDTH = SGU_GROUPS * HEAD_DIM
DIL_GROUPS = ((128, 1), (512, 4), (2048, 16))
DIL_HPG = 4
DIL_GW = DIL_HPG * HEAD_DIM
DIL_WIDTH = DIL_GW * len(DIL_GROUPS)
SWA_Q_HEADS = 16
SWA_KV_HEADS = 2
SWA_WINDOW = 128
N_GROUPS = 4
EPG = 8
N_EXPERTS = N_GROUPS * EPG
EXPERT_FF = 256
TOP_K = 2

LANES = 128
SUBLANES = 8
VMEM_LIMIT_BYTES = 48 * 1024 * 1024

TM = 512
TQ = 512
MOE_BM = 256
NEG = -1e30

ROW_CHUNKS = D_MODEL // LANES
ROUTER_COLS = LANES
ROUTER_E0 = SUBLANES


def _cparams(sem):
    return pltpu.CompilerParams(dimension_semantics=sem, vmem_limit_bytes=VMEM_LIMIT_BYTES)


def _rms(x, gain_row):
    return x * lax.rsqrt(jnp.mean(x * x, axis=-1, keepdims=True) + EPS) * gain_row


def _dot(a, b):
    return jnp.dot(a, b, preferred_element_type=jnp.float32)


def _dot_nt(a, b):
    return lax.dot_general(a, b, (((1,), (1,)), ((), ())), preferred_element_type=jnp.float32)


def _rope_kernel(pos_ref, inv_ref, cos_ref, sin_ref):
    ang = inv_ref[...] * pos_ref[...].astype(jnp.float32)
    c = jnp.cos(ang)
    s = jnp.sin(ang)
    rest = HEAD_DIM // SUBLANES - 2
    head_c = [c, c] + [jnp.ones_like(c)] * rest
    head_s = [-s, s] + [jnp.zeros_like(s)] * rest
    cos_ref[...] = jnp.concatenate(head_c * 2, axis=0).T
    sin_ref[...] = jnp.concatenate(head_s * 2, axis=0).T


def _rope_tables(positions):
    t = positions.size
    inv = ROPE_THETA ** (-jnp.arange(0, ROT_DIM, 2, dtype=jnp.float32) / ROT_DIM)
    tm = 512
    return pl.pallas_call(
        _rope_kernel,
        grid=(t // tm,),
        in_specs=[pl.BlockSpec((1, tm), lambda i: (0, i)),
                  pl.BlockSpec((HALF_ROT, 1), lambda i: (0, 0))],
        out_specs=[pl.BlockSpec((tm, LANES), lambda i: (i, 0)),
                   pl.BlockSpec((tm, LANES), lambda i: (i, 0))],
        out_shape=[jax.ShapeDtypeStruct((t, LANES), jnp.float32)] * 2,
        compiler_params=_cparams(("arbitrary",)),
        name="rope_tables",
    )(positions.reshape(1, t), inv.reshape(HALF_ROT, 1))


def _head_norm_rope(x, ones_bd, gain_row, cos, sin):
    tm, w = x.shape
    sq = x * x
    sq_hi = sq.astype(jnp.bfloat16)
    sq_lo = (sq - sq_hi.astype(jnp.float32)).astype(jnp.bfloat16)
    outs = []
    bd = ones_bd.shape[0]
    for c in range(w // bd):
        sl = slice(c * bd, (c + 1) * bd)
        ss = _dot(sq_hi[:, sl], ones_bd) + _dot(sq_lo[:, sl], ones_bd)
        xn = x[:, sl] * lax.rsqrt(ss * (1.0 / HEAD_DIM) + EPS) * gain_row[:, sl]
        for j in range(bd // LANES):
            xj = xn[:, j * LANES:(j + 1) * LANES]
            lane = lax.broadcasted_iota(jnp.int32, xj.shape, 1) % HEAD_DIM
            rot = jnp.where(lane < HALF_ROT,
                            pltpu.roll(xj, LANES - HALF_ROT, axis=1),
                            pltpu.roll(xj, HALF_ROT, axis=1))
            outs.append(xj * cos + rot * sin)
    return jnp.concatenate(outs, axis=1)


def _residue_major(ref, n, w, dil):
    sub = n // dil
    return jnp.concatenate(
        [jnp.concatenate([ref[pl.ds(c * n + r, sub, stride=dil), :] for c in range(w // LANES)],
                         axis=1) for r in range(dil)], axis=0)


def _even_in_kernel(x_ref, g_ref, w_ref, cos_ref, sin_ref, sgn_ref, sgw_ref, sgb_ref,
                    qn_ref, kn_ref, bd_ref, a_ref, q0_ref, q1_ref, q2_ref, hn_scr):
    x = x_ref[...]
    tm = x.shape[0]
    hn32 = _rms(x, g_ref[...])
    for c in range(ROW_CHUNKS):
        hn_scr[c * tm:(c + 1) * tm, :] = hn32[:, c * LANES:(c + 1) * LANES]
    hn = hn32.astype(jnp.bfloat16)

    u = _dot(hn, w_ref[:, 0:SGU_WIDTH])
    v = _dot(hn, w_ref[:, SGU_WIDTH:2 * SGU_WIDTH])
    gu = jax.nn.gelu(u)
    vn = _rms(jax.nn.gelu(v), sgn_ref[...]).astype(jnp.bfloat16)

    row = lax.broadcasted_iota(jnp.int32, (BLK, BLK), 0)
    col = lax.broadcasted_iota(jnp.int32, (BLK, BLK), 1)
    tril = row >= col
    lane_lo = lax.broadcasted_iota(jnp.int32, (BLK, LANES), 1) < HEAD_DIM
    wts = [jnp.where(tril, sgw_ref[g], 0.0).astype(jnp.bfloat16) for g in range(SGU_GROUPS)]
    for c in range(tm // BLK):
        rows = slice(c * BLK, (c + 1) * BLK)
        for p in range(SGU_GROUPS // 2):
            lanes = slice(p * LANES, (p + 1) * LANES)
            vp = vn[rows, lanes]
            s = jnp.where(lane_lo, _dot(wts[2 * p], vp), _dot(wts[2 * p + 1], vp))
            a_ref[rows, lanes] = (gu[rows, lanes] * (s + sgb_ref[:, lanes])).astype(a_ref.dtype)

    o0 = 2 * SGU_WIDTH
    for g, ((_, dil), o_ref) in enumerate(zip(DIL_GROUPS, (q0_ref, q1_ref, q2_ref))):
        if dil == 1:
            hg, cos, sin = hn, cos_ref[...], sin_ref[...]
        else:
            hg = _residue_major(hn_scr, tm, D_MODEL, dil).astype(jnp.bfloat16)
            cos = _residue_major(cos_ref, tm, LANES, dil)
            sin = _residue_major(sin_ref, tm, LANES, dil)
        gl = slice(g * DIL_GW, (g + 1) * DIL_GW)
        parts = []
        for p, gain_ref in enumerate((qn_ref, kn_ref, None)):
            c0 = o0 + p * DIL_WIDTH + g * DIL_GW
            y = _dot(hg, w_ref[:, c0:c0 + DIL_GW])
            if gain_ref is not None:
                y = _head_norm_rope(y, bd_ref[...], gain_ref[:, gl], cos, sin)
            parts.append(y.astype(o_ref.dtype))
        sub = tm // dil
        for r in range(dil):
            for p, y in enumerate(parts):
                c0 = (r * 3 + p) * DIL_GW
                o_ref[:, c0:c0 + DIL_GW] = y[r * sub:(r + 1) * sub, :]


def _const_spec(shape):
    nd = len(shape)
    return pl.BlockSpec(shape, lambda i: (0,) * nd)


def _even_in(x2, gain, w_in, cos, sin, sgu_norm, sgu_w, sgu_b, qn, kn, ones_bd):
    t = x2.shape[0]
    in_w = w_in.shape[1]
    sgb = jnp.repeat(sgu_b.T, HEAD_DIM, axis=1)
    qn_row = jnp.tile(qn * HEAD_DIM ** -0.5, DIL_WIDTH // HEAD_DIM).reshape(1, DIL_WIDTH)
    kn_row = jnp.tile(kn, DIL_WIDTH // HEAD_DIM).reshape(1, DIL_WIDTH)
    row = lambda n: pl.BlockSpec((TM, n), lambda i: (i, 0))
    return pl.pallas_call(
        _even_in_kernel,
        grid=(t // TM,),
        in_specs=[row(D_MODEL), _const_spec((1, D_MODEL)), _const_spec((D_MODEL, in_w)),
                  row(LANES), row(LANES), _const_spec((1, SGU_WIDTH)),
                  _const_spec((SGU_GROUPS, BLK, BLK)), _const_spec((BLK, SGU_WIDTH)),
                  _const_spec((1, DIL_WIDTH)), _const_spec((1, DIL_WIDTH)),
                  _const_spec(ones_bd.shape)],
        out_specs=[row(SGU_WIDTH)] + [pl.BlockSpec((TM // d, d * 3 * DIL_GW), lambda i: (i, 0))
                                      for _, d in DIL_GROUPS],
        out_shape=[jax.ShapeDtypeStruct((t, SGU_WIDTH), jnp.bfloat16)]
        + [jax.ShapeDtypeStruct((t // d, d * 3 * DIL_GW), jnp.bfloat16) for _, d in DIL_GROUPS],
        scratch_shapes=[pltpu.VMEM((ROW_CHUNKS * TM, LANES), jnp.float32)],
        compiler_params=_cparams(("arbitrary",)),
        name="even_in_proj",
    )(x2, gain.reshape(1, D_MODEL), w_in.astype(jnp.bfloat16), cos, sin,
      sgu_norm.reshape(1, SGU_WIDTH), sgu_w, sgb, qn_row, kn_row, ones_bd)


def _band_mask(max_rel, first):
    i = lax.broadcasted_iota(jnp.int32, (BLK, 2 * BLK), 0)
    j = lax.broadcasted_iota(jnp.int32, (BLK, 2 * BLK), 1)
    rel = BLK + i - j
    first_key = jnp.where(first, BLK, 0)
    return (rel >= 0) & (rel <= max_rel) & (j >= first_key)


def _softmax_pv(s, vv, sink_rows):
    m = jnp.max(s, axis=-1, keepdims=True)
    if sink_rows is not None:
        m = jnp.maximum(m, sink_rows)
    p = jnp.exp(s - m)
    denom = jnp.sum(p, axis=-1, keepdims=True)
    if sink_rows is not None:
        denom = denom + jnp.exp(sink_rows - m)
    o = _dot(p.astype(jnp.bfloat16), vv)
    return o, m, denom


def _dil_attn_kernel(q_ref, kc_ref, kp_ref, vc_ref, vp_ref, o_ref, lse_ref, *, max_rel):
    first_tile = pl.program_id(2) == 0
    lane_lo = lax.broadcasted_iota(jnp.int32, (BLK, LANES), 1) < HEAD_DIM
    zero = jnp.zeros((), jnp.bfloat16)
    for j in range(TQ // BLK):
        rows = slice(j * BLK, (j + 1) * BLK)
        if j == 0:
            kprev, vprev = kp_ref[...], vp_ref[...]
            mask = _band_mask(max_rel, first_tile)
        else:
            prows = slice((j - 1) * BLK, j * BLK)
            kprev, vprev = kc_ref[prows, :], vc_ref[prows, :]
            mask = _band_mask(max_rel, False)
        kk = jnp.concatenate([kprev, kc_ref[rows, :]], axis=0)
        vv = jnp.concatenate([vprev, vc_ref[rows, :]], axis=0)
        mask2 = jnp.concatenate([mask, mask], axis=0)
        for p in range(DIL_GW // LANES):
            lanes = slice(p * LANES, (p + 1) * LANES)
            qp = q_ref[rows, lanes]
            qs = jnp.concatenate([jnp.where(lane_lo, qp, zero), jnp.where(lane_lo, zero, qp)],
                                 axis=0)
            s = jnp.where(mask2, _dot_nt(qs, kk[:, lanes]), NEG)
            o, m, denom = _softmax_pv(s, vv[:, lanes], None)
            on = o / denom
            lse = m + jnp.log(denom)
            o_ref[rows, lanes] = jnp.where(lane_lo, on[:BLK], on[BLK:]).astype(o_ref.dtype)
            lse_ref[rows, lanes] = jnp.where(lane_lo, lse[:BLK], lse[BLK:])


def _dilated_group_attention(qkv, batch, seq, dil, window):
    sub = seq // dil
    a = qkv.reshape(batch, sub, dil * 3 * DIL_GW)
    nq = TQ // BLK
    cur = lambda part: pl.BlockSpec((None, TQ, DIL_GW), lambda b, r, i: (b, i, r * 3 + part))
    prev = lambda part: pl.BlockSpec(
        (None, BLK, DIL_GW), lambda b, r, i: (b, jnp.maximum(i * nq - 1, 0), r * 3 + part))
    out = pl.BlockSpec((None, TQ, DIL_GW), lambda b, r, i: (b, i, r))
    o, lse = pl.pallas_call(
        functools.partial(_dil_attn_kernel, max_rel=window // dil),
        grid=(batch, dil, sub // TQ),
        in_specs=[cur(0), cur(1), prev(1), cur(2), prev(2)],
        out_specs=[out, out],
        out_shape=[jax.ShapeDtypeStruct((batch, sub, dil * DIL_GW), jnp.bfloat16),
                   jax.ShapeDtypeStruct((batch, sub, dil * DIL_GW), jnp.float32)],
        compiler_params=_cparams(("arbitrary",) * 3),
        name=f"dilated_attn_d{dil}",
    )(a, a, a, a, a)
    rows = batch * sub
    return o.reshape(rows, dil * DIL_GW), lse.reshape(rows, dil * DIL_GW)


def _swa_kernel(sink_ref, q_ref, kvc_ref, kvp_ref, o_ref):
    first_tile = pl.program_id(1) == 0
    lane_lo = lax.broadcasted_iota(jnp.int32, (BLK, LANES), 1) < HEAD_DIM
    zero = jnp.zeros((), jnp.bfloat16)
    rep = SWA_Q_HEADS // SWA_KV_HEADS
    kw = SWA_KV_HEADS * LANES
    for j in range(TQ // BLK):
        rows = slice(j * BLK, (j + 1) * BLK)
        if j == 0:
            kvprev = kvp_ref[...]
            mask = _band_mask(SWA_WINDOW - 1, first_tile)
        else:
            kvprev = kvc_ref[(j - 1) * BLK:j * BLK, :]
            mask = _band_mask(SWA_WINDOW - 1, False)
        kv = jnp.concatenate([kvprev, kvc_ref[rows, :]], axis=0)
        mask_all = jnp.concatenate([mask] * rep, axis=0)
        for g in range(SWA_KV_HEADS):
            kk = kv[:, g * LANES:(g + 1) * LANES]
            vv = kv[:, kw + g * LANES:kw + (g + 1) * LANES]
            parts, sinks = [], []
            for p in range(rep // 2):
                h0 = g * rep + 2 * p
                qp = q_ref[rows, h0 * HEAD_DIM:(h0 + 2) * HEAD_DIM]
                parts += [jnp.where(lane_lo, qp, zero), jnp.where(lane_lo, zero, qp)]
                sinks += [jnp.full((BLK, 1), sink_ref[h0], jnp.float32),
                          jnp.full((BLK, 1), sink_ref[h0 + 1], jnp.float32)]
            qs = jnp.concatenate(parts, axis=0)
            s = jnp.where(mask_all, _dot_nt(qs, kk), NEG)
            o, _, denom = _softmax_pv(s, vv, jnp.concatenate(sinks, axis=0))
            on = o / denom
            for p in range(rep // 2):
                h0 = g * rep + 2 * p
                o_ref[rows, h0 * HEAD_DIM:(h0 + 2) * HEAD_DIM] = jnp.where(
                    lane_lo, on[2 * p * BLK:(2 * p + 1) * BLK],
                    on[(2 * p + 1) * BLK:(2 * p + 2) * BLK]).astype(o_ref.dtype)


def _swa_attention(q, kv, sinks, batch, seq):
    q3 = q.reshape(batch, seq, q.shape[1])
    kv3 = kv.reshape(batch, seq, kv.shape[1])
    nq = TQ // BLK
    o = pl.pallas_call(
        _swa_kernel,
        grid_spec=pltpu.PrefetchScalarGridSpec(
            num_scalar_prefetch=1,
            grid=(batch, seq // TQ),
            in_specs=[pl.BlockSpec((None, TQ, q.shape[1]), lambda b, i, s: (b, i, 0)),
                      pl.BlockSpec((None, TQ, kv.shape[1]), lambda b, i, s: (b, i, 0)),
                      pl.BlockSpec((None, BLK, kv.shape[1]),
                                   lambda b, i, s: (b, jnp.maximum(i * nq - 1, 0), 0))],
            out_specs=pl.BlockSpec((None, TQ, q.shape[1]), lambda b, i, s: (b, i, 0))),
        out_shape=jax.ShapeDtypeStruct(q3.shape, jnp.bfloat16),
        compiler_params=_cparams(("arbitrary",) * 2),
        name="swa_attn",
    )(sinks.astype(jnp.float32), q3, kv3, kv3)
    return o.reshape(batch * seq, q.shape[1])


def _odd_in_kernel(x_ref, g_ref, w_ref, cos_ref, sin_ref, qn_ref, kn_ref, bd_ref, q_ref, kv_ref):
    hn = _rms(x_ref[...], g_ref[...]).astype(jnp.bfloat16)
    qw = SWA_Q_HEADS * HEAD_DIM
    kw = SWA_KV_HEADS * LANES
    cos = cos_ref[...]
    sin = sin_ref[...]
    q = _dot(hn, w_ref[:, 0:qw])
    q_ref[...] = _head_norm_rope(q, bd_ref[...], qn_ref[...], cos, sin).astype(q_ref.dtype)
    k = _dot(hn, w_ref[:, qw:qw + kw])
    kv_ref[:, 0:kw] = _head_norm_rope(k, bd_ref[...], kn_ref[...], cos, sin).astype(kv_ref.dtype)
    kv_ref[:, kw:2 * kw] = _dot(hn, w_ref[:, qw + kw:qw + 2 * kw]).astype(kv_ref.dtype)


def _odd_in(h2, gain, w_in, cos, sin, qn, kn, ones_bd):
    t = h2.shape[0]
    qw = SWA_Q_HEADS * HEAD_DIM
    kvw = SWA_KV_HEADS * HEAD_DIM
    dup = lambda w: jnp.concatenate(
        [w[:, h * HEAD_DIM:(h + 1) * HEAD_DIM] for h in range(SWA_KV_HEADS) for _ in range(2)],
        axis=1)
    w_all = jnp.concatenate([w_in[:, :qw], dup(w_in[:, qw:qw + kvw]),
                             dup(w_in[:, qw + kvw:qw + 2 * kvw])], axis=1).astype(jnp.bfloat16)
    kw = SWA_KV_HEADS * LANES
    qn_row = jnp.tile(qn * HEAD_DIM ** -0.5, qw // HEAD_DIM).reshape(1, qw)
    kn_row = jnp.tile(kn, kw // HEAD_DIM).reshape(1, kw)
    row = lambda n: pl.BlockSpec((TM, n), lambda i: (i, 0))
    return pl.pallas_call(
        _odd_in_kernel,
        grid=(t // TM,),
        in_specs=[row(D_MODEL), _const_spec((1, D_MODEL)), _const_spec(w_all.shape),
                  row(LANES), row(LANES), _const_spec((1, qw)), _const_spec((1, kw)),
                  _const_spec(ones_bd.shape)],
        out_specs=[row(qw), row(2 * kw)],
        out_shape=[jax.ShapeDtypeStruct((t, qw), jnp.bfloat16),
                   jax.ShapeDtypeStruct((t, 2 * kw), jnp.bfloat16)],
        compiler_params=_cparams(("arbitrary",)),
        name="odd_in_proj",
    )(h2, gain.reshape(1, D_MODEL), w_all, cos, sin, qn_row, kn_row, ones_bd)


def _route(hn, wr_hi_ref, wr_lo_ref, br_ref, carry_ref, idx_ref, gate_ref, cnt_ref):
    tm = hn.shape[0]
    h_hi = hn.astype(jnp.bfloat16)
    h_lo = (hn - h_hi.astype(jnp.float32)).astype(jnp.bfloat16)
    logits = (_dot(h_hi, wr_hi_ref[...]) + _dot(h_lo, wr_hi_ref[...])
              + _dot(h_hi, wr_lo_ref[...]) + br_ref[...])
    lt = logits.T
    rowf = lax.broadcasted_iota(jnp.int32, (SUBLANES, tm), 0).astype(jnp.float32)

    def first_argmax(x):
        m = jnp.max(x, axis=0, keepdims=True)
        idx = jnp.min(jnp.where(x == m, rowf, float(SUBLANES)), axis=0, keepdims=True)
        return m, idx

    lg = jnp.where(rowf < N_GROUPS, lt[0:SUBLANES], NEG)
    mg, gi = first_argmax(lg)
    p_top = 1.0 / jnp.sum(jnp.exp(lg - mg), axis=0, keepdims=True)
    le = lt[ROUTER_E0:ROUTER_E0 + EPG]
    for g in range(1, N_GROUPS):
        le = jnp.where(gi == float(g), lt[ROUTER_E0 + g * EPG:ROUTER_E0 + (g + 1) * EPG], le)
    m1, i1 = first_argmax(le)
    m2, i2 = first_argmax(jnp.where(rowf == i1, NEG, le))
    e2 = jnp.exp(m2 - m1)
    gate1 = p_top / (1.0 + e2)
    gate2 = p_top * e2 / (1.0 + e2)
    eid1 = gi * EPG + i1
    eid2 = gi * EPG + i2

    erow = lax.broadcasted_iota(jnp.int32, (N_EXPERTS, tm), 0).astype(jnp.float32)
    oh1 = erow == eid1
    oh2 = erow == eid2
    member = jnp.where(oh1 | oh2, 1.0, 0.0)
    s_idx = lax.broadcasted_iota(jnp.int32, (tm, tm), 0)
    t_idx = lax.broadcasted_iota(jnp.int32, (tm, tm), 1)
    upper = jnp.where(s_idx < t_idx, 1.0, 0.0).astype(jnp.bfloat16)
    before = carry_ref[:, 0:1] + _dot(member.astype(jnp.bfloat16), upper)
    rank1 = jnp.sum(jnp.where(oh1, before, 0.0), axis=0, keepdims=True)
    rank2 = jnp.sum(jnp.where(oh2, before, 0.0), axis=0, keepdims=True)
    carry_ref[...] = carry_ref[...] + jnp.sum(member, axis=1, keepdims=True)
    cnt_ref[...] = carry_ref[...].astype(jnp.int32)

    idx = jnp.where(rowf == 0.0, eid1, jnp.where(rowf == 1.0, eid2,
          jnp.where(rowf == 2.0, rank1, jnp.where(rowf == 3.0, rank2, 0.0))))
    idx_ref[...] = idx.astype(jnp.int32)
    gate_ref[...] = jnp.where(rowf == 0.0, gate1, jnp.where(rowf == 1.0, gate2, 0.0))


def _store_row_tiles(ref, x):
    n = x.shape[0]
    for c in range(ROW_CHUNKS):
        ref[pl.ds(c, n, stride=ROW_CHUNKS), :] = x[:, c * LANES:(c + 1) * LANES]


def _load_row_tiles(ref, n):
    return jnp.concatenate([ref[pl.ds(c, n, stride=ROW_CHUNKS), :] for c in range(ROW_CHUNKS)],
                           axis=1)


def _post_proj(y, x_ref, gf_ref, wr_hi_ref, wr_lo_ref, br_ref, carry_ref,
               h_ref, hn_ref, idx_ref, gate_ref, cnt_ref):
    @pl.when(pl.program_id(0) == 0)
    def _():
        carry_ref[...] = jnp.zeros_like(carry_ref)

    h = x_ref[...] + y
    h_ref[...] = h
    hn = _rms(h, gf_ref[...])
    _store_row_tiles(hn_ref, hn)
    _route(hn, wr_hi_ref, wr_lo_ref, br_ref, carry_ref, idx_ref, gate_ref, cnt_ref)


def _token_major(ref, dil, scr):
    if dil == 1:
        return ref[...].astype(jnp.float32)
    sub = ref.shape[0]
    n = sub * dil
    chunks = DIL_GW // LANES
    for r in range(dil):
        for c in range(chunks):
            l0 = r * DIL_GW + c * LANES
            scr[pl.ds(c * n + r, sub, stride=dil), :] = ref[:, l0:l0 + LANES].astype(jnp.float32)
    return jnp.concatenate([scr[c * n:(c + 1) * n, :] for c in range(chunks)], axis=1)


def _even_out_kernel(a_ref, o0_ref, o1_ref, o2_ref, l0_ref, l1_ref, l2_ref, w_ref, x_ref, gf_ref,
                     wr_hi_ref, wr_lo_ref, br_ref, h_ref, hn_ref, idx_ref, gate_ref, cnt_ref,
                     carry_ref, so1, so2, sl1, sl2):
    dils = [d for _, d in DIL_GROUPS]
    o0, o1, o2 = (_token_major(r, d, s) for r, d, s in zip((o0_ref, o1_ref, o2_ref), dils,
                                                            (None, so1, so2)))
    l0, l1, l2 = (_token_major(r, d, s) for r, d, s in zip((l0_ref, l1_ref, l2_ref), dils,
                                                            (None, sl1, sl2)))
    m = jnp.maximum(jnp.maximum(l0, l1), l2)
    e0, e1, e2 = jnp.exp(l0 - m), jnp.exp(l1 - m), jnp.exp(l2 - m)
    b = (e0 * o0 + e1 * o1 + e2 * o2) / (e0 + e1 + e2)
    y = _dot(a_ref[...], w_ref[0:SGU_WIDTH, :]) + _dot(b.astype(jnp.bfloat16),
                                                       w_ref[SGU_WIDTH:, :])
    _post_proj(y, x_ref, gf_ref, wr_hi_ref, wr_lo_ref, br_ref, carry_ref,
               h_ref, hn_ref, idx_ref, gate_ref, cnt_ref)


def _odd_out_kernel(o_ref, w_ref, x_ref, gf_ref, wr_hi_ref, wr_lo_ref, br_ref,
                    h_ref, hn_ref, idx_ref, gate_ref, cnt_ref, carry_ref):
    y = _dot(o_ref[...], w_ref[...])
    _post_proj(y, x_ref, gf_ref, wr_hi_ref, wr_lo_ref, br_ref, carry_ref,
               h_ref, hn_ref, idx_ref, gate_ref, cnt_ref)


def _router_weights(w_rg, b_rg, w_re, b_re):
    d = w_rg.shape[0]
    w = jnp.zeros((d, ROUTER_COLS), jnp.float32)
    w = w.at[:, 0:N_GROUPS].set(w_rg)
    w = w.at[:, ROUTER_E0:ROUTER_E0 + N_EXPERTS].set(
        jnp.transpose(w_re, (1, 0, 2)).reshape(d, N_EXPERTS))
    b = jnp.zeros((1, ROUTER_COLS), jnp.float32)
    b = b.at[0, 0:N_GROUPS].set(b_rg)
    b = b.at[0, ROUTER_E0:ROUTER_E0 + N_EXPERTS].set(b_re.reshape(N_EXPERTS))
    w_hi = w.astype(jnp.bfloat16)
    w_lo = (w - w_hi.astype(jnp.float32)).astype(jnp.bfloat16)
    return w_hi, w_lo, b


def _out_proj(kernel, acts, w_out, x2, gain_ffn, router, extra_scratch=()):
    t = x2.shape[0]
    w_hi, w_lo, b = router
    row = lambda n: pl.BlockSpec((TM, n), lambda i: (i, 0))
    colblk = pl.BlockSpec((SUBLANES, TM), lambda i: (0, i))
    return pl.pallas_call(
        kernel,
        grid=(t // TM,),
        in_specs=[pl.BlockSpec((TM * a.shape[0] // t, a.shape[1]), lambda i: (i, 0)) for a in acts]
        + [_const_spec(w_out.shape), row(D_MODEL), _const_spec((1, D_MODEL)),
           _const_spec(w_hi.shape), _const_spec(w_lo.shape), _const_spec(b.shape)],
        out_specs=[row(D_MODEL), pl.BlockSpec((TM * ROW_CHUNKS, LANES), lambda i: (i, 0)),
                   colblk, colblk,
                   _const_spec((N_EXPERTS, LANES))],
        out_shape=[jax.ShapeDtypeStruct((t, D_MODEL), jnp.float32),
                   jax.ShapeDtypeStruct((t * ROW_CHUNKS, LANES), jnp.float32),
                   jax.ShapeDtypeStruct((SUBLANES, t), jnp.int32),
                   jax.ShapeDtypeStruct((SUBLANES, t), jnp.float32),
                   jax.ShapeDtypeStruct((N_EXPERTS, LANES), jnp.int32)],
        scratch_shapes=[pltpu.VMEM((N_EXPERTS, LANES), jnp.float32)] + list(extra_scratch),
        compiler_params=_cparams(("arbitrary",)),
        name=kernel.__name__.strip("_"),
    )(*acts, w_out.astype(jnp.bfloat16), x2, gain_ffn.reshape(1, D_MODEL), w_hi, w_lo, b)


ROWS_PER_TILE = TOP_K * TM
DISPATCH_TM = 1024
ISSUE_UNROLL = 8


def _row_tile(ref, row):
    return ref.at[pl.ds(pl.multiple_of(row * ROW_CHUNKS, ROW_CHUNKS), ROW_CHUNKS), :]


def _wait_rows(copy, n):
    def body(_, c):
        copy.wait()
        return c

    lax.fori_loop(0, n, body, 0, unroll=16)


def _dispatch_kernel(pstart_ref, pend_ref, dest_ref, hn_ref, xs_ref, zero_ref, sem, zsem):
    blk = MOE_BM * ROW_CHUNKS

    def zero_block(row0):
        return pltpu.make_async_copy(
            zero_ref, xs_ref.at[pl.ds(pl.multiple_of(row0 * ROW_CHUNKS, blk), blk), :], zsem)

    @pl.when(pl.program_id(0) == 0)
    def _():
        zero_ref[...] = jnp.zeros_like(zero_ref)
        for e in range(N_EXPERTS):
            @pl.when(pend_ref[e] > pstart_ref[e])
            def _():
                zero_block(pend_ref[e] - MOE_BM).start()
        for e in range(N_EXPERTS):
            @pl.when(pend_ref[e] > pstart_ref[e])
            def _():
                zero_block(pend_ref[e] - MOE_BM).wait()
        first_unused = lax.div(pend_ref[N_EXPERTS - 1], MOE_BM)
        n_blocks = xs_ref.shape[0] // blk

        def start_tail(b, _):
            zero_block(b * MOE_BM).start()
            return 0

        def wait_tail(b, _):
            zero_block(b * MOE_BM).wait()
            return 0

        lax.fori_loop(first_unused, n_blocks, start_tail, 0)
        lax.fori_loop(first_unused, n_blocks, wait_tail, 0)

    def row_copy(src, dst):
        return pltpu.make_async_copy(_row_tile(hn_ref, src), _row_tile(xs_ref, dst), sem)

    def issue(b, c):
        for j in range(ISSUE_UNROLL):
            for k in range(TOP_K):
                dst = dest_ref[0, b * (ISSUE_UNROLL * TOP_K) + j * TOP_K + k]
                row_copy(b * ISSUE_UNROLL + j, dst).start(priority=k)
        return c

    lax.fori_loop(0, DISPATCH_TM // ISSUE_UNROLL, issue, 0)
    _wait_rows(row_copy(0, 0), TOP_K * DISPATCH_TM)


def _dispatch(hn, dest, pstart, pend, rows):
    t = hn.shape[0] // ROW_CHUNKS
    n = t // DISPATCH_TM
    return pl.pallas_call(
        _dispatch_kernel,
        grid_spec=pltpu.PrefetchScalarGridSpec(
            num_scalar_prefetch=2,
            grid=(n,),
            in_specs=[pl.BlockSpec((None, 1, TOP_K * DISPATCH_TM), lambda i, ps, pe: (i, 0, 0),
                                   memory_space=pltpu.SMEM),
                      pl.BlockSpec((DISPATCH_TM * ROW_CHUNKS, LANES), lambda i, ps, pe: (i, 0))],
            out_specs=pl.BlockSpec(memory_space=pl.ANY),
            scratch_shapes=[pltpu.VMEM((MOE_BM * ROW_CHUNKS, LANES), jnp.float32),
                            pltpu.SemaphoreType.DMA(()), pltpu.SemaphoreType.DMA(())]),
        out_shape=jax.ShapeDtypeStruct((rows * ROW_CHUNKS, LANES), jnp.float32),
        compiler_params=_cparams(("arbitrary",)),
        name="moe_dispatch",
    )(pstart, pend, dest.reshape(n, 1, TOP_K * DISPATCH_TM), hn)


def _expert_kernel(blk_e_ref, nused_ref, xs_ref, wg_ref, wu_ref, wd_ref, out_ref,
                   wg_s, wu_s, wd_s):
    i = pl.program_id(0)

    @pl.when(i < nused_ref[0])
    def _():
        @pl.when((i == 0) | (blk_e_ref[i] != blk_e_ref[jnp.maximum(i - 1, 0)]))
        def _():
            wg_s[...] = wg_ref[...].astype(jnp.bfloat16)
            wu_s[...] = wu_ref[...].astype(jnp.bfloat16)
            wd_s[...] = wd_ref[...].astype(jnp.bfloat16)

        x = _load_row_tiles(xs_ref, MOE_BM).astype(jnp.bfloat16)
        hid = jax.nn.silu(_dot(x, wg_s[...])) * _dot(x, wu_s[...])
        _store_row_tiles(out_ref, _dot(hid.astype(jnp.bfloat16), wd_s[...]))

    @pl.when(i >= nused_ref[0])
    def _():
        out_ref[...] = jnp.zeros_like(out_ref)


def _expert_ffn(xs, blk_e, nused, layer, w_gate, w_up, w_down):
    rows = xs.shape[0] // ROW_CHUNKS
    blk = MOE_BM * ROW_CHUNKS
    wspec = lambda shape: pl.BlockSpec((None, None) + shape,
                                       lambda i, be, nu: (layer, be[i], 0, 0))
    return pl.pallas_call(
        _expert_kernel,
        grid_spec=pltpu.PrefetchScalarGridSpec(
            num_scalar_prefetch=2,
            grid=(rows // MOE_BM,),
            in_specs=[pl.BlockSpec((blk, LANES),
                                   lambda i, be, nu: (jnp.minimum(i, nu[0] - 1), 0)),
                      wspec((D_MODEL, EXPERT_FF)), wspec((D_MODEL, EXPERT_FF)),
                      wspec((EXPERT_FF, D_MODEL))],
            out_specs=pl.BlockSpec((blk, LANES), lambda i, be, nu: (i, 0)),
            scratch_shapes=[pltpu.VMEM((D_MODEL, EXPERT_FF), jnp.bfloat16),
                            pltpu.VMEM((D_MODEL, EXPERT_FF), jnp.bfloat16),
                            pltpu.VMEM((EXPERT_FF, D_MODEL), jnp.bfloat16)]),
        out_shape=jax.ShapeDtypeStruct((rows * ROW_CHUNKS, LANES), jnp.float32),
        compiler_params=_cparams(("arbitrary",)),
        name="moe_expert_ffn",
    )(blk_e, nused, xs, w_gate, w_up, w_down)


def _combine_kernel(dest_ref, dest_next_ref, h_ref, gate_ref, ys_ref, out_ref, buf_ref, sems):
    i = pl.program_id(0)
    n = pl.num_programs(0)

    def row_copy(src, slot, k, t):
        return pltpu.make_async_copy(_row_tile(ys_ref, src), _row_tile(buf_ref.at[slot, k], t),
                                     sems.at[slot])

    def issue_tile(d_ref, slot):
        def issue(b, c):
            for j in range(ISSUE_UNROLL):
                for k in range(TOP_K):
                    src = d_ref[0, b * (ISSUE_UNROLL * TOP_K) + j * TOP_K + k]
                    row_copy(src, slot, k, b * ISSUE_UNROLL + j).start(priority=k)
            return c

        lax.fori_loop(0, TM // ISSUE_UNROLL, issue, 0)

    slot = lax.rem(i, 2)

    @pl.when(i == 0)
    def _():
        issue_tile(dest_ref, 0)

    @pl.when(i + 1 < n)
    def _():
        issue_tile(dest_next_ref, 1 - slot)

    _wait_rows(row_copy(0, slot, 0, 0), ROWS_PER_TILE)
    g = gate_ref[...]
    r1 = _load_row_tiles(buf_ref.at[slot, 0], TM)
    r2 = _load_row_tiles(buf_ref.at[slot, 1], TM)
    out_ref[...] = h_ref[...] + (r1 * g[:, 0:1] + r2 * g[:, 1:2])


def _combine(h, gate_cols, ys, dest):
    t = h.shape[0]
    n = t // TM
    dest_tiles = dest.reshape(n, 1, ROWS_PER_TILE)
    dspec = lambda f: pl.BlockSpec((None, 1, ROWS_PER_TILE), f, memory_space=pltpu.SMEM)
    return pl.pallas_call(
        _combine_kernel,
        grid=(n,),
        in_specs=[dspec(lambda i: (i, 0, 0)), dspec(lambda i: (jnp.minimum(i + 1, n - 1), 0, 0)),
                  pl.BlockSpec((TM, D_MODEL), lambda i: (i, 0)),
                  pl.BlockSpec((TM, SUBLANES), lambda i: (i, 0)),
                  pl.BlockSpec(memory_space=pl.ANY)],
        out_specs=pl.BlockSpec((TM, D_MODEL), lambda i: (i, 0)),
        scratch_shapes=[pltpu.VMEM((2, TOP_K, TM * ROW_CHUNKS, LANES), jnp.float32),
                        pltpu.SemaphoreType.DMA((2,))],
        out_shape=jax.ShapeDtypeStruct((t, D_MODEL), jnp.float32),
        compiler_params=_cparams(("arbitrary",)),
        name="moe_combine",
    )(dest_tiles, dest_tiles, h, gate_cols, ys)


def _moe(h, hn, idx, gates, counts, layer, w_gate, w_up, w_down):
    t = h.shape[0]
    rows = t * TOP_K + N_EXPERTS * MOE_BM
    nblk = rows // MOE_BM
    cnt = counts[:, 0]
    padded = (cnt + MOE_BM - 1) // MOE_BM * MOE_BM
    pend = jnp.cumsum(padded).astype(jnp.int32)
    pstart = pend - padded
    blk_row0 = jnp.arange(nblk, dtype=jnp.int32) * MOE_BM
    blk_e = jnp.minimum(jnp.sum(pend[None, :] <= blk_row0[:, None], axis=1),
                        N_EXPERTS - 1).astype(jnp.int32)
    nused = (pend[-1:] // MOE_BM).astype(jnp.int32)
    eid, rank = idx[0:TOP_K], idx[TOP_K:2 * TOP_K]
    seg_start = jnp.sum(jnp.where(eid[..., None] == jnp.arange(N_EXPERTS), pstart, 0), axis=-1)
    dest = (seg_start + rank).T.reshape(t * TOP_K)
    xs = _dispatch(hn, dest, pstart, pend, rows)
    ys = _expert_ffn(xs, blk_e, nused, layer, w_gate, w_up, w_down)
    return _combine(h, gates.T, ys, dest)


def kernel(x, positions, norm_mix, norm_ffn, w_in_even, w_out_even, sgu_norm, sgu_w, sgu_b,
           qn_dil, kn_dil, w_in_odd, w_out_odd, qn_swa, kn_swa, sinks,
           w_router_g, b_router_g, w_router_e, b_router_e, w_gate, w_up, w_down):
    batch, seq, d = x.shape
    t = batch * seq
    depth = norm_mix.shape[0]
    cos, sin = _rope_tables(positions)
    bd = 2 * LANES
    ones_bd = (jnp.arange(bd)[:, None] // HEAD_DIM == jnp.arange(bd)[None, :] // HEAD_DIM
               ).astype(jnp.bfloat16)
    h = x.reshape(t, d)
    for layer in range(depth):
        i = layer // 2
        router = _router_weights(w_router_g[layer], b_router_g[layer], w_router_e[layer],
                                 b_router_e[layer])
        if layer % 2 == 0:
            a, *qkv = _even_in(h, norm_mix[layer], w_in_even[i], cos, sin, sgu_norm[i],
                               sgu_w[i], sgu_b[i], qn_dil[i], kn_dil[i], ones_bd)
            outs, lses = [], []
            for g, (window, dil) in enumerate(DIL_GROUPS):
                o, lse = _dilated_group_attention(qkv[g], batch, seq, dil, window)
                outs.append(o)
                lses.append(lse)
            stage = [pltpu.VMEM((TM * DIL_GW // LANES, LANES), jnp.float32)] * 4
            res = _out_proj(_even_out_kernel, [a] + outs + lses, w_out_even[i], h,
                            norm_ffn[layer], router, stage)
        else:
            q, kv = _odd_in(h, norm_mix[layer], w_in_odd[i], cos, sin, qn_swa[i], kn_swa[i],
                            ones_bd)
            o = _swa_attention(q, kv, sinks[i], batch, seq)
            res = _out_proj(_odd_out_kernel, [o], w_out_odd[i], h, norm_ffn[layer], router)
        h_mid, hn, idx, gates, counts = res
        h = _moe(h_mid, hn, idx, gates, counts, layer, w_gate, w_up, w_down)
    return h.reshape(batch, seq, d)
```

```python
import functools

import jax
import jax.numpy as jnp
from jax import lax
from jax.experimental import pallas as pl
from jax.experimental.pallas import tpu as pltpu

D_MODEL = 1024
HEAD_DIM = 64
BLK = 128
ROPE_THETA = 500000.0
ROT_DIM = HEAD_DIM // 4
HALF_ROT = ROT_DIM // 2
EPS = 1e-6
SGU_GROUPS = 8
SGU_WIDTH = SGU_GROUPS * HEAD_DIM
DIL_GROUPS = ((128, 1), (512, 4), (2048, 16))
DIL_HPG = 4
DIL_GW = DIL_HPG * HEAD_DIM
DIL_WIDTH = DIL_GW * len(DIL_GROUPS)
SWA_Q_HEADS = 16
SWA_KV_HEADS = 2
SWA_WINDOW = 128
N_GROUPS = 4
EPG = 8
N_EXPERTS = N_GROUPS * EPG
EXPERT_FF = 256
TOP_K = 2

LANES = 128
SUBLANES = 8
VMEM_LIMIT_BYTES = 48 * 1024 * 1024

TM = 512
TQ = 512
MOE_BM = 256
NEG = -1e30

ROW_CHUNKS = D_MODEL // LANES
ROW_DTYPE = jnp.float32
ROUTER_COLS = LANES
ROUTER_E0 = SUBLANES


def _cparams(sem):
    return pltpu.CompilerParams(dimension_semantics=sem, vmem_limit_bytes=VMEM_LIMIT_BYTES)


def _rms(x, gain_row):
    return x * lax.rsqrt(jnp.mean(x * x, axis=-1, keepdims=True) + EPS) * gain_row


def _dot(a, b):
    return jnp.dot(a, b, preferred_element_type=jnp.float32)


def _dot_nt(a, b):
    return lax.dot_general(a, b, (((1,), (1,)), ((), ())), preferred_element_type=jnp.float32)


def _rope_kernel(pos_ref, inv_ref, cos_ref, sin_ref):
    ang = inv_ref[...] * pos_ref[...].astype(jnp.float32)
    c = jnp.cos(ang)
    s = jnp.sin(ang)
    rest = HEAD_DIM // SUBLANES - 2
    head_c = [c, c] + [jnp.ones_like(c)] * rest
    head_s = [-s, s] + [jnp.zeros_like(s)] * rest
    cos_ref[...] = jnp.concatenate(head_c * 2, axis=0).T
    sin_ref[...] = jnp.concatenate(head_s * 2, axis=0).T


def _rope_tables(positions):
    t = positions.size
    inv = ROPE_THETA ** (-jnp.arange(0, ROT_DIM, 2, dtype=jnp.float32) / ROT_DIM)
    tm = 512
    return pl.pallas_call(
        _rope_kernel,
        grid=(t // tm,),
        in_specs=[pl.BlockSpec((1, tm), lambda i: (0, i)),
                  pl.BlockSpec((HALF_ROT, 1), lambda i: (0, 0))],
        out_specs=[pl.BlockSpec((tm, LANES), lambda i: (i, 0)),
                   pl.BlockSpec((tm, LANES), lambda i: (i, 0))],
        out_shape=[jax.ShapeDtypeStruct((t, LANES), jnp.float32)] * 2,
        compiler_params=_cparams(("arbitrary",)),
        name="rope_tables",
    )(positions.reshape(1, t), inv.reshape(HALF_ROT, 1))


def _head_norm_rope(x, ones_bd, gain_row, cos, sin):
    tm, w = x.shape
    sq = x * x
    sq_hi = sq.astype(jnp.bfloat16)
    sq_lo = (sq - sq_hi.astype(jnp.float32)).astype(jnp.bfloat16)
    outs = []
    bd = ones_bd.shape[0]
    for c in range(w // bd):
        sl = slice(c * bd, (c + 1) * bd)
        ss = _dot(sq_hi[:, sl], ones_bd) + _dot(sq_lo[:, sl], ones_bd)
        xn = x[:, sl] * lax.rsqrt(ss * (1.0 / HEAD_DIM) + EPS) * gain_row[:, sl]
        for j in range(bd // LANES):
            xj = xn[:, j * LANES:(j + 1) * LANES]
            lane = lax.broadcasted_iota(jnp.int32, xj.shape, 1) % HEAD_DIM
            rot = jnp.where(lane < HALF_ROT,
                            pltpu.roll(xj, LANES - HALF_ROT, axis=1),
                            pltpu.roll(xj, HALF_ROT, axis=1))
            outs.append(xj * cos + rot * sin)
    return jnp.concatenate(outs, axis=1)


def _residue_major(ref, n, w, dil):
    sub = n // dil
    return jnp.concatenate(
        [jnp.concatenate([ref[pl.ds(c * n + r, sub, stride=dil), :] for c in range(w // LANES)],
                         axis=1) for r in range(dil)], axis=0)


def _even_in_kernel(x_ref, g_ref, w_ref, cos_ref, sin_ref, sgn_ref, sgw_ref, sgb_ref,
                    qn_ref, kn_ref, bd_ref, a_ref, q0_ref, q1_ref, q2_ref, hn_scr):
    x = x_ref[...]
    tm = x.shape[0]
    hn32 = _rms(x, g_ref[...])
    for c in range(ROW_CHUNKS):
        hn_scr[c * tm:(c + 1) * tm, :] = hn32[:, c * LANES:(c + 1) * LANES]
    hn = hn32.astype(jnp.bfloat16)

    u = _dot(hn, w_ref[:, 0:SGU_WIDTH])
    v = _dot(hn, w_ref[:, SGU_WIDTH:2 * SGU_WIDTH])
    gu = jax.nn.gelu(u)
    vn = _rms(jax.nn.gelu(v), sgn_ref[...]).astype(jnp.bfloat16)

    row = lax.broadcasted_iota(jnp.int32, (BLK, BLK), 0)
    col = lax.broadcasted_iota(jnp.int32, (BLK, BLK), 1)
    tril = row >= col
    lane_lo = lax.broadcasted_iota(jnp.int32, (BLK, LANES), 1) < HEAD_DIM
    wts = [jnp.where(tril, sgw_ref[g], 0.0).astype(jnp.bfloat16) for g in range(SGU_GROUPS)]
    for c in range(tm // BLK):
        rows = slice(c * BLK, (c + 1) * BLK)
        for p in range(SGU_GROUPS // 2):
            lanes = slice(p * LANES, (p + 1) * LANES)
            vp = vn[rows, lanes]
            s = jnp.where(lane_lo, _dot(wts[2 * p], vp), _dot(wts[2 * p + 1], vp))
            a_ref[rows, lanes] = (gu[rows, lanes] * (s + sgb_ref[:, lanes])).astype(a_ref.dtype)

    o0 = 2 * SGU_WIDTH
    for g, ((_, dil), o_ref) in enumerate(zip(DIL_GROUPS, (q0_ref, q1_ref, q2_ref))):
        if dil == 1:
            hg, cos, sin = hn, cos_ref[...], sin_ref[...]
        else:
            hg = _residue_major(hn_scr, tm, D_MODEL, dil).astype(jnp.bfloat16)
            cos = _residue_major(cos_ref, tm, LANES, dil)
            sin = _residue_major(sin_ref, tm, LANES, dil)
        gl = slice(g * DIL_GW, (g + 1) * DIL_GW)
        parts = []
        for p, gain_ref in enumerate((qn_ref, kn_ref, None)):
            c0 = o0 + p * DIL_WIDTH + g * DIL_GW
            y = _dot(hg, w_ref[:, c0:c0 + DIL_GW])
            if gain_ref is not None:
                y = _head_norm_rope(y, bd_ref[...], gain_ref[:, gl], cos, sin)
            parts.append(y.astype(o_ref.dtype))
        sub = tm // dil
        for r in range(dil):
            for p, y in enumerate(parts):
                c0 = (r * 3 + p) * DIL_GW
                o_ref[:, c0:c0 + DIL_GW] = y[r * sub:(r + 1) * sub, :]


def _const_spec(shape):
    nd = len(shape)
    return pl.BlockSpec(shape, lambda i: (0,) * nd)


def _even_in(x2, gain, w_in, cos, sin, sgu_norm, sgu_w, sgu_b, qn, kn, ones_bd):
    t = x2.shape[0]
    in_w = w_in.shape[1]
    sgb = jnp.repeat(sgu_b.T, HEAD_DIM, axis=1)
    qn_row = jnp.tile(qn * HEAD_DIM ** -0.5, DIL_WIDTH // HEAD_DIM).reshape(1, DIL_WIDTH)
    kn_row = jnp.tile(kn, DIL_WIDTH // HEAD_DIM).reshape(1, DIL_WIDTH)
    row = lambda n: pl.BlockSpec((TM, n), lambda i: (i, 0))
    return pl.pallas_call(
        _even_in_kernel,
        grid=(t // TM,),
        in_specs=[row(D_MODEL), _const_spec((1, D_MODEL)), _const_spec((D_MODEL, in_w)),
                  row(LANES), row(LANES), _const_spec((1, SGU_WIDTH)),
                  _const_spec((SGU_GROUPS, BLK, BLK)), _const_spec((BLK, SGU_WIDTH)),
                  _const_spec((1, DIL_WIDTH)), _const_spec((1, DIL_WIDTH)),
                  _const_spec(ones_bd.shape)],
        out_specs=[row(SGU_WIDTH)] + [pl.BlockSpec((TM // d, d * 3 * DIL_GW), lambda i: (i, 0))
                                      for _, d in DIL_GROUPS],
        out_shape=[jax.ShapeDtypeStruct((t, SGU_WIDTH), jnp.bfloat16)]
        + [jax.ShapeDtypeStruct((t // d, d * 3 * DIL_GW), jnp.bfloat16) for _, d in DIL_GROUPS],
        scratch_shapes=[pltpu.VMEM((ROW_CHUNKS * TM, LANES), jnp.float32)],
        compiler_params=_cparams(("arbitrary",)),
        name="even_in_proj",
    )(x2, gain.reshape(1, D_MODEL), w_in.astype(jnp.bfloat16), cos, sin,
      sgu_norm.reshape(1, SGU_WIDTH), sgu_w, sgb, qn_row, kn_row, ones_bd)


def _band_mask(max_rel, first):
    i = lax.broadcasted_iota(jnp.int32, (BLK, 2 * BLK), 0)
    j = lax.broadcasted_iota(jnp.int32, (BLK, 2 * BLK), 1)
    rel = BLK + i - j
    first_key = jnp.where(first, BLK, 0)
    return (rel >= 0) & (rel <= max_rel) & (j >= first_key)


def _head_attention(qh, kk, vv, mask, sink):
    s = jnp.where(mask, _dot_nt(qh, kk), NEG)
    m = jnp.max(s, axis=-1, keepdims=True)
    if sink is not None:
        m = jnp.maximum(m, sink)
    p = jnp.exp(s - m)
    denom = jnp.sum(p, axis=-1, keepdims=True)
    if sink is not None:
        denom = denom + jnp.exp(sink - m)
    o = _dot(p.astype(jnp.bfloat16), vv)
    return o / denom, m, denom


def _dil_attn_kernel(q_ref, kc_ref, kp_ref, vc_ref, vp_ref, o_ref, lse_ref, *, max_rel):
    first_tile = pl.program_id(2) == 0
    lane_lo = lax.broadcasted_iota(jnp.int32, (BLK, LANES), 1) < HEAD_DIM
    zero = jnp.zeros((), jnp.bfloat16)
    for j in range(TQ // BLK):
        rows = slice(j * BLK, (j + 1) * BLK)
        if j == 0:
            kprev, vprev = kp_ref[...], vp_ref[...]
            mask = _band_mask(max_rel, first_tile)
        else:
            prows = slice((j - 1) * BLK, j * BLK)
            kprev, vprev = kc_ref[prows, :], vc_ref[prows, :]
            mask = _band_mask(max_rel, False)
        kk = jnp.concatenate([kprev, kc_ref[rows, :]], axis=0)
        vv = jnp.concatenate([vprev, vc_ref[rows, :]], axis=0)
        for p in range(DIL_GW // LANES):
            lanes = slice(p * LANES, (p + 1) * LANES)
            qp = q_ref[rows, lanes]
            oa, ma, da = _head_attention(jnp.where(lane_lo, qp, zero), kk[:, lanes],
                                         vv[:, lanes], mask, None)
            ob, mb, db = _head_attention(jnp.where(lane_lo, zero, qp), kk[:, lanes],
                                         vv[:, lanes], mask, None)
            o_ref[rows, lanes] = jnp.where(lane_lo, oa, ob).astype(o_ref.dtype)
            lse_ref[rows, lanes] = jnp.where(lane_lo, ma + jnp.log(da), mb + jnp.log(db))


def _dilated_group_attention(qkv, batch, seq, dil, window):
    sub = seq // dil
    a = qkv.reshape(batch, sub, dil * 3 * DIL_GW)
    nq = TQ // BLK
    cur = lambda part: pl.BlockSpec((None, TQ, DIL_GW), lambda b, r, i: (b, i, r * 3 + part))
    prev = lambda part: pl.BlockSpec(
        (None, BLK, DIL_GW), lambda b, r, i: (b, jnp.maximum(i * nq - 1, 0), r * 3 + part))
    out = pl.BlockSpec((None, TQ, DIL_GW), lambda b, r, i: (b, i, r))
    o, lse = pl.pallas_call(
        functools.partial(_dil_attn_kernel, max_rel=window // dil),
        grid=(batch, dil, sub // TQ),
        in_specs=[cur(0), cur(1), prev(1), cur(2), prev(2)],
        out_specs=[out, out],
        out_shape=[jax.ShapeDtypeStruct((batch, sub, dil * DIL_GW), jnp.bfloat16),
                   jax.ShapeDtypeStruct((batch, sub, dil * DIL_GW), jnp.float32)],
        compiler_params=_cparams(("arbitrary",) * 3),
        name=f"dilated_attn_d{dil}",
    )(a, a, a, a, a)
    rows = batch * sub
    return o.reshape(rows, dil * DIL_GW), lse.reshape(rows, dil * DIL_GW)


def _swa_kernel(sink_ref, q_ref, kvc_ref, kvp_ref, o_ref):
    first_tile = pl.program_id(1) == 0
    lane_lo = lax.broadcasted_iota(jnp.int32, (BLK, LANES), 1) < HEAD_DIM
    zero = jnp.zeros((), jnp.bfloat16)
    rep = SWA_Q_HEADS // SWA_KV_HEADS
    kw = SWA_KV_HEADS * LANES
    for j in range(TQ // BLK):
        rows = slice(j * BLK, (j + 1) * BLK)
        if j == 0:
            kvprev = kvp_ref[...]
            mask = _band_mask(SWA_WINDOW - 1, first_tile)
        else:
            kvprev = kvc_ref[(j - 1) * BLK:j * BLK, :]
            mask = _band_mask(SWA_WINDOW - 1, False)
        kv = jnp.concatenate([kvprev, kvc_ref[rows, :]], axis=0)
        for g in range(SWA_KV_HEADS):
            kk = kv[:, g * LANES:(g + 1) * LANES]
            vv = kv[:, kw + g * LANES:kw + (g + 1) * LANES]
            for p in range(rep // 2):
                h0 = g * rep + 2 * p
                lanes = slice(h0 * HEAD_DIM, (h0 + 2) * HEAD_DIM)
                qp = q_ref[rows, lanes]
                oa, _, _ = _head_attention(jnp.where(lane_lo, qp, zero), kk, vv, mask,
                                           sink_ref[h0])
                ob, _, _ = _head_attention(jnp.where(lane_lo, zero, qp), kk, vv, mask,
                                           sink_ref[h0 + 1])
                o_ref[rows, lanes] = jnp.where(lane_lo, oa, ob).astype(o_ref.dtype)


def _swa_attention(q, kv, sinks, batch, seq):
    q3 = q.reshape(batch, seq, q.shape[1])
    kv3 = kv.reshape(batch, seq, kv.shape[1])
    nq = TQ // BLK
    o = pl.pallas_call(
        _swa_kernel,
        grid_spec=pltpu.PrefetchScalarGridSpec(
            num_scalar_prefetch=1,
            grid=(batch, seq // TQ),
            in_specs=[pl.BlockSpec((None, TQ, q.shape[1]), lambda b, i, s: (b, i, 0)),
                      pl.BlockSpec((None, TQ, kv.shape[1]), lambda b, i, s: (b, i, 0)),
                      pl.BlockSpec((None, BLK, kv.shape[1]),
                                   lambda b, i, s: (b, jnp.maximum(i * nq - 1, 0), 0))],
            out_specs=pl.BlockSpec((None, TQ, q.shape[1]), lambda b, i, s: (b, i, 0))),
        out_shape=jax.ShapeDtypeStruct(q3.shape, jnp.bfloat16),
        compiler_params=_cparams(("arbitrary",) * 2),
        name="swa_attn",
    )(sinks.astype(jnp.float32), q3, kv3, kv3)
    return o.reshape(batch * seq, q.shape[1])


def _odd_in_kernel(*refs, fused_combine):
    if fused_combine:
        (dest_ref, dest_next_ref, h_ref, gate_ref, ys_ref, g_ref, w_ref, cos_ref, sin_ref,
         qn_ref, kn_ref, bd_ref, x_out_ref, q_ref, kv_ref, buf_ref, sems) = refs
        x = _gather_combine(dest_ref, dest_next_ref, h_ref, gate_ref, ys_ref, buf_ref, sems)
        x_out_ref[...] = x
    else:
        x_ref, g_ref, w_ref, cos_ref, sin_ref, qn_ref, kn_ref, bd_ref, q_ref, kv_ref = refs
        x = x_ref[...]
    hn = _rms(x, g_ref[...]).astype(jnp.bfloat16)
    qw = SWA_Q_HEADS * HEAD_DIM
    kw = SWA_KV_HEADS * LANES
    cos = cos_ref[...]
    sin = sin_ref[...]
    q = _dot(hn, w_ref[:, 0:qw])
    q_ref[...] = _head_norm_rope(q, bd_ref[...], qn_ref[...], cos, sin).astype(q_ref.dtype)
    k = _dot(hn, w_ref[:, qw:qw + kw])
    kv_ref[:, 0:kw] = _head_norm_rope(k, bd_ref[...], kn_ref[...], cos, sin).astype(kv_ref.dtype)
    kv_ref[:, kw:2 * kw] = _dot(hn, w_ref[:, qw + kw:qw + 2 * kw]).astype(kv_ref.dtype)


def _odd_in(x, gain, w_in, cos, sin, qn, kn, ones_bd):
    fused = isinstance(x, tuple)
    t = (x[0] if fused else x).shape[0]
    qw = SWA_Q_HEADS * HEAD_DIM
    kvw = SWA_KV_HEADS * HEAD_DIM
    dup = lambda w: jnp.concatenate(
        [w[:, h * HEAD_DIM:(h + 1) * HEAD_DIM] for h in range(SWA_KV_HEADS) for _ in range(2)],
        axis=1)
    w_all = jnp.concatenate([w_in[:, :qw], dup(w_in[:, qw:qw + kvw]),
                             dup(w_in[:, qw + kvw:qw + 2 * kvw])], axis=1).astype(jnp.bfloat16)
    kw = SWA_KV_HEADS * LANES
    qn_row = jnp.tile(qn * HEAD_DIM ** -0.5, qw // HEAD_DIM).reshape(1, qw)
    kn_row = jnp.tile(kn, kw // HEAD_DIM).reshape(1, kw)
    row = lambda n: pl.BlockSpec((TM, n), lambda i: (i, 0))
    proj_specs = [_const_spec((1, D_MODEL)), _const_spec(w_all.shape), row(LANES), row(LANES),
                  _const_spec((1, qw)), _const_spec((1, kw)), _const_spec(ones_bd.shape)]
    proj_args = (gain.reshape(1, D_MODEL), w_all, cos, sin, qn_row, kn_row, ones_bd)
    out_specs = [row(qw), row(2 * kw)]
    out_shape = [jax.ShapeDtypeStruct((t, qw), jnp.bfloat16),
                 jax.ShapeDtypeStruct((t, 2 * kw), jnp.bfloat16)]
    if fused:
        in_specs, args, scratch = _combine_operands(*x)
        out_specs = [row(D_MODEL)] + out_specs
        out_shape = [jax.ShapeDtypeStruct((t, D_MODEL), jnp.float32)] + out_shape
    else:
        in_specs, args, scratch = [row(D_MODEL)], (x,), []
    res = pl.pallas_call(
        functools.partial(_odd_in_kernel, fused_combine=fused),
        grid=(t // TM,),
        in_specs=in_specs + proj_specs,
        out_specs=out_specs,
        out_shape=out_shape,
        scratch_shapes=scratch,
        compiler_params=_cparams(("arbitrary",)),
        name="odd_in_proj",
    )(*args, *proj_args)
    return tuple(res) if fused else (x, *res)


def _route(hn, wr_ref, br_ref, carry_ref, idx_ref, gate_ref, cnt_ref):
    tm = hn.shape[0]
    h_hi = hn.astype(jnp.bfloat16)
    h_lo = (hn - h_hi.astype(jnp.float32)).astype(jnp.bfloat16)
    hi_pass = _dot(h_hi, wr_ref[...])
    logits = (hi_pass[:, :ROUTER_COLS] + _dot(h_lo, wr_ref[:, :ROUTER_COLS])
              + hi_pass[:, ROUTER_COLS:] + br_ref[...])
    lt = logits.T
    rowf = lax.broadcasted_iota(jnp.int32, (SUBLANES, tm), 0).astype(jnp.float32)

    def first_argmax(x):
        m = jnp.max(x, axis=0, keepdims=True)
        idx = jnp.min(jnp.where(x == m, rowf, float(SUBLANES)), axis=0, keepdims=True)
        return m, idx

    lg = jnp.where(rowf < N_GROUPS, lt[0:SUBLANES], NEG)
    mg, gi = first_argmax(lg)
    p_top = 1.0 / jnp.sum(jnp.exp(lg - mg), axis=0, keepdims=True)
    le = lt[ROUTER_E0:ROUTER_E0 + EPG]
    for g in range(1, N_GROUPS):
        le = jnp.where(gi == float(g), lt[ROUTER_E0 + g * EPG:ROUTER_E0 + (g + 1) * EPG], le)
    m1, i1 = first_argmax(le)
    m2, i2 = first_argmax(jnp.where(rowf == i1, NEG, le))
    e2 = jnp.exp(m2 - m1)
    gate1 = p_top / (1.0 + e2)
    gate2 = p_top * e2 / (1.0 + e2)
    eid1 = gi * EPG + i1
    eid2 = gi * EPG + i2

    erow = lax.broadcasted_iota(jnp.int32, (N_EXPERTS, tm), 0).astype(jnp.float32)
    oh1 = erow == eid1
    oh2 = erow == eid2
    member = jnp.where(oh1 | oh2, 1.0, 0.0)
    s_idx = lax.broadcasted_iota(jnp.int32, (tm, tm), 0)
    t_idx = lax.broadcasted_iota(jnp.int32, (tm, tm), 1)
    upper = jnp.where(s_idx < t_idx, 1.0, 0.0).astype(jnp.bfloat16)
    before = carry_ref[:, 0:1] + _dot(member.astype(jnp.bfloat16), upper)
    rank1 = jnp.sum(jnp.where(oh1, before, 0.0), axis=0, keepdims=True)
    rank2 = jnp.sum(jnp.where(oh2, before, 0.0), axis=0, keepdims=True)
    carry_ref[...] = carry_ref[...] + jnp.sum(member, axis=1, keepdims=True)
    cnt_ref[...] = carry_ref[...].astype(jnp.int32)

    idx = jnp.where(rowf == 0.0, eid1, jnp.where(rowf == 1.0, eid2,
          jnp.where(rowf == 2.0, rank1, jnp.where(rowf == 3.0, rank2, 0.0))))
    idx_ref[...] = idx.astype(jnp.int32)
    gate_ref[...] = jnp.where(rowf == 0.0, gate1, jnp.where(rowf == 1.0, gate2, 0.0))


def _store_row_tiles(ref, x):
    n = x.shape[0]
    for c in range(ROW_CHUNKS):
        ref[pl.ds(c, n, stride=ROW_CHUNKS), :] = x[:, c * LANES:(c + 1) * LANES].astype(ref.dtype)


def _load_row_tiles(ref, n):
    return jnp.concatenate([ref[pl.ds(c, n, stride=ROW_CHUNKS), :] for c in range(ROW_CHUNKS)],
                           axis=1)


def _post_proj(y, x_ref, gf_ref, wr_ref, br_ref, carry_ref,
               h_ref, hn_ref, idx_ref, gate_ref, cnt_ref):
    @pl.when(pl.program_id(0) == 0)
    def _():
        carry_ref[...] = jnp.zeros_like(carry_ref)

    h = x_ref[...] + y
    h_ref[...] = h
    hn = _rms(h, gf_ref[...])
    _store_row_tiles(hn_ref, hn)
    _route(hn, wr_ref, br_ref, carry_ref, idx_ref, gate_ref, cnt_ref)


def _token_major(ref, dil, scr):
    if dil == 1:
        return ref[...].astype(jnp.float32)
    sub = ref.shape[0]
    n = sub * dil
    chunks = DIL_GW // LANES
    for r in range(dil):
        for c in range(chunks):
            l0 = r * DIL_GW + c * LANES
            scr[pl.ds(c * n + r, sub, stride=dil), :] = ref[:, l0:l0 + LANES].astype(jnp.float32)
    return jnp.concatenate([scr[c * n:(c + 1) * n, :] for c in range(chunks)], axis=1)


def _even_out_kernel(a_ref, o0_ref, o1_ref, o2_ref, l0_ref, l1_ref, l2_ref, w_ref, x_ref, gf_ref,
                     wr_ref, br_ref, h_ref, hn_ref, idx_ref, gate_ref, cnt_ref,
                     carry_ref, so1, so2, sl1, sl2):
    dils = [d for _, d in DIL_GROUPS]
    o0, o1, o2 = (_token_major(r, d, s) for r, d, s in zip((o0_ref, o1_ref, o2_ref), dils,
                                                            (None, so1, so2)))
    l0, l1, l2 = (_token_major(r, d, s) for r, d, s in zip((l0_ref, l1_ref, l2_ref), dils,
                                                            (None, sl1, sl2)))
    m = jnp.maximum(jnp.maximum(l0, l1), l2)
    e0, e1, e2 = jnp.exp(l0 - m), jnp.exp(l1 - m), jnp.exp(l2 - m)
    b = (e0 * o0 + e1 * o1 + e2 * o2) / (e0 + e1 + e2)
    y = _dot(a_ref[...], w_ref[0:SGU_WIDTH, :]) + _dot(b.astype(jnp.bfloat16),
                                                       w_ref[SGU_WIDTH:, :])
    _post_proj(y, x_ref, gf_ref, wr_ref, br_ref, carry_ref,
               h_ref, hn_ref, idx_ref, gate_ref, cnt_ref)


def _odd_out_kernel(o_ref, w_ref, x_ref, gf_ref, wr_ref, br_ref,
                    h_ref, hn_ref, idx_ref, gate_ref, cnt_ref, carry_ref):
    y = _dot(o_ref[...], w_ref[...])
    _post_proj(y, x_ref, gf_ref, wr_ref, br_ref, carry_ref,
               h_ref, hn_ref, idx_ref, gate_ref, cnt_ref)


def _router_weights(w_rg, b_rg, w_re, b_re):
    d = w_rg.shape[0]
    w = jnp.zeros((d, ROUTER_COLS), jnp.float32)
    w = w.at[:, 0:N_GROUPS].set(w_rg)
    w = w.at[:, ROUTER_E0:ROUTER_E0 + N_EXPERTS].set(
        jnp.transpose(w_re, (1, 0, 2)).reshape(d, N_EXPERTS))
    b = jnp.zeros((1, ROUTER_COLS), jnp.float32)
    b = b.at[0, 0:N_GROUPS].set(b_rg)
    b = b.at[0, ROUTER_E0:ROUTER_E0 + N_EXPERTS].set(b_re.reshape(N_EXPERTS))
    w_hi = w.astype(jnp.bfloat16)
    w_lo = (w - w_hi.astype(jnp.float32)).astype(jnp.bfloat16)
    return jnp.concatenate([w_hi, w_lo], axis=1), b


def _out_proj(kernel, acts, w_out, x2, gain_ffn, router, extra_scratch=()):
    t = x2.shape[0]
    w_router, b = router
    row = lambda n: pl.BlockSpec((TM, n), lambda i: (i, 0))
    colblk = pl.BlockSpec((SUBLANES, TM), lambda i: (0, i))
    return pl.pallas_call(
        kernel,
        grid=(t // TM,),
        in_specs=[pl.BlockSpec((TM * a.shape[0] // t, a.shape[1]), lambda i: (i, 0)) for a in acts]
        + [_const_spec(w_out.shape), row(D_MODEL), _const_spec((1, D_MODEL)),
           _const_spec(w_router.shape), _const_spec(b.shape)],
        out_specs=[row(D_MODEL), pl.BlockSpec((TM * ROW_CHUNKS, LANES), lambda i: (i, 0)),
                   colblk, colblk,
                   _const_spec((N_EXPERTS, LANES))],
        out_shape=[jax.ShapeDtypeStruct((t, D_MODEL), jnp.float32),
                   jax.ShapeDtypeStruct((t * ROW_CHUNKS, LANES), ROW_DTYPE),
                   jax.ShapeDtypeStruct((SUBLANES, t), jnp.int32),
                   jax.ShapeDtypeStruct((SUBLANES, t), jnp.float32),
                   jax.ShapeDtypeStruct((N_EXPERTS, LANES), jnp.int32)],
        scratch_shapes=[pltpu.VMEM((N_EXPERTS, LANES), jnp.float32)] + list(extra_scratch),
        compiler_params=_cparams(("arbitrary",)),
        name=kernel.__name__.strip("_"),
    )(*acts, w_out.astype(jnp.bfloat16), x2, gain_ffn.reshape(1, D_MODEL), w_router, b)


ROWS_PER_TILE = TOP_K * TM
DISPATCH_TM = 1024
ISSUE_UNROLL = 8


def _row_tile(ref, row):
    return ref.at[pl.ds(pl.multiple_of(row * ROW_CHUNKS, ROW_CHUNKS), ROW_CHUNKS), :]


def _wait_rows(copy, n):
    def body(_, c):
        copy.wait()
        return c

    lax.fori_loop(0, n, body, 0, unroll=16)


def _dispatch_kernel(pstart_ref, pend_ref, dest_ref, hn_ref, xs_ref, zero_ref, sem, zsem):
    blk = MOE_BM * ROW_CHUNKS

    def zero_block(row0):
        return pltpu.make_async_copy(
            zero_ref, xs_ref.at[pl.ds(pl.multiple_of(row0 * ROW_CHUNKS, blk), blk), :], zsem)

    @pl.when(pl.program_id(0) == 0)
    def _():
        zero_ref[...] = jnp.zeros_like(zero_ref)
        for e in range(N_EXPERTS):
            @pl.when(pend_ref[e] > pstart_ref[e])
            def _():
                zero_block(pend_ref[e] - MOE_BM).start()
        for e in range(N_EXPERTS):
            @pl.when(pend_ref[e] > pstart_ref[e])
            def _():
                zero_block(pend_ref[e] - MOE_BM).wait()
        first_unused = lax.div(pend_ref[N_EXPERTS - 1], MOE_BM)
        n_blocks = xs_ref.shape[0] // blk

        def start_tail(b, _):
            zero_block(b * MOE_BM).start()
            return 0

        def wait_tail(b, _):
            zero_block(b * MOE_BM).wait()
            return 0

        lax.fori_loop(first_unused, n_blocks, start_tail, 0)
        lax.fori_loop(first_unused, n_blocks, wait_tail, 0)

    def row_copy(src, dst):
        return pltpu.make_async_copy(_row_tile(hn_ref, src), _row_tile(xs_ref, dst), sem)

    def issue(b, c):
        for j in range(ISSUE_UNROLL):
            for k in range(TOP_K):
                dst = dest_ref[0, b * (ISSUE_UNROLL * TOP_K) + j * TOP_K + k]
                row_copy(b * ISSUE_UNROLL + j, dst).start(priority=k)
        return c

    lax.fori_loop(0, DISPATCH_TM // ISSUE_UNROLL, issue, 0)
    _wait_rows(row_copy(0, 0), TOP_K * DISPATCH_TM)


def _dispatch(hn, dest, pstart, pend, rows):
    t = hn.shape[0] // ROW_CHUNKS
    n = t // DISPATCH_TM
    return pl.pallas_call(
        _dispatch_kernel,
        grid_spec=pltpu.PrefetchScalarGridSpec(
            num_scalar_prefetch=2,
            grid=(n,),
            in_specs=[pl.BlockSpec((None, 1, TOP_K * DISPATCH_TM), lambda i, ps, pe: (i, 0, 0),
                                   memory_space=pltpu.SMEM),
                      pl.BlockSpec((DISPATCH_TM * ROW_CHUNKS, LANES), lambda i, ps, pe: (i, 0))],
            out_specs=pl.BlockSpec(memory_space=pl.ANY),
            scratch_shapes=[pltpu.VMEM((MOE_BM * ROW_CHUNKS, LANES), ROW_DTYPE),
                            pltpu.SemaphoreType.DMA(()), pltpu.SemaphoreType.DMA(())]),
        out_shape=jax.ShapeDtypeStruct((rows * ROW_CHUNKS, LANES), ROW_DTYPE),
        compiler_params=_cparams(("arbitrary",)),
        name="moe_dispatch",
    )(pstart, pend, dest.reshape(n, 1, TOP_K * DISPATCH_TM), hn)


def _expert_kernel(blk_e_ref, nused_ref, xs_ref, wg_ref, wu_ref, wd_ref, out_ref,
                   wg_s, wu_s, wd_s):
    i = pl.program_id(0)

    @pl.when(i < nused_ref[0])
    def _():
        @pl.when((i == 0) | (blk_e_ref[i] != blk_e_ref[jnp.maximum(i - 1, 0)]))
        def _():
            wg_s[...] = wg_ref[...].astype(jnp.bfloat16)
            wu_s[...] = wu_ref[...].astype(jnp.bfloat16)
            wd_s[...] = wd_ref[...].astype(jnp.bfloat16)

        x = _load_row_tiles(xs_ref, MOE_BM).astype(jnp.bfloat16)
        hid = jax.nn.silu(_dot(x, wg_s[...])) * _dot(x, wu_s[...])
        _store_row_tiles(out_ref, _dot(hid.astype(jnp.bfloat16), wd_s[...]))

    @pl.when(i >= nused_ref[0])
    def _():
        out_ref[...] = jnp.zeros_like(out_ref)


def _expert_ffn(xs, blk_e, nused, layer, w_gate, w_up, w_down):
    rows = xs.shape[0] // ROW_CHUNKS
    blk = MOE_BM * ROW_CHUNKS
    wspec = lambda shape: pl.BlockSpec((None, None) + shape,
                                       lambda i, be, nu: (layer, be[i], 0, 0))
    return pl.pallas_call(
        _expert_kernel,
        grid_spec=pltpu.PrefetchScalarGridSpec(
            num_scalar_prefetch=2,
            grid=(rows // MOE_BM,),
            in_specs=[pl.BlockSpec((blk, LANES),
                                   lambda i, be, nu: (jnp.minimum(i, nu[0] - 1), 0)),
                      wspec((D_MODEL, EXPERT_FF)), wspec((D_MODEL, EXPERT_FF)),
                      wspec((EXPERT_FF, D_MODEL))],
            out_specs=pl.BlockSpec((blk, LANES), lambda i, be, nu: (i, 0)),
            scratch_shapes=[pltpu.VMEM((D_MODEL, EXPERT_FF), jnp.bfloat16),
                            pltpu.VMEM((D_MODEL, EXPERT_FF), jnp.bfloat16),
                            pltpu.VMEM((EXPERT_FF, D_MODEL), jnp.bfloat16)]),
        out_shape=jax.ShapeDtypeStruct((rows * ROW_CHUNKS, LANES), ROW_DTYPE),
        compiler_params=_cparams(("arbitrary",)),
        name="moe_expert_ffn",
    )(blk_e, nused, xs, w_gate, w_up, w_down)


def _gather_combine(dest_ref, dest_next_ref, h_ref, gate_ref, ys_ref, buf_ref, sems):
    i = pl.program_id(0)
    n = pl.num_programs(0)

    def row_copy(src, slot, k, t):
        return pltpu.make_async_copy(_row_tile(ys_ref, src), _row_tile(buf_ref.at[slot, k], t),
                                     sems.at[slot])

    def issue_tile(d_ref, slot):
        def issue(b, c):
            for j in range(ISSUE_UNROLL):
                for k in range(TOP_K):
                    src = d_ref[0, b * (ISSUE_UNROLL * TOP_K) + j * TOP_K + k]
                    row_copy(src, slot, k, b * ISSUE_UNROLL + j).start(priority=k)
            return c

        lax.fori_loop(0, TM // ISSUE_UNROLL, issue, 0)

    slot = lax.rem(i, 2)

    @pl.when(i == 0)
    def _():
        issue_tile(dest_ref, 0)

    @pl.when(i + 1 < n)
    def _():
        issue_tile(dest_next_ref, 1 - slot)

    _wait_rows(row_copy(0, slot, 0, 0), ROWS_PER_TILE)
    g = gate_ref[...]
    r1 = _load_row_tiles(buf_ref.at[slot, 0], TM)
    r2 = _load_row_tiles(buf_ref.at[slot, 1], TM)
    return h_ref[...] + (r1 * g[:, 0:1] + r2 * g[:, 1:2])


def _combine_kernel(dest_ref, dest_next_ref, h_ref, gate_ref, ys_ref, out_ref, buf_ref, sems):
    out_ref[...] = _gather_combine(dest_ref, dest_next_ref, h_ref, gate_ref, ys_ref, buf_ref, sems)


def _combine_operands(h, gate_cols, ys, dest):
    t = h.shape[0]
    n = t // TM
    dest_tiles = dest.reshape(n, 1, ROWS_PER_TILE)
    dspec = lambda f: pl.BlockSpec((None, 1, ROWS_PER_TILE), f, memory_space=pltpu.SMEM)
    in_specs = [dspec(lambda i: (i, 0, 0)), dspec(lambda i: (jnp.minimum(i + 1, n - 1), 0, 0)),
                pl.BlockSpec((TM, D_MODEL), lambda i: (i, 0)),
                pl.BlockSpec((TM, SUBLANES), lambda i: (i, 0)),
                pl.BlockSpec(memory_space=pl.ANY)]
    scratch = [pltpu.VMEM((2, TOP_K, TM * ROW_CHUNKS, LANES), ROW_DTYPE),
               pltpu.SemaphoreType.DMA((2,))]
    return in_specs, (dest_tiles, dest_tiles, h, gate_cols, ys), scratch


def _combine(h, gate_cols, ys, dest):
    t = h.shape[0]
    in_specs, args, scratch = _combine_operands(h, gate_cols, ys, dest)
    return pl.pallas_call(
        _combine_kernel,
        grid=(t // TM,),
        in_specs=in_specs,
        out_specs=pl.BlockSpec((TM, D_MODEL), lambda i: (i, 0)),
        scratch_shapes=scratch,
        out_shape=jax.ShapeDtypeStruct((t, D_MODEL), jnp.float32),
        compiler_params=_cparams(("arbitrary",)),
        name="moe_combine",
    )(*args)


def _moe(h, hn, idx, gates, counts, layer, w_gate, w_up, w_down, defer_combine):
    t = h.shape[0]
    rows = t * TOP_K + N_EXPERTS * MOE_BM
    nblk = rows // MOE_BM
    cnt = counts[:, 0]
    padded = (cnt + MOE_BM - 1) // MOE_BM * MOE_BM
    pend = jnp.cumsum(padded).astype(jnp.int32)
    pstart = pend - padded
    blk_row0 = jnp.arange(nblk, dtype=jnp.int32) * MOE_BM
    blk_e = jnp.minimum(jnp.sum(pend[None, :] <= blk_row0[:, None], axis=1),
                        N_EXPERTS - 1).astype(jnp.int32)
    nused = (pend[-1:] // MOE_BM).astype(jnp.int32)
    eid, rank = idx[0:TOP_K], idx[TOP_K:2 * TOP_K]
    seg_start = jnp.sum(jnp.where(eid[..., None] == jnp.arange(N_EXPERTS), pstart, 0), axis=-1)
    dest = (seg_start + rank).T.reshape(t * TOP_K)
    xs = _dispatch(hn, dest, pstart, pend, rows)
    ys = _expert_ffn(xs, blk_e, nused, layer, w_gate, w_up, w_down)
    pending = (h, gates.T, ys, dest)
    return pending if defer_combine else _combine(*pending)


def kernel(x, positions, norm_mix, norm_ffn, w_in_even, w_out_even, sgu_norm, sgu_w, sgu_b,
           qn_dil, kn_dil, w_in_odd, w_out_odd, qn_swa, kn_swa, sinks,
           w_router_g, b_router_g, w_router_e, b_router_e, w_gate, w_up, w_down):
    batch, seq, d = x.shape
    t = batch * seq
    depth = norm_mix.shape[0]
    cos, sin = _rope_tables(positions)
    bd = 2 * LANES
    ones_bd = (jnp.arange(bd)[:, None] // HEAD_DIM == jnp.arange(bd)[None, :] // HEAD_DIM
               ).astype(jnp.bfloat16)
    h = x.reshape(t, d)
    for layer in range(depth):
        i = layer // 2
        router = _router_weights(w_router_g[layer], b_router_g[layer], w_router_e[layer],
                                 b_router_e[layer])
        if layer % 2 == 0:
            a, *qkv = _even_in(h, norm_mix[layer], w_in_even[i], cos, sin, sgu_norm[i],
                               sgu_w[i], sgu_b[i], qn_dil[i], kn_dil[i], ones_bd)
            outs, lses = [], []
            for g, (window, dil) in enumerate(DIL_GROUPS):
                o, lse = _dilated_group_attention(qkv[g], batch, seq, dil, window)
                outs.append(o)
                lses.append(lse)
            stage = [pltpu.VMEM((TM * DIL_GW // LANES, LANES), jnp.float32)] * 4
            res = _out_proj(_even_out_kernel, [a] + outs + lses, w_out_even[i], h,
                            norm_ffn[layer], router, stage)
        else:
            h, q, kv = _odd_in(h, norm_mix[layer], w_in_odd[i], cos, sin, qn_swa[i], kn_swa[i],
                               ones_bd)
            o = _swa_attention(q, kv, sinks[i], batch, seq)
            res = _out_proj(_odd_out_kernel, [o], w_out_odd[i], h, norm_ffn[layer], router)
        h_mid, hn, idx, gates, counts = res
        next_is_odd = layer + 1 < depth and (layer + 1) % 2 == 1
        h = _moe(h_mid, hn, idx, gates, counts, layer, w_gate, w_up, w_down,
                 defer_combine=next_is_odd)
    return h.reshape(batch, seq, d)
```

```python
import functools

import jax
import jax.numpy as jnp
from jax import lax
from jax.experimental import pallas as pl
from jax.experimental.pallas import tpu as pltpu

D_MODEL = 1024
HEAD_DIM = 64
BLK = 128
ROPE_THETA = 500000.0
ROT_DIM = HEAD_DIM // 4
HALF_ROT = ROT_DIM // 2
EPS = 1e-6
SGU_GROUPS = 8
SGU_WIDTH = SGU_GROUPS * HEAD_DIM
DIL_GROUPS = ((128, 1), (512, 4), (2048, 16))
DIL_HPG = 4
DIL_GW = DIL_HPG * HEAD_DIM
DIL_WIDTH = DIL_GW * len(DIL_GROUPS)
SWA_Q_HEADS = 16
SWA_KV_HEADS = 2
SWA_WINDOW = 128
N_GROUPS = 4
EPG = 8
N_EXPERTS = N_GROUPS * EPG
EXPERT_FF = 256
TOP_K = 2
PAIRS_PER_GROUP = EPG * (EPG - 1) // 2
N_CLASSES = N_GROUPS * PAIRS_PER_GROUP

LANES = 128
SUBLANES = 8
VMEM_LIMIT_BYTES = 48 * 1024 * 1024

TM = 512
TQ = 512
MOE_BM = 128
CLASS_ROWS = 128
NEG = -1e30

ROW_CHUNKS = D_MODEL // LANES
ROW_DTYPE = jnp.float32
ROUTER_COLS = LANES
ROUTER_E0 = SUBLANES


def _cparams(sem):
    return pltpu.CompilerParams(dimension_semantics=sem, vmem_limit_bytes=VMEM_LIMIT_BYTES)


def _rms(x, gain_row):
    return x * lax.rsqrt(jnp.mean(x * x, axis=-1, keepdims=True) + EPS) * gain_row


def _dot(a, b):
    return jnp.dot(a, b, preferred_element_type=jnp.float32)


def _dot_nt(a, b):
    return lax.dot_general(a, b, (((1,), (1,)), ((), ())), preferred_element_type=jnp.float32)


def _rope_kernel(pos_ref, inv_ref, cos_ref, sin_ref):
    ang = inv_ref[...] * pos_ref[...].astype(jnp.float32)
    c = jnp.cos(ang)
    s = jnp.sin(ang)
    rest = HEAD_DIM // SUBLANES - 2
    head_c = [c, c] + [jnp.ones_like(c)] * rest
    head_s = [-s, s] + [jnp.zeros_like(s)] * rest
    cos_ref[...] = jnp.concatenate(head_c * 2, axis=0).T
    sin_ref[...] = jnp.concatenate(head_s * 2, axis=0).T


def _rope_tables(positions):
    t = positions.size
    inv = ROPE_THETA ** (-jnp.arange(0, ROT_DIM, 2, dtype=jnp.float32) / ROT_DIM)
    tm = 512
    return pl.pallas_call(
        _rope_kernel,
        grid=(t // tm,),
        in_specs=[pl.BlockSpec((1, tm), lambda i: (0, i)),
                  pl.BlockSpec((HALF_ROT, 1), lambda i: (0, 0))],
        out_specs=[pl.BlockSpec((tm, LANES), lambda i: (i, 0)),
                   pl.BlockSpec((tm, LANES), lambda i: (i, 0))],
        out_shape=[jax.ShapeDtypeStruct((t, LANES), jnp.float32)] * 2,
        compiler_params=_cparams(("arbitrary",)),
        name="rope_tables",
    )(positions.reshape(1, t), inv.reshape(HALF_ROT, 1))


def _head_norm_rope(x, ones_bd, gain_row, cos, sin):
    tm, w = x.shape
    sq = x * x
    sq_hi = sq.astype(jnp.bfloat16)
    sq_lo = (sq - sq_hi.astype(jnp.float32)).astype(jnp.bfloat16)
    outs = []
    bd = ones_bd.shape[0]
    for c in range(w // bd):
        sl = slice(c * bd, (c + 1) * bd)
        ss = _dot(sq_hi[:, sl], ones_bd) + _dot(sq_lo[:, sl], ones_bd)
        xn = x[:, sl] * lax.rsqrt(ss * (1.0 / HEAD_DIM) + EPS) * gain_row[:, sl]
        for j in range(bd // LANES):
            xj = xn[:, j * LANES:(j + 1) * LANES]
            lane = lax.broadcasted_iota(jnp.int32, xj.shape, 1) % HEAD_DIM
            rot = jnp.where(lane < HALF_ROT,
                            pltpu.roll(xj, LANES - HALF_ROT, axis=1),
                            pltpu.roll(xj, HALF_ROT, axis=1))
            outs.append(xj * cos + rot * sin)
    return jnp.concatenate(outs, axis=1)


def _residue_major(ref, n, w, dil):
    sub = n // dil
    return jnp.concatenate(
        [jnp.concatenate([ref[pl.ds(c * n + r, sub, stride=dil), :] for c in range(w // LANES)],
                         axis=1) for r in range(dil)], axis=0)


def _even_in_kernel(x_ref, g_ref, w_ref, cos_ref, sin_ref, sgn_ref, sgw_ref, sgb_ref,
                    qn_ref, kn_ref, bd_ref, a_ref, q0_ref, q1_ref, q2_ref, hn_scr):
    x = x_ref[...]
    tm = x.shape[0]
    hn32 = _rms(x, g_ref[...])
    for c in range(ROW_CHUNKS):
        hn_scr[c * tm:(c + 1) * tm, :] = hn32[:, c * LANES:(c + 1) * LANES]
    hn = hn32.astype(jnp.bfloat16)

    u = _dot(hn, w_ref[:, 0:SGU_WIDTH])
    v = _dot(hn, w_ref[:, SGU_WIDTH:2 * SGU_WIDTH])
    gu = jax.nn.gelu(u)
    vn = _rms(jax.nn.gelu(v), sgn_ref[...]).astype(jnp.bfloat16)

    row = lax.broadcasted_iota(jnp.int32, (BLK, BLK), 0)
    col = lax.broadcasted_iota(jnp.int32, (BLK, BLK), 1)
    tril = row >= col
    lane_lo = lax.broadcasted_iota(jnp.int32, (BLK, LANES), 1) < HEAD_DIM
    wts = [jnp.where(tril, sgw_ref[g], 0.0).astype(jnp.bfloat16) for g in range(SGU_GROUPS)]
    for c in range(tm // BLK):
        rows = slice(c * BLK, (c + 1) * BLK)
        for p in range(SGU_GROUPS // 2):
            lanes = slice(p * LANES, (p + 1) * LANES)
            vp = vn[rows, lanes]
            s = jnp.where(lane_lo, _dot(wts[2 * p], vp), _dot(wts[2 * p + 1], vp))
            a_ref[rows, lanes] = (gu[rows, lanes] * (s + sgb_ref[:, lanes])).astype(a_ref.dtype)

    o0 = 2 * SGU_WIDTH
    for g, ((_, dil), o_ref) in enumerate(zip(DIL_GROUPS, (q0_ref, q1_ref, q2_ref))):
        if dil == 1:
            hg, cos, sin = hn, cos_ref[...], sin_ref[...]
        else:
            hg = _residue_major(hn_scr, tm, D_MODEL, dil).astype(jnp.bfloat16)
            cos = _residue_major(cos_ref, tm, LANES, dil)
            sin = _residue_major(sin_ref, tm, LANES, dil)
        gl = slice(g * DIL_GW, (g + 1) * DIL_GW)
        parts = []
        for p, gain_ref in enumerate((qn_ref, kn_ref, None)):
            c0 = o0 + p * DIL_WIDTH + g * DIL_GW
            y = _dot(hg, w_ref[:, c0:c0 + DIL_GW])
            if gain_ref is not None:
                y = _head_norm_rope(y, bd_ref[...], gain_ref[:, gl], cos, sin)
            parts.append(y.astype(o_ref.dtype))
        sub = tm // dil
        for r in range(dil):
            for p, y in enumerate(parts):
                c0 = (r * 3 + p) * DIL_GW
                o_ref[:, c0:c0 + DIL_GW] = y[r * sub:(r + 1) * sub, :]


def _const_spec(shape):
    nd = len(shape)
    return pl.BlockSpec(shape, lambda i: (0,) * nd)


def _even_in(x2, gain, w_in, cos, sin, sgu_norm, sgu_w, sgu_b, qn, kn, ones_bd):
    t = x2.shape[0]
    in_w = w_in.shape[1]
    sgb = jnp.repeat(sgu_b.T, HEAD_DIM, axis=1)
    qn_row = jnp.tile(qn * HEAD_DIM ** -0.5, DIL_WIDTH // HEAD_DIM).reshape(1, DIL_WIDTH)
    kn_row = jnp.tile(kn, DIL_WIDTH // HEAD_DIM).reshape(1, DIL_WIDTH)
    row = lambda n: pl.BlockSpec((TM, n), lambda i: (i, 0))
    return pl.pallas_call(
        _even_in_kernel,
        grid=(t // TM,),
        in_specs=[row(D_MODEL), _const_spec((1, D_MODEL)), _const_spec((D_MODEL, in_w)),
                  row(LANES), row(LANES), _const_spec((1, SGU_WIDTH)),
                  _const_spec((SGU_GROUPS, BLK, BLK)), _const_spec((BLK, SGU_WIDTH)),
                  _const_spec((1, DIL_WIDTH)), _const_spec((1, DIL_WIDTH)),
                  _const_spec(ones_bd.shape)],
        out_specs=[row(SGU_WIDTH)] + [pl.BlockSpec((TM // d, d * 3 * DIL_GW), lambda i: (i, 0))
                                      for _, d in DIL_GROUPS],
        out_shape=[jax.ShapeDtypeStruct((t, SGU_WIDTH), jnp.bfloat16)]
        + [jax.ShapeDtypeStruct((t // d, d * 3 * DIL_GW), jnp.bfloat16) for _, d in DIL_GROUPS],
        scratch_shapes=[pltpu.VMEM((ROW_CHUNKS * TM, LANES), jnp.float32)],
        compiler_params=_cparams(("arbitrary",)),
        name="even_in_proj",
    )(x2, gain.reshape(1, D_MODEL), w_in.astype(jnp.bfloat16), cos, sin,
      sgu_norm.reshape(1, SGU_WIDTH), sgu_w, sgb, qn_row, kn_row, ones_bd)


def _band_mask(max_rel, first):
    i = lax.broadcasted_iota(jnp.int32, (BLK, 2 * BLK), 0)
    j = lax.broadcasted_iota(jnp.int32, (BLK, 2 * BLK), 1)
    rel = BLK + i - j
    first_key = jnp.where(first, BLK, 0)
    return (rel >= 0) & (rel <= max_rel) & (j >= first_key)


def _head_attention(qh, kk, vv, mask, sink):
    s = jnp.where(mask, _dot_nt(qh, kk), NEG)
    m = jnp.max(s, axis=-1, keepdims=True)
    if sink is not None:
        m = jnp.maximum(m, sink)
    p = jnp.exp(s - m)
    denom = jnp.sum(p, axis=-1, keepdims=True)
    if sink is not None:
        denom = denom + jnp.exp(sink - m)
    o = _dot(p.astype(jnp.bfloat16), vv)
    return o / denom, m, denom


def _dil_attn_kernel(q_ref, kc_ref, kp_ref, vc_ref, vp_ref, o_ref, lse_ref, *, max_rel):
    first_tile = pl.program_id(2) == 0
    lane_lo = lax.broadcasted_iota(jnp.int32, (BLK, LANES), 1) < HEAD_DIM
    zero = jnp.zeros((), jnp.bfloat16)
    for j in range(TQ // BLK):
        rows = slice(j * BLK, (j + 1) * BLK)
        if j == 0:
            kprev, vprev = kp_ref[...], vp_ref[...]
            mask = _band_mask(max_rel, first_tile)
        else:
            prows = slice((j - 1) * BLK, j * BLK)
            kprev, vprev = kc_ref[prows, :], vc_ref[prows, :]
            mask = _band_mask(max_rel, False)
        kk = jnp.concatenate([kprev, kc_ref[rows, :]], axis=0)
        vv = jnp.concatenate([vprev, vc_ref[rows, :]], axis=0)
        for p in range(DIL_GW // LANES):
            lanes = slice(p * LANES, (p + 1) * LANES)
            qp = q_ref[rows, lanes]
            oa, ma, da = _head_attention(jnp.where(lane_lo, qp, zero), kk[:, lanes],
                                         vv[:, lanes], mask, None)
            ob, mb, db = _head_attention(jnp.where(lane_lo, zero, qp), kk[:, lanes],
                                         vv[:, lanes], mask, None)
            o_ref[rows, lanes] = jnp.where(lane_lo, oa, ob).astype(o_ref.dtype)
            lse_ref[rows, lanes] = jnp.where(lane_lo, ma + jnp.log(da), mb + jnp.log(db))


def _dilated_group_attention(qkv, batch, seq, dil, window):
    sub = seq // dil
    a = qkv.reshape(batch, sub, dil * 3 * DIL_GW)
    nq = TQ // BLK
    cur = lambda part: pl.BlockSpec((None, TQ, DIL_GW), lambda b, r, i: (b, i, r * 3 + part))
    prev = lambda part: pl.BlockSpec(
        (None, BLK, DIL_GW), lambda b, r, i: (b, jnp.maximum(i * nq - 1, 0), r * 3 + part))
    out = pl.BlockSpec((None, TQ, DIL_GW), lambda b, r, i: (b, i, r))
    o, lse = pl.pallas_call(
        functools.partial(_dil_attn_kernel, max_rel=window // dil),
        grid=(batch, dil, sub // TQ),
        in_specs=[cur(0), cur(1), prev(1), cur(2), prev(2)],
        out_specs=[out, out],
        out_shape=[jax.ShapeDtypeStruct((batch, sub, dil * DIL_GW), jnp.bfloat16),
                   jax.ShapeDtypeStruct((batch, sub, dil * DIL_GW), jnp.float32)],
        compiler_params=_cparams(("arbitrary",) * 3),
        name=f"dilated_attn_d{dil}",
    )(a, a, a, a, a)
    rows = batch * sub
    return o.reshape(rows, dil * DIL_GW), lse.reshape(rows, dil * DIL_GW)


def _swa_kernel(sink_ref, q_ref, kvc_ref, kvp_ref, o_ref):
    first_tile = pl.program_id(1) == 0
    lane_lo = lax.broadcasted_iota(jnp.int32, (BLK, LANES), 1) < HEAD_DIM
    zero = jnp.zeros((), jnp.bfloat16)
    rep = SWA_Q_HEADS // SWA_KV_HEADS
    kw = SWA_KV_HEADS * LANES
    for j in range(TQ // BLK):
        rows = slice(j * BLK, (j + 1) * BLK)
        if j == 0:
            kvprev = kvp_ref[...]
            mask = _band_mask(SWA_WINDOW - 1, first_tile)
        else:
            kvprev = kvc_ref[(j - 1) * BLK:j * BLK, :]
            mask = _band_mask(SWA_WINDOW - 1, False)
        kv = jnp.concatenate([kvprev, kvc_ref[rows, :]], axis=0)
        for g in range(SWA_KV_HEADS):
            kk = kv[:, g * LANES:(g + 1) * LANES]
            vv = kv[:, kw + g * LANES:kw + (g + 1) * LANES]
            for p in range(rep // 2):
                h0 = g * rep + 2 * p
                lanes = slice(h0 * HEAD_DIM, (h0 + 2) * HEAD_DIM)
                qp = q_ref[rows, lanes]
                oa, _, _ = _head_attention(jnp.where(lane_lo, qp, zero), kk, vv, mask,
                                           sink_ref[h0])
                ob, _, _ = _head_attention(jnp.where(lane_lo, zero, qp), kk, vv, mask,
                                           sink_ref[h0 + 1])
                o_ref[rows, lanes] = jnp.where(lane_lo, oa, ob).astype(o_ref.dtype)


def _swa_attention(q, kv, sinks, batch, seq):
    q3 = q.reshape(batch, seq, q.shape[1])
    kv3 = kv.reshape(batch, seq, kv.shape[1])
    nq = TQ // BLK
    o = pl.pallas_call(
        _swa_kernel,
        grid_spec=pltpu.PrefetchScalarGridSpec(
            num_scalar_prefetch=1,
            grid=(batch, seq // TQ),
            in_specs=[pl.BlockSpec((None, TQ, q.shape[1]), lambda b, i, s: (b, i, 0)),
                      pl.BlockSpec((None, TQ, kv.shape[1]), lambda b, i, s: (b, i, 0)),
                      pl.BlockSpec((None, BLK, kv.shape[1]),
                                   lambda b, i, s: (b, jnp.maximum(i * nq - 1, 0), 0))],
            out_specs=pl.BlockSpec((None, TQ, q.shape[1]), lambda b, i, s: (b, i, 0))),
        out_shape=jax.ShapeDtypeStruct(q3.shape, jnp.bfloat16),
        compiler_params=_cparams(("arbitrary",) * 2),
        name="swa_attn",
    )(sinks.astype(jnp.float32), q3, kv3, kv3)
    return o.reshape(batch * seq, q.shape[1])


def _odd_in_kernel(x_ref, g_ref, w_ref, cos_ref, sin_ref, qn_ref, kn_ref, bd_ref, q_ref, kv_ref):
    hn = _rms(x_ref[...], g_ref[...]).astype(jnp.bfloat16)
    qw = SWA_Q_HEADS * HEAD_DIM
    kw = SWA_KV_HEADS * LANES
    cos = cos_ref[...]
    sin = sin_ref[...]
    q = _dot(hn, w_ref[:, 0:qw])
    q_ref[...] = _head_norm_rope(q, bd_ref[...], qn_ref[...], cos, sin).astype(q_ref.dtype)
    k = _dot(hn, w_ref[:, qw:qw + kw])
    kv_ref[:, 0:kw] = _head_norm_rope(k, bd_ref[...], kn_ref[...], cos, sin).astype(kv_ref.dtype)
    kv_ref[:, kw:2 * kw] = _dot(hn, w_ref[:, qw + kw:qw + 2 * kw]).astype(kv_ref.dtype)


def _odd_in(x, gain, w_in, cos, sin, qn, kn, ones_bd):
    t = x.shape[0]
    qw = SWA_Q_HEADS * HEAD_DIM
    kvw = SWA_KV_HEADS * HEAD_DIM
    dup = lambda w: jnp.concatenate(
        [w[:, h * HEAD_DIM:(h + 1) * HEAD_DIM] for h in range(SWA_KV_HEADS) for _ in range(2)],
        axis=1)
    w_all = jnp.concatenate([w_in[:, :qw], dup(w_in[:, qw:qw + kvw]),
                             dup(w_in[:, qw + kvw:qw + 2 * kvw])], axis=1).astype(jnp.bfloat16)
    kw = SWA_KV_HEADS * LANES
    qn_row = jnp.tile(qn * HEAD_DIM ** -0.5, qw // HEAD_DIM).reshape(1, qw)
    kn_row = jnp.tile(kn, kw // HEAD_DIM).reshape(1, kw)
    row = lambda n: pl.BlockSpec((TM, n), lambda i: (i, 0))
    return pl.pallas_call(
        _odd_in_kernel,
        grid=(t // TM,),
        in_specs=[row(D_MODEL), _const_spec((1, D_MODEL)), _const_spec(w_all.shape),
                  row(LANES), row(LANES), _const_spec((1, qw)), _const_spec((1, kw)),
                  _const_spec(ones_bd.shape)],
        out_specs=[row(qw), row(2 * kw)],
        out_shape=[jax.ShapeDtypeStruct((t, qw), jnp.bfloat16),
                   jax.ShapeDtypeStruct((t, 2 * kw), jnp.bfloat16)],
        compiler_params=_cparams(("arbitrary",)),
        name="odd_in_proj",
    )(x, gain.reshape(1, D_MODEL), w_all, cos, sin, qn_row, kn_row, ones_bd)


def _route(hn, wr_ref, br_ref, carry_ref, idx_ref, gate_ref, cnt_ref):
    tm = hn.shape[0]
    h_hi = hn.astype(jnp.bfloat16)
    h_lo = (hn - h_hi.astype(jnp.float32)).astype(jnp.bfloat16)
    hi_pass = _dot(h_hi, wr_ref[...])
    logits = (hi_pass[:, :ROUTER_COLS] + _dot(h_lo, wr_ref[:, :ROUTER_COLS])
              + hi_pass[:, ROUTER_COLS:] + br_ref[...])
    lt = logits.T
    rowf = lax.broadcasted_iota(jnp.int32, (SUBLANES, tm), 0).astype(jnp.float32)

    def first_argmax(x):
        m = jnp.max(x, axis=0, keepdims=True)
        idx = jnp.min(jnp.where(x == m, rowf, float(SUBLANES)), axis=0, keepdims=True)
        return m, idx

    lg = jnp.where(rowf < N_GROUPS, lt[0:SUBLANES], NEG)
    mg, gi = first_argmax(lg)
    p_top = 1.0 / jnp.sum(jnp.exp(lg - mg), axis=0, keepdims=True)
    le = lt[ROUTER_E0:ROUTER_E0 + EPG]
    for g in range(1, N_GROUPS):
        le = jnp.where(gi == float(g), lt[ROUTER_E0 + g * EPG:ROUTER_E0 + (g + 1) * EPG], le)
    m1, i1 = first_argmax(le)
    m2, i2 = first_argmax(jnp.where(rowf == i1, NEG, le))
    e2 = jnp.exp(m2 - m1)
    gate1 = p_top / (1.0 + e2)
    gate2 = p_top * e2 / (1.0 + e2)
    a = jnp.minimum(i1, i2)
    b = jnp.maximum(i1, i2)
    gate_a = jnp.where(i1 < i2, gate1, gate2)
    gate_b = jnp.where(i1 < i2, gate2, gate1)
    pair = a * (EPG - 1) - a * (a - 1.0) * 0.5 + (b - a - 1.0)
    cls = gi * PAIRS_PER_GROUP + pair

    crow = lax.broadcasted_iota(jnp.int32, (CLASS_ROWS, tm), 0).astype(jnp.float32)
    member = crow == cls
    s_idx = lax.broadcasted_iota(jnp.int32, (tm, tm), 0)
    t_idx = lax.broadcasted_iota(jnp.int32, (tm, tm), 1)
    upper = jnp.where(s_idx < t_idx, 1.0, 0.0).astype(jnp.bfloat16)
    ones = jnp.where(member, 1.0, 0.0)
    before = carry_ref[:, 0:1] + _dot(ones.astype(jnp.bfloat16), upper)
    rank = jnp.sum(jnp.where(member, before, 0.0), axis=0, keepdims=True)
    carry_ref[...] = carry_ref[...] + jnp.sum(ones, axis=1, keepdims=True)
    cnt_ref[...] = carry_ref[...].astype(jnp.int32)

    idx = jnp.where(rowf == 0.0, cls, jnp.where(rowf == 1.0, rank, 0.0))
    idx_ref[...] = idx.astype(jnp.int32)
    gate_ref[...] = jnp.where(rowf == 0.0, gate_a, jnp.where(rowf == 1.0, gate_b, 0.0))


def _store_row_tiles(ref, x):
    n = x.shape[0]
    for c in range(ROW_CHUNKS):
        ref[pl.ds(c, n, stride=ROW_CHUNKS), :] = x[:, c * LANES:(c + 1) * LANES].astype(ref.dtype)


def _load_row_tiles(ref, n):
    return jnp.concatenate([ref[pl.ds(c, n, stride=ROW_CHUNKS), :] for c in range(ROW_CHUNKS)],
                           axis=1)


def _post_proj(y, x_ref, gf_ref, wr_ref, br_ref, carry_ref,
               h_ref, hn_ref, idx_ref, gate_ref, cnt_ref):
    @pl.when(pl.program_id(0) == 0)
    def _():
        carry_ref[...] = jnp.zeros_like(carry_ref)

    h = x_ref[...] + y
    h_ref[...] = h
    hn = _rms(h, gf_ref[...])
    _store_row_tiles(hn_ref, hn)
    _route(hn, wr_ref, br_ref, carry_ref, idx_ref, gate_ref, cnt_ref)


def _token_major(ref, dil, scr):
    if dil == 1:
        return ref[...].astype(jnp.float32)
    sub = ref.shape[0]
    n = sub * dil
    chunks = DIL_GW // LANES
    for r in range(dil):
        for c in range(chunks):
            l0 = r * DIL_GW + c * LANES
            scr[pl.ds(c * n + r, sub, stride=dil), :] = ref[:, l0:l0 + LANES].astype(jnp.float32)
    return jnp.concatenate([scr[c * n:(c + 1) * n, :] for c in range(chunks)], axis=1)


def _even_out_kernel(a_ref, o0_ref, o1_ref, o2_ref, l0_ref, l1_ref, l2_ref, w_ref, x_ref, gf_ref,
                     wr_ref, br_ref, h_ref, hn_ref, idx_ref, gate_ref, cnt_ref,
                     carry_ref, so1, so2, sl1, sl2):
    dils = [d for _, d in DIL_GROUPS]
    o0, o1, o2 = (_token_major(r, d, s) for r, d, s in zip((o0_ref, o1_ref, o2_ref), dils,
                                                            (None, so1, so2)))
    l0, l1, l2 = (_token_major(r, d, s) for r, d, s in zip((l0_ref, l1_ref, l2_ref), dils,
                                                            (None, sl1, sl2)))
    m = jnp.maximum(jnp.maximum(l0, l1), l2)
    e0, e1, e2 = jnp.exp(l0 - m), jnp.exp(l1 - m), jnp.exp(l2 - m)
    b = (e0 * o0 + e1 * o1 + e2 * o2) / (e0 + e1 + e2)
    y = _dot(a_ref[...], w_ref[0:SGU_WIDTH, :]) + _dot(b.astype(jnp.bfloat16),
                                                       w_ref[SGU_WIDTH:, :])
    _post_proj(y, x_ref, gf_ref, wr_ref, br_ref, carry_ref,
               h_ref, hn_ref, idx_ref, gate_ref, cnt_ref)


def _odd_out_kernel(o_ref, w_ref, x_ref, gf_ref, wr_ref, br_ref,
                    h_ref, hn_ref, idx_ref, gate_ref, cnt_ref, carry_ref):
    y = _dot(o_ref[...], w_ref[...])
    _post_proj(y, x_ref, gf_ref, wr_ref, br_ref, carry_ref,
               h_ref, hn_ref, idx_ref, gate_ref, cnt_ref)


def _router_weights(w_rg, b_rg, w_re, b_re):
    d = w_rg.shape[0]
    w = jnp.zeros((d, ROUTER_COLS), jnp.float32)
    w = w.at[:, 0:N_GROUPS].set(w_rg)
    w = w.at[:, ROUTER_E0:ROUTER_E0 + N_EXPERTS].set(
        jnp.transpose(w_re, (1, 0, 2)).reshape(d, N_EXPERTS))
    b = jnp.zeros((1, ROUTER_COLS), jnp.float32)
    b = b.at[0, 0:N_GROUPS].set(b_rg)
    b = b.at[0, ROUTER_E0:ROUTER_E0 + N_EXPERTS].set(b_re.reshape(N_EXPERTS))
    w_hi = w.astype(jnp.bfloat16)
    w_lo = (w - w_hi.astype(jnp.float32)).astype(jnp.bfloat16)
    return jnp.concatenate([w_hi, w_lo], axis=1), b


def _out_proj(kernel, acts, w_out, x2, gain_ffn, router, extra_scratch=()):
    t = x2.shape[0]
    w_router, b = router
    row = lambda n: pl.BlockSpec((TM, n), lambda i: (i, 0))
    colblk = pl.BlockSpec((SUBLANES, TM), lambda i: (0, i))
    return pl.pallas_call(
        kernel,
        grid=(t // TM,),
        in_specs=[pl.BlockSpec((TM * a.shape[0] // t, a.shape[1]), lambda i: (i, 0)) for a in acts]
        + [_const_spec(w_out.shape), row(D_MODEL), _const_spec((1, D_MODEL)),
           _const_spec(w_router.shape), _const_spec(b.shape)],
        out_specs=[row(D_MODEL), pl.BlockSpec((TM * ROW_CHUNKS, LANES), lambda i: (i, 0)),
                   colblk, colblk,
                   _const_spec((CLASS_ROWS, LANES))],
        out_shape=[jax.ShapeDtypeStruct((t, D_MODEL), jnp.float32),
                   jax.ShapeDtypeStruct((t * ROW_CHUNKS, LANES), ROW_DTYPE),
                   jax.ShapeDtypeStruct((SUBLANES, t), jnp.int32),
                   jax.ShapeDtypeStruct((SUBLANES, t), jnp.float32),
                   jax.ShapeDtypeStruct((CLASS_ROWS, LANES), jnp.int32)],
        scratch_shapes=[pltpu.VMEM((CLASS_ROWS, LANES), jnp.float32)] + list(extra_scratch),
        compiler_params=_cparams(("arbitrary",)),
        name=kernel.__name__.strip("_"),
    )(*acts, w_out.astype(jnp.bfloat16), x2, gain_ffn.reshape(1, D_MODEL), w_router, b)


DISPATCH_TM = 1024
ISSUE_UNROLL = 8
OUT_TILES = TOP_K


def _row_tile(ref, row, tiles=1):
    n = tiles * ROW_CHUNKS
    return ref.at[pl.ds(pl.multiple_of(row * n, n), n), :]


def _wait_rows(copy, n):
    def body(_, c):
        copy.wait()
        return c

    lax.fori_loop(0, n, body, 0, unroll=16)


def _dispatch_kernel(pstart_ref, pend_ref, dest_ref, hn_ref, xs_ref, zero_ref, sem, zsem):
    blk = MOE_BM * ROW_CHUNKS

    def zero_block(row0):
        return pltpu.make_async_copy(
            zero_ref, xs_ref.at[pl.ds(pl.multiple_of(row0 * ROW_CHUNKS, blk), blk), :], zsem)

    @pl.when(pl.program_id(0) == 0)
    def _():
        zero_ref[...] = jnp.zeros_like(zero_ref)

        def start_last(c, _):
            @pl.when(pend_ref[c] > pstart_ref[c])
            def _():
                zero_block(pend_ref[c] - MOE_BM).start()
            return 0

        def wait_last(c, _):
            @pl.when(pend_ref[c] > pstart_ref[c])
            def _():
                zero_block(pend_ref[c] - MOE_BM).wait()
            return 0

        lax.fori_loop(0, N_CLASSES, start_last, 0)
        lax.fori_loop(0, N_CLASSES, wait_last, 0)
        first_unused = lax.div(pend_ref[N_CLASSES - 1], MOE_BM)
        n_blocks = xs_ref.shape[0] // blk

        def start_tail(b, _):
            zero_block(b * MOE_BM).start()
            return 0

        def wait_tail(b, _):
            zero_block(b * MOE_BM).wait()
            return 0

        lax.fori_loop(first_unused, n_blocks, start_tail, 0)
        lax.fori_loop(first_unused, n_blocks, wait_tail, 0)

    def row_copy(src, dst):
        return pltpu.make_async_copy(_row_tile(hn_ref, src), _row_tile(xs_ref, dst), sem)

    def issue(b, c):
        for j in range(ISSUE_UNROLL):
            t = b * ISSUE_UNROLL + j
            row_copy(t, dest_ref[0, t]).start(priority=j % 2)
        return c

    lax.fori_loop(0, DISPATCH_TM // ISSUE_UNROLL, issue, 0)
    _wait_rows(row_copy(0, 0), DISPATCH_TM)


def _dispatch(hn, dest, pstart, pend, rows):
    t = hn.shape[0] // ROW_CHUNKS
    n = t // DISPATCH_TM
    return pl.pallas_call(
        _dispatch_kernel,
        grid_spec=pltpu.PrefetchScalarGridSpec(
            num_scalar_prefetch=2,
            grid=(n,),
            in_specs=[pl.BlockSpec((None, 1, DISPATCH_TM), lambda i, ps, pe: (i, 0, 0),
                                   memory_space=pltpu.SMEM),
                      pl.BlockSpec((DISPATCH_TM * ROW_CHUNKS, LANES), lambda i, ps, pe: (i, 0))],
            out_specs=pl.BlockSpec(memory_space=pl.ANY),
            scratch_shapes=[pltpu.VMEM((MOE_BM * ROW_CHUNKS, LANES), ROW_DTYPE),
                            pltpu.SemaphoreType.DMA(()), pltpu.SemaphoreType.DMA(())]),
        out_shape=jax.ShapeDtypeStruct((rows * ROW_CHUNKS, LANES), ROW_DTYPE),
        compiler_params=_cparams(("arbitrary",)),
        name="moe_dispatch",
    )(pstart, pend, dest.reshape(n, 1, DISPATCH_TM), hn)


def _expert_kernel(blk_g_ref, blk_a_ref, blk_b_ref, nused_ref, xs_ref, wg_ref, wu_ref, wd_ref,
                   out_ref):
    i = pl.program_id(0)

    @pl.when(i < nused_ref[0])
    def _():
        x = _load_row_tiles(xs_ref, MOE_BM).astype(jnp.bfloat16)
        for k, e_ref in enumerate((blk_a_ref, blk_b_ref)):
            e = e_ref[i]
            hid = jax.nn.silu(_dot(x, wg_ref[e])) * _dot(x, wu_ref[e])
            y = _dot(hid.astype(jnp.bfloat16), wd_ref[e])
            for c in range(ROW_CHUNKS):
                out_ref[pl.ds(k * ROW_CHUNKS + c, MOE_BM, stride=OUT_TILES * ROW_CHUNKS), :] = (
                    y[:, c * LANES:(c + 1) * LANES])

    @pl.when(i >= nused_ref[0])
    def _():
        out_ref[...] = jnp.zeros_like(out_ref)


def _expert_ffn(xs, blk_g, blk_a, blk_b, nused, layer, w_gate, w_up, w_down):
    rows = xs.shape[0] // ROW_CHUNKS
    blk = MOE_BM * ROW_CHUNKS
    wspec = lambda shape: pl.BlockSpec((None, None, EPG) + shape,
                                       lambda i, bg, ba, bb, nu: (layer, bg[i], 0, 0, 0))
    return pl.pallas_call(
        _expert_kernel,
        grid_spec=pltpu.PrefetchScalarGridSpec(
            num_scalar_prefetch=4,
            grid=(rows // MOE_BM,),
            in_specs=[pl.BlockSpec((blk, LANES),
                                   lambda i, bg, ba, bb, nu: (jnp.minimum(i, nu[0] - 1), 0)),
                      wspec((D_MODEL, EXPERT_FF)), wspec((D_MODEL, EXPERT_FF)),
                      wspec((EXPERT_FF, D_MODEL))],
            out_specs=pl.BlockSpec((OUT_TILES * blk, LANES), lambda i, bg, ba, bb, nu: (i, 0))),
        out_shape=jax.ShapeDtypeStruct((OUT_TILES * rows * ROW_CHUNKS, LANES), ROW_DTYPE),
        compiler_params=_cparams(("arbitrary",)),
        name="moe_expert_ffn",
    )(blk_g, blk_a, blk_b, nused, xs, w_gate, w_up, w_down)


def _gather_combine(dest_ref, dest_next_ref, h_ref, gate_ref, ys_ref, buf_ref, sems):
    i = pl.program_id(0)
    n = pl.num_programs(0)

    def row_copy(src, slot, t):
        return pltpu.make_async_copy(_row_tile(ys_ref, src, OUT_TILES),
                                     _row_tile(buf_ref.at[slot], t, OUT_TILES), sems.at[slot])

    def issue_tile(d_ref, slot):
        def issue(b, c):
            for j in range(ISSUE_UNROLL):
                t = b * ISSUE_UNROLL + j
                row_copy(d_ref[0, t], slot, t).start(priority=j % 2)
            return c

        lax.fori_loop(0, TM // ISSUE_UNROLL, issue, 0)

    slot = lax.rem(i, 2)

    @pl.when(i == 0)
    def _():
        issue_tile(dest_ref, 0)

    @pl.when(i + 1 < n)
    def _():
        issue_tile(dest_next_ref, 1 - slot)

    _wait_rows(row_copy(0, slot, 0), TM)
    g = gate_ref[...]
    stride = OUT_TILES * ROW_CHUNKS
    buf = buf_ref.at[slot]
    ya, yb = (jnp.concatenate([buf[pl.ds(k * ROW_CHUNKS + c, TM, stride=stride), :]
                               for c in range(ROW_CHUNKS)], axis=1) for k in range(OUT_TILES))
    return h_ref[...] + (ya * g[:, 0:1] + yb * g[:, 1:2])


def _combine_kernel(dest_ref, dest_next_ref, h_ref, gate_ref, ys_ref, out_ref, buf_ref, sems):
    out_ref[...] = _gather_combine(dest_ref, dest_next_ref, h_ref, gate_ref, ys_ref, buf_ref, sems)


def _combine(h, gate_cols, ys, dest):
    t = h.shape[0]
    n = t // TM
    dest_tiles = dest.reshape(n, 1, TM)
    dspec = lambda f: pl.BlockSpec((None, 1, TM), f, memory_space=pltpu.SMEM)
    return pl.pallas_call(
        _combine_kernel,
        grid=(n,),
        in_specs=[dspec(lambda i: (i, 0, 0)), dspec(lambda i: (jnp.minimum(i + 1, n - 1), 0, 0)),
                  pl.BlockSpec((TM, D_MODEL), lambda i: (i, 0)),
                  pl.BlockSpec((TM, SUBLANES), lambda i: (i, 0)),
                  pl.BlockSpec(memory_space=pl.ANY)],
        out_specs=pl.BlockSpec((TM, D_MODEL), lambda i: (i, 0)),
        scratch_shapes=[pltpu.VMEM((2, OUT_TILES * TM * ROW_CHUNKS, LANES), ROW_DTYPE),
                        pltpu.SemaphoreType.DMA((2,))],
        out_shape=jax.ShapeDtypeStruct((t, D_MODEL), jnp.float32),
        compiler_params=_cparams(("arbitrary",)),
        name="moe_combine",
    )(dest_tiles, dest_tiles, h, gate_cols, ys)


def _moe(h, hn, idx, gates, counts, layer, w_gate, w_up, w_down):
    t = h.shape[0]
    rows = t + N_CLASSES * MOE_BM
    nblk = rows // MOE_BM
    cnt = counts[:N_CLASSES, 0]
    padded = (cnt + MOE_BM - 1) // MOE_BM * MOE_BM
    pend = jnp.cumsum(padded).astype(jnp.int32)
    pstart = pend - padded
    blk_row0 = jnp.arange(nblk, dtype=jnp.int32) * MOE_BM
    blk_c = jnp.minimum(jnp.sum(pend[None, :] <= blk_row0[:, None], axis=1), N_CLASSES - 1)
    pairs = [(a, b) for a in range(EPG) for b in range(a + 1, EPG)]
    blk_pair = blk_c % PAIRS_PER_GROUP
    blk_g = (blk_c // PAIRS_PER_GROUP).astype(jnp.int32)
    blk_a = jnp.asarray([p[0] for p in pairs], jnp.int32)[blk_pair]
    blk_b = jnp.asarray([p[1] for p in pairs], jnp.int32)[blk_pair]
    nused = (pend[-1:] // MOE_BM).astype(jnp.int32)
    cls, rank = idx[0], idx[1]
    dest = rank + jnp.sum(jnp.where(cls[:, None] == jnp.arange(N_CLASSES), pstart, 0), axis=-1)
    xs = _dispatch(hn, dest, pstart, pend, rows)
    ys = _expert_ffn(xs, blk_g, blk_a, blk_b, nused, layer, w_gate, w_up, w_down)
    return _combine(h, gates.T, ys, dest)


def kernel(x, positions, norm_mix, norm_ffn, w_in_even, w_out_even, sgu_norm, sgu_w, sgu_b,
           qn_dil, kn_dil, w_in_odd, w_out_odd, qn_swa, kn_swa, sinks,
           w_router_g, b_router_g, w_router_e, b_router_e, w_gate, w_up, w_down):
    batch, seq, d = x.shape
    t = batch * seq
    depth = norm_mix.shape[0]
    cos, sin = _rope_tables(positions)
    bd = 2 * LANES
    ones_bd = (jnp.arange(bd)[:, None] // HEAD_DIM == jnp.arange(bd)[None, :] // HEAD_DIM
               ).astype(jnp.bfloat16)
    expert_w = [w.astype(jnp.bfloat16).reshape(depth, N_GROUPS, EPG, *w.shape[2:])
                for w in (w_gate, w_up, w_down)]
    h = x.reshape(t, d)
    for layer in range(depth):
        i = layer // 2
        router = _router_weights(w_router_g[layer], b_router_g[layer], w_router_e[layer],
                                 b_router_e[layer])
        if layer % 2 == 0:
            a, *qkv = _even_in(h, norm_mix[layer], w_in_even[i], cos, sin, sgu_norm[i],
                               sgu_w[i], sgu_b[i], qn_dil[i], kn_dil[i], ones_bd)
            outs, lses = [], []
            for g, (window, dil) in enumerate(DIL_GROUPS):
                o, lse = _dilated_group_attention(qkv[g], batch, seq, dil, window)
                outs.append(o)
                lses.append(lse)
            stage = [pltpu.VMEM((TM * DIL_GW // LANES, LANES), jnp.float32)] * 4
            res = _out_proj(_even_out_kernel, [a] + outs + lses, w_out_even[i], h,
                            norm_ffn[layer], router, stage)
        else:
            q, kv = _odd_in(h, norm_mix[layer], w_in_odd[i], cos, sin, qn_swa[i], kn_swa[i],
                            ones_bd)
            o = _swa_attention(q, kv, sinks[i], batch, seq)
            res = _out_proj(_odd_out_kernel, [o], w_out_odd[i], h, norm_ffn[layer], router)
        h_mid, hn, idx, gates, counts = res
        h = _moe(h_mid, hn, idx, gates, counts, layer, *expert_w)
    return h.reshape(batch, seq, d)
```

```python
import functools

import jax
import jax.numpy as jnp
from jax import lax
from jax.experimental import pallas as pl
from jax.experimental.pallas import tpu as pltpu

D_MODEL = 1024
HEAD_DIM = 64
BLK = 128
ROPE_THETA = 500000.0
ROT_DIM = HEAD_DIM // 4
HALF_ROT = ROT_DIM // 2
EPS = 1e-6
SGU_GROUPS = 8
SGU_WIDTH = SGU_GROUPS * HEAD_DIM
DIL_GROUPS = ((128, 1), (512, 4), (2048, 16))
DIL_HPG = 4
DIL_GW = DIL_HPG * HEAD_DIM
DIL_WIDTH = DIL_GW * len(DIL_GROUPS)
SWA_Q_HEADS = 16
SWA_KV_HEADS = 2
SWA_WINDOW = 128
N_GROUPS = 4
EPG = 8
N_EXPERTS = N_GROUPS * EPG
EXPERT_FF = 256
TOP_K = 2

LANES = 128
SUBLANES = 8
VMEM_LIMIT_BYTES = 48 * 1024 * 1024

TM = 512
TQ = 512
MOE_BM = 256
NEG = -1e30

ROW_CHUNKS = D_MODEL // LANES
ROW_DTYPE = jnp.float32
ROUTER_COLS = LANES
ROUTER_E0 = SUBLANES


def _cparams(sem):
    return pltpu.CompilerParams(dimension_semantics=sem, vmem_limit_bytes=VMEM_LIMIT_BYTES)


def _rms(x, gain_row):
    return x * lax.rsqrt(jnp.mean(x * x, axis=-1, keepdims=True) + EPS) * gain_row


def _dot(a, b):
    return jnp.dot(a, b, preferred_element_type=jnp.float32)


def _dot_nt(a, b):
    return lax.dot_general(a, b, (((1,), (1,)), ((), ())), preferred_element_type=jnp.float32)


def _rope_kernel(pos_ref, inv_ref, cos_ref, sin_ref):
    ang = inv_ref[...] * pos_ref[...].astype(jnp.float32)
    c = jnp.cos(ang)
    s = jnp.sin(ang)
    rest = HEAD_DIM // SUBLANES - 2
    head_c = [c, c] + [jnp.ones_like(c)] * rest
    head_s = [-s, s] + [jnp.zeros_like(s)] * rest
    cos_ref[...] = jnp.concatenate(head_c * 2, axis=0).T
    sin_ref[...] = jnp.concatenate(head_s * 2, axis=0).T


def _rope_tables(positions):
    t = positions.size
    inv = ROPE_THETA ** (-jnp.arange(0, ROT_DIM, 2, dtype=jnp.float32) / ROT_DIM)
    tm = 512
    return pl.pallas_call(
        _rope_kernel,
        grid=(t // tm,),
        in_specs=[pl.BlockSpec((1, tm), lambda i: (0, i)),
                  pl.BlockSpec((HALF_ROT, 1), lambda i: (0, 0))],
        out_specs=[pl.BlockSpec((tm, LANES), lambda i: (i, 0)),
                   pl.BlockSpec((tm, LANES), lambda i: (i, 0))],
        out_shape=[jax.ShapeDtypeStruct((t, LANES), jnp.float32)] * 2,
        compiler_params=_cparams(("arbitrary",)),
        name="rope_tables",
    )(positions.reshape(1, t), inv.reshape(HALF_ROT, 1))


def _head_norm_rope(x, ones_bd, gain_row, cos, sin):
    tm, w = x.shape
    sq = x * x
    sq_hi = sq.astype(jnp.bfloat16)
    sq_lo = (sq - sq_hi.astype(jnp.float32)).astype(jnp.bfloat16)
    outs = []
    bd = ones_bd.shape[0]
    for c in range(w // bd):
        sl = slice(c * bd, (c + 1) * bd)
        ss = _dot(sq_hi[:, sl], ones_bd) + _dot(sq_lo[:, sl], ones_bd)
        xn = x[:, sl] * lax.rsqrt(ss * (1.0 / HEAD_DIM) + EPS) * gain_row[:, sl]
        for j in range(bd // LANES):
            xj = xn[:, j * LANES:(j + 1) * LANES]
            lane = lax.broadcasted_iota(jnp.int32, xj.shape, 1) % HEAD_DIM
            rot = jnp.where(lane < HALF_ROT,
                            pltpu.roll(xj, LANES - HALF_ROT, axis=1),
                            pltpu.roll(xj, HALF_ROT, axis=1))
            outs.append(xj * cos + rot * sin)
    return jnp.concatenate(outs, axis=1)


def _residue_major(ref, n, w, dil):
    sub = n // dil
    return jnp.concatenate(
        [jnp.concatenate([ref[pl.ds(c * n + r, sub, stride=dil), :] for c in range(w // LANES)],
                         axis=1) for r in range(dil)], axis=0)


def _even_in_kernel(x_ref, g_ref, w_ref, cos_ref, sin_ref, sgn_ref, sgw_ref, sgb_ref,
                    qn_ref, kn_ref, bd_ref, a_ref, q0_ref, q1_ref, q2_ref, hn_scr):
    x = x_ref[...]
    tm = x.shape[0]
    hn32 = _rms(x, g_ref[...])
    for c in range(ROW_CHUNKS):
        hn_scr[c * tm:(c + 1) * tm, :] = hn32[:, c * LANES:(c + 1) * LANES]
    hn = hn32.astype(jnp.bfloat16)

    u = _dot(hn, w_ref[:, 0:SGU_WIDTH])
    v = _dot(hn, w_ref[:, SGU_WIDTH:2 * SGU_WIDTH])
    gu = jax.nn.gelu(u)
    vn = _rms(jax.nn.gelu(v), sgn_ref[...]).astype(jnp.bfloat16)

    row = lax.broadcasted_iota(jnp.int32, (BLK, BLK), 0)
    col = lax.broadcasted_iota(jnp.int32, (BLK, BLK), 1)
    tril = row >= col
    lane_lo = lax.broadcasted_iota(jnp.int32, (BLK, LANES), 1) < HEAD_DIM
    wts = [jnp.where(tril, sgw_ref[g], 0.0).astype(jnp.bfloat16) for g in range(SGU_GROUPS)]
    for c in range(tm // BLK):
        rows = slice(c * BLK, (c + 1) * BLK)
        for p in range(SGU_GROUPS // 2):
            lanes = slice(p * LANES, (p + 1) * LANES)
            vp = vn[rows, lanes]
            s = jnp.where(lane_lo, _dot(wts[2 * p], vp), _dot(wts[2 * p + 1], vp))
            a_ref[rows, lanes] = (gu[rows, lanes] * (s + sgb_ref[:, lanes])).astype(a_ref.dtype)

    o0 = 2 * SGU_WIDTH
    for g, ((_, dil), o_ref) in enumerate(zip(DIL_GROUPS, (q0_ref, q1_ref, q2_ref))):
        if dil == 1:
            hg, cos, sin = hn, cos_ref[...], sin_ref[...]
        else:
            hg = _residue_major(hn_scr, tm, D_MODEL, dil).astype(jnp.bfloat16)
            cos = _residue_major(cos_ref, tm, LANES, dil)
            sin = _residue_major(sin_ref, tm, LANES, dil)
        gl = slice(g * DIL_GW, (g + 1) * DIL_GW)
        parts = []
        for p, gain_ref in enumerate((qn_ref, kn_ref, None)):
            c0 = o0 + p * DIL_WIDTH + g * DIL_GW
            y = _dot(hg, w_ref[:, c0:c0 + DIL_GW])
            if gain_ref is not None:
                y = _head_norm_rope(y, bd_ref[...], gain_ref[:, gl], cos, sin)
            parts.append(y.astype(o_ref.dtype))
        sub = tm // dil
        for r in range(dil):
            for p, y in enumerate(parts):
                c0 = (r * 3 + p) * DIL_GW
                o_ref[:, c0:c0 + DIL_GW] = y[r * sub:(r + 1) * sub, :]


def _const_spec(shape):
    nd = len(shape)
    return pl.BlockSpec(shape, lambda i: (0,) * nd)


def _even_in(x2, gain, w_in, cos, sin, sgu_norm, sgu_w, sgu_b, qn, kn, ones_bd):
    t = x2.shape[0]
    in_w = w_in.shape[1]
    sgb = jnp.repeat(sgu_b.T, HEAD_DIM, axis=1)
    qn_row = jnp.tile(qn * HEAD_DIM ** -0.5, DIL_WIDTH // HEAD_DIM).reshape(1, DIL_WIDTH)
    kn_row = jnp.tile(kn, DIL_WIDTH // HEAD_DIM).reshape(1, DIL_WIDTH)
    row = lambda n: pl.BlockSpec((TM, n), lambda i: (i, 0))
    return pl.pallas_call(
        _even_in_kernel,
        grid=(t // TM,),
        in_specs=[row(D_MODEL), _const_spec((1, D_MODEL)), _const_spec((D_MODEL, in_w)),
                  row(LANES), row(LANES), _const_spec((1, SGU_WIDTH)),
                  _const_spec((SGU_GROUPS, BLK, BLK)), _const_spec((BLK, SGU_WIDTH)),
                  _const_spec((1, DIL_WIDTH)), _const_spec((1, DIL_WIDTH)),
                  _const_spec(ones_bd.shape)],
        out_specs=[row(SGU_WIDTH)] + [pl.BlockSpec((TM // d, d * 3 * DIL_GW), lambda i: (i, 0))
                                      for _, d in DIL_GROUPS],
        out_shape=[jax.ShapeDtypeStruct((t, SGU_WIDTH), jnp.bfloat16)]
        + [jax.ShapeDtypeStruct((t // d, d * 3 * DIL_GW), jnp.bfloat16) for _, d in DIL_GROUPS],
        scratch_shapes=[pltpu.VMEM((ROW_CHUNKS * TM, LANES), jnp.float32)],
        compiler_params=_cparams(("arbitrary",)),
        name="even_in_proj",
    )(x2, gain.reshape(1, D_MODEL), w_in.astype(jnp.bfloat16), cos, sin,
      sgu_norm.reshape(1, SGU_WIDTH), sgu_w, sgb, qn_row, kn_row, ones_bd)


def _band_mask(max_rel, first):
    i = lax.broadcasted_iota(jnp.int32, (BLK, 2 * BLK), 0)
    j = lax.broadcasted_iota(jnp.int32, (BLK, 2 * BLK), 1)
    rel = BLK + i - j
    first_key = jnp.where(first, BLK, 0)
    return (rel >= 0) & (rel <= max_rel) & (j >= first_key)


def _head_attention(qh, kk, vv, mask, sink):
    s = jnp.where(mask, _dot_nt(qh, kk), NEG)
    m = jnp.max(s, axis=-1, keepdims=True)
    if sink is not None:
        m = jnp.maximum(m, sink)
    p = jnp.exp(s - m)
    denom = jnp.sum(p, axis=-1, keepdims=True)
    if sink is not None:
        denom = denom + jnp.exp(sink - m)
    o = _dot(p.astype(jnp.bfloat16), vv)
    return o / denom, m, denom


def _dil_attn_kernel(q_ref, kc_ref, kp_ref, vc_ref, vp_ref, o_ref, lse_ref, *, max_rel):
    first_tile = pl.program_id(2) == 0
    lane_lo = lax.broadcasted_iota(jnp.int32, (BLK, LANES), 1) < HEAD_DIM
    zero = jnp.zeros((), jnp.bfloat16)
    for j in range(TQ // BLK):
        rows = slice(j * BLK, (j + 1) * BLK)
        if j == 0:
            kprev, vprev = kp_ref[...], vp_ref[...]
            mask = _band_mask(max_rel, first_tile)
        else:
            prows = slice((j - 1) * BLK, j * BLK)
            kprev, vprev = kc_ref[prows, :], vc_ref[prows, :]
            mask = _band_mask(max_rel, False)
        kk = jnp.concatenate([kprev, kc_ref[rows, :]], axis=0)
        vv = jnp.concatenate([vprev, vc_ref[rows, :]], axis=0)
        for p in range(DIL_GW // LANES):
            lanes = slice(p * LANES, (p + 1) * LANES)
            qp = q_ref[rows, lanes]
            oa, ma, da = _head_attention(jnp.where(lane_lo, qp, zero), kk[:, lanes],
                                         vv[:, lanes], mask, None)
            ob, mb, db = _head_attention(jnp.where(lane_lo, zero, qp), kk[:, lanes],
                                         vv[:, lanes], mask, None)
            o_ref[rows, lanes] = jnp.where(lane_lo, oa, ob).astype(o_ref.dtype)
            lse_ref[rows, lanes] = jnp.where(lane_lo, ma + jnp.log(da), mb + jnp.log(db))


def _dilated_group_attention(qkv, batch, seq, dil, window):
    sub = seq // dil
    a = qkv.reshape(batch, sub, dil * 3 * DIL_GW)
    nq = TQ // BLK
    cur = lambda part: pl.BlockSpec((None, TQ, DIL_GW), lambda b, r, i: (b, i, r * 3 + part))
    prev = lambda part: pl.BlockSpec(
        (None, BLK, DIL_GW), lambda b, r, i: (b, jnp.maximum(i * nq - 1, 0), r * 3 + part))
    out = pl.BlockSpec((None, TQ, DIL_GW), lambda b, r, i: (b, i, r))
    o, lse = pl.pallas_call(
        functools.partial(_dil_attn_kernel, max_rel=window // dil),
        grid=(batch, dil, sub // TQ),
        in_specs=[cur(0), cur(1), prev(1), cur(2), prev(2)],
        out_specs=[out, out],
        out_shape=[jax.ShapeDtypeStruct((batch, sub, dil * DIL_GW), jnp.bfloat16),
                   jax.ShapeDtypeStruct((batch, sub, dil * DIL_GW), jnp.float32)],
        compiler_params=_cparams(("arbitrary",) * 3),
        name=f"dilated_attn_d{dil}",
    )(a, a, a, a, a)
    rows = batch * sub
    return o.reshape(rows, dil * DIL_GW), lse.reshape(rows, dil * DIL_GW)


def _swa_kernel(sink_ref, q_ref, kvc_ref, kvp_ref, o_ref):
    first_tile = pl.program_id(1) == 0
    lane_lo = lax.broadcasted_iota(jnp.int32, (BLK, LANES), 1) < HEAD_DIM
    zero = jnp.zeros((), jnp.bfloat16)
    rep = SWA_Q_HEADS // SWA_KV_HEADS
    kw = SWA_KV_HEADS * LANES
    for j in range(TQ // BLK):
        rows = slice(j * BLK, (j + 1) * BLK)
        if j == 0:
            kvprev = kvp_ref[...]
            mask = _band_mask(SWA_WINDOW - 1, first_tile)
        else:
            kvprev = kvc_ref[(j - 1) * BLK:j * BLK, :]
            mask = _band_mask(SWA_WINDOW - 1, False)
        kv = jnp.concatenate([kvprev, kvc_ref[rows, :]], axis=0)
        for g in range(SWA_KV_HEADS):
            kk = kv[:, g * LANES:(g + 1) * LANES]
            vv = kv[:, kw + g * LANES:kw + (g + 1) * LANES]
            for p in range(rep // 2):
                h0 = g * rep + 2 * p
                lanes = slice(h0 * HEAD_DIM, (h0 + 2) * HEAD_DIM)
                qp = q_ref[rows, lanes]
                oa, _, _ = _head_attention(jnp.where(lane_lo, qp, zero), kk, vv, mask,
                                           sink_ref[h0])
                ob, _, _ = _head_attention(jnp.where(lane_lo, zero, qp), kk, vv, mask,
                                           sink_ref[h0 + 1])
                o_ref[rows, lanes] = jnp.where(lane_lo, oa, ob).astype(o_ref.dtype)


def _swa_attention(q, kv, sinks, batch, seq):
    q3 = q.reshape(batch, seq, q.shape[1])
    kv3 = kv.reshape(batch, seq, kv.shape[1])
    nq = TQ // BLK
    o = pl.pallas_call(
        _swa_kernel,
        grid_spec=pltpu.PrefetchScalarGridSpec(
            num_scalar_prefetch=1,
            grid=(batch, seq // TQ),
            in_specs=[pl.BlockSpec((None, TQ, q.shape[1]), lambda b, i, s: (b, i, 0)),
                      pl.BlockSpec((None, TQ, kv.shape[1]), lambda b, i, s: (b, i, 0)),
                      pl.BlockSpec((None, BLK, kv.shape[1]),
                                   lambda b, i, s: (b, jnp.maximum(i * nq - 1, 0), 0))],
            out_specs=pl.BlockSpec((None, TQ, q.shape[1]), lambda b, i, s: (b, i, 0))),
        out_shape=jax.ShapeDtypeStruct(q3.shape, jnp.bfloat16),
        compiler_params=_cparams(("arbitrary",) * 2),
        name="swa_attn",
    )(sinks.astype(jnp.float32), q3, kv3, kv3)
    return o.reshape(batch * seq, q.shape[1])


def _odd_in_kernel(x_ref, g_ref, w_ref, cos_ref, sin_ref, qn_ref, kn_ref, bd_ref, q_ref, kv_ref):
    hn = _rms(x_ref[...], g_ref[...]).astype(jnp.bfloat16)
    qw = SWA_Q_HEADS * HEAD_DIM
    kw = SWA_KV_HEADS * LANES
    cos = cos_ref[...]
    sin = sin_ref[...]
    q = _dot(hn, w_ref[:, 0:qw])
    q_ref[...] = _head_norm_rope(q, bd_ref[...], qn_ref[...], cos, sin).astype(q_ref.dtype)
    k = _dot(hn, w_ref[:, qw:qw + kw])
    kv_ref[:, 0:kw] = _head_norm_rope(k, bd_ref[...], kn_ref[...], cos, sin).astype(kv_ref.dtype)
    kv_ref[:, kw:2 * kw] = _dot(hn, w_ref[:, qw + kw:qw + 2 * kw]).astype(kv_ref.dtype)


def _odd_in(x, gain, w_in, cos, sin, qn, kn, ones_bd):
    t = x.shape[0]
    qw = SWA_Q_HEADS * HEAD_DIM
    kvw = SWA_KV_HEADS * HEAD_DIM
    dup = lambda w: jnp.concatenate(
        [w[:, h * HEAD_DIM:(h + 1) * HEAD_DIM] for h in range(SWA_KV_HEADS) for _ in range(2)],
        axis=1)
    w_all = jnp.concatenate([w_in[:, :qw], dup(w_in[:, qw:qw + kvw]),
                             dup(w_in[:, qw + kvw:qw + 2 * kvw])], axis=1).astype(jnp.bfloat16)
    kw = SWA_KV_HEADS * LANES
    qn_row = jnp.tile(qn * HEAD_DIM ** -0.5, qw // HEAD_DIM).reshape(1, qw)
    kn_row = jnp.tile(kn, kw // HEAD_DIM).reshape(1, kw)
    row = lambda n: pl.BlockSpec((TM, n), lambda i: (i, 0))
    return pl.pallas_call(
        _odd_in_kernel,
        grid=(t // TM,),
        in_specs=[row(D_MODEL), _const_spec((1, D_MODEL)), _const_spec(w_all.shape),
                  row(LANES), row(LANES), _const_spec((1, qw)), _const_spec((1, kw)),
                  _const_spec(ones_bd.shape)],
        out_specs=[row(qw), row(2 * kw)],
        out_shape=[jax.ShapeDtypeStruct((t, qw), jnp.bfloat16),
                   jax.ShapeDtypeStruct((t, 2 * kw), jnp.bfloat16)],
        compiler_params=_cparams(("arbitrary",)),
        name="odd_in_proj",
    )(x, gain.reshape(1, D_MODEL), w_all, cos, sin, qn_row, kn_row, ones_bd)


def _route(hn, wr_ref, br_ref, carry_ref, idx_ref, gate_ref, cnt_ref):
    tm = hn.shape[0]
    h_hi = hn.astype(jnp.bfloat16)
    h_lo = (hn - h_hi.astype(jnp.float32)).astype(jnp.bfloat16)
    hi_pass = _dot(h_hi, wr_ref[...])
    logits = (hi_pass[:, :ROUTER_COLS] + _dot(h_lo, wr_ref[:, :ROUTER_COLS])
              + hi_pass[:, ROUTER_COLS:] + br_ref[...])
    lt = logits.T
    rowf = lax.broadcasted_iota(jnp.int32, (SUBLANES, tm), 0).astype(jnp.float32)

    def first_argmax(x):
        m = jnp.max(x, axis=0, keepdims=True)
        idx = jnp.min(jnp.where(x == m, rowf, float(SUBLANES)), axis=0, keepdims=True)
        return m, idx

    lg = jnp.where(rowf < N_GROUPS, lt[0:SUBLANES], NEG)
    mg, gi = first_argmax(lg)
    p_top = 1.0 / jnp.sum(jnp.exp(lg - mg), axis=0, keepdims=True)
    le = lt[ROUTER_E0:ROUTER_E0 + EPG]
    for g in range(1, N_GROUPS):
        le = jnp.where(gi == float(g), lt[ROUTER_E0 + g * EPG:ROUTER_E0 + (g + 1) * EPG], le)
    m1, i1 = first_argmax(le)
    m2, i2 = first_argmax(jnp.where(rowf == i1, NEG, le))
    e2 = jnp.exp(m2 - m1)
    gate1 = p_top / (1.0 + e2)
    gate2 = p_top * e2 / (1.0 + e2)
    eid1 = gi * EPG + i1
    eid2 = gi * EPG + i2

    erow = lax.broadcasted_iota(jnp.int32, (N_EXPERTS, tm), 0).astype(jnp.float32)
    oh1 = erow == eid1
    oh2 = erow == eid2
    member = jnp.where(oh1 | oh2, 1.0, 0.0)
    s_idx = lax.broadcasted_iota(jnp.int32, (tm, tm), 0)
    t_idx = lax.broadcasted_iota(jnp.int32, (tm, tm), 1)
    upper = jnp.where(s_idx < t_idx, 1.0, 0.0).astype(jnp.bfloat16)
    before = carry_ref[:, 0:1] + _dot(member.astype(jnp.bfloat16), upper)
    rank1 = jnp.sum(jnp.where(oh1, before, 0.0), axis=0, keepdims=True)
    rank2 = jnp.sum(jnp.where(oh2, before, 0.0), axis=0, keepdims=True)
    carry_ref[...] = carry_ref[...] + jnp.sum(member, axis=1, keepdims=True)
    cnt_ref[...] = carry_ref[...].astype(jnp.int32)

    idx = jnp.where(rowf == 0.0, eid1, jnp.where(rowf == 1.0, eid2,
          jnp.where(rowf == 2.0, rank1, jnp.where(rowf == 3.0, rank2, 0.0))))
    idx_ref[...] = idx.astype(jnp.int32)
    gate_ref[...] = jnp.where(rowf == 0.0, gate1, jnp.where(rowf == 1.0, gate2, 0.0))


def _store_row_tiles(ref, x):
    n = x.shape[0]
    for c in range(ROW_CHUNKS):
        ref[pl.ds(c, n, stride=ROW_CHUNKS), :] = x[:, c * LANES:(c + 1) * LANES].astype(ref.dtype)


def _load_row_tiles(ref, n):
    return jnp.concatenate([ref[pl.ds(c, n, stride=ROW_CHUNKS), :] for c in range(ROW_CHUNKS)],
                           axis=1)


def _post_proj(y, x_ref, gf_ref, wr_ref, br_ref, carry_ref,
               h_ref, hn_ref, idx_ref, gate_ref, cnt_ref):
    @pl.when(pl.program_id(0) == 0)
    def _():
        carry_ref[...] = jnp.zeros_like(carry_ref)

    h = x_ref[...] + y
    h_ref[...] = h
    hn = _rms(h, gf_ref[...])
    _store_row_tiles(hn_ref, hn)
    _route(hn, wr_ref, br_ref, carry_ref, idx_ref, gate_ref, cnt_ref)


def _token_major(ref, dil, scr):
    if dil == 1:
        return ref[...].astype(jnp.float32)
    sub = ref.shape[0]
    n = sub * dil
    chunks = DIL_GW // LANES
    for r in range(dil):
        for c in range(chunks):
            l0 = r * DIL_GW + c * LANES
            scr[pl.ds(c * n + r, sub, stride=dil), :] = ref[:, l0:l0 + LANES].astype(jnp.float32)
    return jnp.concatenate([scr[c * n:(c + 1) * n, :] for c in range(chunks)], axis=1)


def _even_out_kernel(a_ref, o0_ref, o1_ref, o2_ref, l0_ref, l1_ref, l2_ref, w_ref, x_ref, gf_ref,
                     wr_ref, br_ref, h_ref, hn_ref, idx_ref, gate_ref, cnt_ref,
                     carry_ref, so1, so2, sl1, sl2):
    dils = [d for _, d in DIL_GROUPS]
    o0, o1, o2 = (_token_major(r, d, s) for r, d, s in zip((o0_ref, o1_ref, o2_ref), dils,
                                                            (None, so1, so2)))
    l0, l1, l2 = (_token_major(r, d, s) for r, d, s in zip((l0_ref, l1_ref, l2_ref), dils,
                                                            (None, sl1, sl2)))
    m = jnp.maximum(jnp.maximum(l0, l1), l2)
    e0, e1, e2 = jnp.exp(l0 - m), jnp.exp(l1 - m), jnp.exp(l2 - m)
    b = (e0 * o0 + e1 * o1 + e2 * o2) / (e0 + e1 + e2)
    y = _dot(a_ref[...], w_ref[0:SGU_WIDTH, :]) + _dot(b.astype(jnp.bfloat16),
                                                       w_ref[SGU_WIDTH:, :])
    _post_proj(y, x_ref, gf_ref, wr_ref, br_ref, carry_ref,
               h_ref, hn_ref, idx_ref, gate_ref, cnt_ref)


def _odd_out_kernel(o_ref, w_ref, x_ref, gf_ref, wr_ref, br_ref,
                    h_ref, hn_ref, idx_ref, gate_ref, cnt_ref, carry_ref):
    y = _dot(o_ref[...], w_ref[...])
    _post_proj(y, x_ref, gf_ref, wr_ref, br_ref, carry_ref,
               h_ref, hn_ref, idx_ref, gate_ref, cnt_ref)


def _router_weights(w_rg, b_rg, w_re, b_re):
    d = w_rg.shape[0]
    w = jnp.zeros((d, ROUTER_COLS), jnp.float32)
    w = w.at[:, 0:N_GROUPS].set(w_rg)
    w = w.at[:, ROUTER_E0:ROUTER_E0 + N_EXPERTS].set(
        jnp.transpose(w_re, (1, 0, 2)).reshape(d, N_EXPERTS))
    b = jnp.zeros((1, ROUTER_COLS), jnp.float32)
    b = b.at[0, 0:N_GROUPS].set(b_rg)
    b = b.at[0, ROUTER_E0:ROUTER_E0 + N_EXPERTS].set(b_re.reshape(N_EXPERTS))
    w_hi = w.astype(jnp.bfloat16)
    w_lo = (w - w_hi.astype(jnp.float32)).astype(jnp.bfloat16)
    return jnp.concatenate([w_hi, w_lo], axis=1), b


def _out_proj(kernel, acts, w_out, x2, gain_ffn, router, extra_scratch=()):
    t = x2.shape[0]
    w_router, b = router
    row = lambda n: pl.BlockSpec((TM, n), lambda i: (i, 0))
    colblk = pl.BlockSpec((SUBLANES, TM), lambda i: (0, i))
    return pl.pallas_call(
        kernel,
        grid=(t // TM,),
        in_specs=[pl.BlockSpec((TM * a.shape[0] // t, a.shape[1]), lambda i: (i, 0)) for a in acts]
        + [_const_spec(w_out.shape), row(D_MODEL), _const_spec((1, D_MODEL)),
           _const_spec(w_router.shape), _const_spec(b.shape)],
        out_specs=[row(D_MODEL), pl.BlockSpec((TM * ROW_CHUNKS, LANES), lambda i: (i, 0)),
                   colblk, colblk,
                   _const_spec((N_EXPERTS, LANES))],
        out_shape=[jax.ShapeDtypeStruct((t, D_MODEL), jnp.float32),
                   jax.ShapeDtypeStruct((t * ROW_CHUNKS, LANES), ROW_DTYPE),
                   jax.ShapeDtypeStruct((SUBLANES, t), jnp.int32),
                   jax.ShapeDtypeStruct((SUBLANES, t), jnp.float32),
                   jax.ShapeDtypeStruct((N_EXPERTS, LANES), jnp.int32)],
        scratch_shapes=[pltpu.VMEM((N_EXPERTS, LANES), jnp.float32)] + list(extra_scratch),
        compiler_params=_cparams(("arbitrary",)),
        name=kernel.__name__.strip("_"),
    )(*acts, w_out.astype(jnp.bfloat16), x2, gain_ffn.reshape(1, D_MODEL), w_router, b)


ROWS_PER_TILE = TOP_K * TM
DISPATCH_TM = 1024
ISSUE_UNROLL = 8
FFN_SLOTS = 3


def _row_tile(ref, row):
    return ref.at[pl.ds(pl.multiple_of(row * ROW_CHUNKS, ROW_CHUNKS), ROW_CHUNKS), :]


def _wait_rows(copy, n):
    def body(_, c):
        copy.wait()
        return c

    lax.fori_loop(0, n, body, 0, unroll=16)


def _dispatch_kernel(pstart_ref, pend_ref, dest_ref, hn_ref, xs_ref, zero_ref, sem, zsem):
    blk = MOE_BM * ROW_CHUNKS

    def zero_block(row0):
        return pltpu.make_async_copy(
            zero_ref, xs_ref.at[pl.ds(pl.multiple_of(row0 * ROW_CHUNKS, blk), blk), :], zsem)

    @pl.when(pl.program_id(0) == 0)
    def _():
        zero_ref[...] = jnp.zeros_like(zero_ref)

        def start_last(c, _):
            @pl.when(pend_ref[c] > pstart_ref[c])
            def _():
                zero_block(pend_ref[c] - MOE_BM).start()
            return 0

        def wait_last(c, _):
            @pl.when(pend_ref[c] > pstart_ref[c])
            def _():
                zero_block(pend_ref[c] - MOE_BM).wait()
            return 0

        lax.fori_loop(0, N_EXPERTS, start_last, 0)
        lax.fori_loop(0, N_EXPERTS, wait_last, 0)
        first_unused = lax.div(pend_ref[N_EXPERTS - 1], MOE_BM)
        n_blocks = xs_ref.shape[0] // blk

        def start_tail(b, _):
            zero_block(b * MOE_BM).start()
            return 0

        def wait_tail(b, _):
            zero_block(b * MOE_BM).wait()
            return 0

        lax.fori_loop(first_unused, n_blocks, start_tail, 0)
        lax.fori_loop(first_unused, n_blocks, wait_tail, 0)

    def row_copy(src, dst):
        return pltpu.make_async_copy(_row_tile(hn_ref, src), _row_tile(xs_ref, dst), sem)

    def issue(b, c):
        for j in range(ISSUE_UNROLL):
            for k in range(TOP_K):
                dst = dest_ref[0, b * (ISSUE_UNROLL * TOP_K) + j * TOP_K + k]
                row_copy(b * ISSUE_UNROLL + j, dst).start(priority=k)
        return c

    lax.fori_loop(0, DISPATCH_TM // ISSUE_UNROLL, issue, 0)
    _wait_rows(row_copy(0, 0), TOP_K * DISPATCH_TM)


def _dispatch(hn, dest, pstart, pend, rows):
    t = hn.shape[0] // ROW_CHUNKS
    n = t // DISPATCH_TM
    return pl.pallas_call(
        _dispatch_kernel,
        grid_spec=pltpu.PrefetchScalarGridSpec(
            num_scalar_prefetch=2,
            grid=(n,),
            in_specs=[pl.BlockSpec((None, 1, TOP_K * DISPATCH_TM), lambda i, ps, pe: (i, 0, 0),
                                   memory_space=pltpu.SMEM),
                      pl.BlockSpec((DISPATCH_TM * ROW_CHUNKS, LANES), lambda i, ps, pe: (i, 0))],
            out_specs=pl.BlockSpec(memory_space=pl.ANY),
            scratch_shapes=[pltpu.VMEM((MOE_BM * ROW_CHUNKS, LANES), ROW_DTYPE),
                            pltpu.SemaphoreType.DMA(()), pltpu.SemaphoreType.DMA(())]),
        out_shape=jax.ShapeDtypeStruct((rows * ROW_CHUNKS, LANES), ROW_DTYPE),
        compiler_params=_cparams(("arbitrary",)),
        name="moe_dispatch",
    )(pstart, pend, dest.reshape(n, 1, TOP_K * DISPATCH_TM), hn)


def _expert_kernel(blk_e_ref, nused_ref, xs_ref, wg_ref, wu_ref, wd_ref, ys_ref,
                   wg_s, wu_s, wd_s, xbuf, ybuf, zbuf, in_sems, out_sems, zsem):
    i = pl.program_id(0)
    n_steps = pl.num_programs(0)
    nu = nused_ref[0]
    blk = MOE_BM * ROW_CHUNKS

    def block(ref, b):
        return ref.at[pl.ds(pl.multiple_of(b * blk, blk), blk), :]

    def load(b):
        s = lax.rem(b, FFN_SLOTS)
        return pltpu.make_async_copy(block(xs_ref, b), xbuf.at[s], in_sems.at[s])

    def store(b):
        s = lax.rem(b, FFN_SLOTS)
        return pltpu.make_async_copy(ybuf.at[s], block(ys_ref, b), out_sems.at[s])

    @pl.when(i == 0)
    def _():
        for b in range(FFN_SLOTS - 1):
            @pl.when(b < nu)
            def _():
                load(i + b).start()

    @pl.when(i + (FFN_SLOTS - 1) < nu)
    def _():
        load(i + (FFN_SLOTS - 1)).start()

    @pl.when(i < nu)
    def _():
        @pl.when((i == 0) | (blk_e_ref[i] != blk_e_ref[jnp.maximum(i - 1, 0)]))
        def _():
            wg_s[...] = wg_ref[...].astype(jnp.bfloat16)
            wu_s[...] = wu_ref[...].astype(jnp.bfloat16)
            wd_s[...] = wd_ref[...].astype(jnp.bfloat16)

        slot = lax.rem(i, FFN_SLOTS)
        load(i).wait()

        @pl.when(i >= FFN_SLOTS)
        def _():
            store(i - FFN_SLOTS).wait()

        x = _load_row_tiles(xbuf.at[slot], MOE_BM).astype(jnp.bfloat16)
        hid = jax.nn.silu(_dot(x, wg_s[...])) * _dot(x, wu_s[...])
        _store_row_tiles(ybuf.at[slot], _dot(hid.astype(jnp.bfloat16), wd_s[...]))
        store(i).start()

    @pl.when(i >= nu)
    def _():
        @pl.when(i == nu)
        def _():
            zbuf[...] = jnp.zeros_like(zbuf)

        fill = pltpu.make_async_copy(zbuf, block(ys_ref, i), zsem)
        fill.start()
        fill.wait()

    @pl.when(i == n_steps - 1)
    def _():
        for d in range(FFN_SLOTS):
            b = nu - 1 - d

            @pl.when(b >= 0)
            def _():
                store(b).wait()


def _expert_ffn(xs, blk_e, nused, layer, w_gate, w_up, w_down):
    rows = xs.shape[0] // ROW_CHUNKS
    blk = MOE_BM * ROW_CHUNKS
    wspec = lambda shape: pl.BlockSpec((None, None) + shape,
                                       lambda i, be, nu: (layer, be[i], 0, 0))
    ring = pltpu.VMEM((FFN_SLOTS, blk, LANES), ROW_DTYPE)
    return pl.pallas_call(
        _expert_kernel,
        grid_spec=pltpu.PrefetchScalarGridSpec(
            num_scalar_prefetch=2,
            grid=(rows // MOE_BM,),
            in_specs=[pl.BlockSpec(memory_space=pl.ANY),
                      wspec((D_MODEL, EXPERT_FF)), wspec((D_MODEL, EXPERT_FF)),
                      wspec((EXPERT_FF, D_MODEL))],
            out_specs=pl.BlockSpec(memory_space=pl.ANY),
            scratch_shapes=[pltpu.VMEM((D_MODEL, EXPERT_FF), jnp.bfloat16),
                            pltpu.VMEM((D_MODEL, EXPERT_FF), jnp.bfloat16),
                            pltpu.VMEM((EXPERT_FF, D_MODEL), jnp.bfloat16),
                            ring, ring, pltpu.VMEM((blk, LANES), ROW_DTYPE),
                            pltpu.SemaphoreType.DMA((FFN_SLOTS,)),
                            pltpu.SemaphoreType.DMA((FFN_SLOTS,)),
                            pltpu.SemaphoreType.DMA(())]),
        out_shape=jax.ShapeDtypeStruct((rows * ROW_CHUNKS, LANES), ROW_DTYPE),
        compiler_params=_cparams(("arbitrary",)),
        name="moe_expert_ffn",
    )(blk_e, nused, xs, w_gate, w_up, w_down)


def _gather_combine(dest_ref, dest_next_ref, h_ref, gate_ref, ys_ref, buf_ref, sems):
    i = pl.program_id(0)
    n = pl.num_programs(0)

    def row_copy(src, slot, k, t):
        return pltpu.make_async_copy(_row_tile(ys_ref, src), _row_tile(buf_ref.at[slot, k], t),
                                     sems.at[slot])

    def issue_tile(d_ref, slot):
        def issue(b, c):
            for j in range(ISSUE_UNROLL):
                for k in range(TOP_K):
                    src = d_ref[0, b * (ISSUE_UNROLL * TOP_K) + j * TOP_K + k]
                    row_copy(src, slot, k, b * ISSUE_UNROLL + j).start(priority=k)
            return c

        lax.fori_loop(0, TM // ISSUE_UNROLL, issue, 0)

    slot = lax.rem(i, 2)

    @pl.when(i == 0)
    def _():
        issue_tile(dest_ref, 0)

    @pl.when(i + 1 < n)
    def _():
        issue_tile(dest_next_ref, 1 - slot)

    _wait_rows(row_copy(0, slot, 0, 0), ROWS_PER_TILE)
    g = gate_ref[...]
    r1 = _load_row_tiles(buf_ref.at[slot, 0], TM)
    r2 = _load_row_tiles(buf_ref.at[slot, 1], TM)
    return h_ref[...] + (r1 * g[:, 0:1] + r2 * g[:, 1:2])


def _combine_kernel(dest_ref, dest_next_ref, h_ref, gate_ref, ys_ref, out_ref, buf_ref, sems):
    out_ref[...] = _gather_combine(dest_ref, dest_next_ref, h_ref, gate_ref, ys_ref, buf_ref, sems)


def _combine(h, gate_cols, ys, dest):
    t = h.shape[0]
    n = t // TM
    dest_tiles = dest.reshape(n, 1, ROWS_PER_TILE)
    dspec = lambda f: pl.BlockSpec((None, 1, ROWS_PER_TILE), f, memory_space=pltpu.SMEM)
    return pl.pallas_call(
        _combine_kernel,
        grid=(n,),
        in_specs=[dspec(lambda i: (i, 0, 0)), dspec(lambda i: (jnp.minimum(i + 1, n - 1), 0, 0)),
                  pl.BlockSpec((TM, D_MODEL), lambda i: (i, 0)),
                  pl.BlockSpec((TM, SUBLANES), lambda i: (i, 0)),
                  pl.BlockSpec(memory_space=pl.ANY)],
        out_specs=pl.BlockSpec((TM, D_MODEL), lambda i: (i, 0)),
        scratch_shapes=[pltpu.VMEM((2, TOP_K, TM * ROW_CHUNKS, LANES), ROW_DTYPE),
                        pltpu.SemaphoreType.DMA((2,))],
        out_shape=jax.ShapeDtypeStruct((t, D_MODEL), jnp.float32),
        compiler_params=_cparams(("arbitrary",)),
        name="moe_combine",
    )(dest_tiles, dest_tiles, h, gate_cols, ys)


def _moe(h, hn, idx, gates, counts, layer, w_gate, w_up, w_down):
    t = h.shape[0]
    rows = t * TOP_K + N_EXPERTS * MOE_BM
    nblk = rows // MOE_BM
    cnt = counts[:, 0]
    padded = (cnt + MOE_BM - 1) // MOE_BM * MOE_BM
    pend = jnp.cumsum(padded).astype(jnp.int32)
    pstart = pend - padded
    blk_row0 = jnp.arange(nblk, dtype=jnp.int32) * MOE_BM
    blk_e = jnp.minimum(jnp.sum(pend[None, :] <= blk_row0[:, None], axis=1),
                        N_EXPERTS - 1).astype(jnp.int32)
    nused = (pend[-1:] // MOE_BM).astype(jnp.int32)
    eid, rank = idx[0:TOP_K], idx[TOP_K:2 * TOP_K]
    seg_start = jnp.sum(jnp.where(eid[..., None] == jnp.arange(N_EXPERTS), pstart, 0), axis=-1)
    dest = (seg_start + rank).T.reshape(t * TOP_K)
    xs = _dispatch(hn, dest, pstart, pend, rows)
    ys = _expert_ffn(xs, blk_e, nused, layer, w_gate, w_up, w_down)
    return _combine(h, gates.T, ys, dest)


def kernel(x, positions, norm_mix, norm_ffn, w_in_even, w_out_even, sgu_norm, sgu_w, sgu_b,
           qn_dil, kn_dil, w_in_odd, w_out_odd, qn_swa, kn_swa, sinks,
           w_router_g, b_router_g, w_router_e, b_router_e, w_gate, w_up, w_down):
    batch, seq, d = x.shape
    t = batch * seq
    depth = norm_mix.shape[0]
    cos, sin = _rope_tables(positions)
    bd = 2 * LANES
    ones_bd = (jnp.arange(bd)[:, None] // HEAD_DIM == jnp.arange(bd)[None, :] // HEAD_DIM
               ).astype(jnp.bfloat16)
    h = x.reshape(t, d)
    for layer in range(depth):
        i = layer // 2
        router = _router_weights(w_router_g[layer], b_router_g[layer], w_router_e[layer],
                                 b_router_e[layer])
        if layer % 2 == 0:
            a, *qkv = _even_in(h, norm_mix[layer], w_in_even[i], cos, sin, sgu_norm[i],
                               sgu_w[i], sgu_b[i], qn_dil[i], kn_dil[i], ones_bd)
            outs, lses = [], []
            for g, (window, dil) in enumerate(DIL_GROUPS):
                o, lse = _dilated_group_attention(qkv[g], batch, seq, dil, window)
                outs.append(o)
                lses.append(lse)
            stage = [pltpu.VMEM((TM * DIL_GW // LANES, LANES), jnp.float32)] * 4
            res = _out_proj(_even_out_kernel, [a] + outs + lses, w_out_even[i], h,
                            norm_ffn[layer], router, stage)
        else:
            q, kv = _odd_in(h, norm_mix[layer], w_in_odd[i], cos, sin, qn_swa[i], kn_swa[i],
                            ones_bd)
            o = _swa_attention(q, kv, sinks[i], batch, seq)
            res = _out_proj(_odd_out_kernel, [o], w_out_odd[i], h, norm_ffn[layer], router)
        h_mid, hn, idx, gates, counts = res
        h = _moe(h_mid, hn, idx, gates, counts, layer, w_gate, w_up, w_down)
    return h.reshape(batch, seq, d)
```

```python
import functools

import jax
import jax.numpy as jnp
from jax import lax
from jax.experimental import pallas as pl
from jax.experimental.pallas import tpu as pltpu

D_MODEL = 1024
HEAD_DIM = 64
BLK = 128
ROPE_THETA = 500000.0
ROT_DIM = HEAD_DIM // 4
HALF_ROT = ROT_DIM // 2
EPS = 1e-6
SGU_GROUPS = 8
SGU_WIDTH = SGU_GROUPS * HEAD_DIM
DIL_GROUPS = ((128, 1), (512, 4), (2048, 16))
DIL_HPG = 4
DIL_GW = DIL_HPG * HEAD_DIM
DIL_WIDTH = DIL_GW * len(DIL_GROUPS)
SWA_Q_HEADS = 16
SWA_KV_HEADS = 2
SWA_WINDOW = 128
N_GROUPS = 4
EPG = 8
N_EXPERTS = N_GROUPS * EPG
EXPERT_FF = 256
TOP_K = 2

LANES = 128
SUBLANES = 8
VMEM_LIMIT_BYTES = 48 * 1024 * 1024

TM = 512
TQ = 512
MOE_BM = 256
NEG = -1e30

ROW_CHUNKS = D_MODEL // LANES
ROW_DTYPE = jnp.float32
ROUTER_COLS = LANES
ROUTER_E0 = SUBLANES


def _cparams(sem):
    return pltpu.CompilerParams(dimension_semantics=sem, vmem_limit_bytes=VMEM_LIMIT_BYTES)


def _rms(x, gain_row):
    return x * lax.rsqrt(jnp.mean(x * x, axis=-1, keepdims=True) + EPS) * gain_row


def _dot(a, b):
    return jnp.dot(a, b, preferred_element_type=jnp.float32)


def _dot_nt(a, b):
    return lax.dot_general(a, b, (((1,), (1,)), ((), ())), preferred_element_type=jnp.float32)


def _rope_kernel(pos_ref, inv_ref, cos_ref, sin_ref):
    ang = inv_ref[...] * pos_ref[...].astype(jnp.float32)
    c = jnp.cos(ang)
    s = jnp.sin(ang)
    rest = HEAD_DIM // SUBLANES - 2
    head_c = [c, c] + [jnp.ones_like(c)] * rest
    head_s = [-s, s] + [jnp.zeros_like(s)] * rest
    cos_ref[...] = jnp.concatenate(head_c * 2, axis=0).T
    sin_ref[...] = jnp.concatenate(head_s * 2, axis=0).T


def _rope_tables(positions):
    t = positions.size
    inv = ROPE_THETA ** (-jnp.arange(0, ROT_DIM, 2, dtype=jnp.float32) / ROT_DIM)
    tm = 512
    return pl.pallas_call(
        _rope_kernel,
        grid=(t // tm,),
        in_specs=[pl.BlockSpec((1, tm), lambda i: (0, i)),
                  pl.BlockSpec((HALF_ROT, 1), lambda i: (0, 0))],
        out_specs=[pl.BlockSpec((tm, LANES), lambda i: (i, 0)),
                   pl.BlockSpec((tm, LANES), lambda i: (i, 0))],
        out_shape=[jax.ShapeDtypeStruct((t, LANES), jnp.float32)] * 2,
        compiler_params=_cparams(("arbitrary",)),
        name="rope_tables",
    )(positions.reshape(1, t), inv.reshape(HALF_ROT, 1))


def _head_norm_rope(x, ones_bd, gain_row, cos, sin):
    tm, w = x.shape
    sq = (x * x).astype(jnp.bfloat16)
    outs = []
    bd = ones_bd.shape[0]
    for c in range(w // bd):
        sl = slice(c * bd, (c + 1) * bd)
        ss = _dot(sq[:, sl], ones_bd)
        xn = x[:, sl] * lax.rsqrt(ss * (1.0 / HEAD_DIM) + EPS) * gain_row[:, sl]
        for j in range(bd // LANES):
            xj = xn[:, j * LANES:(j + 1) * LANES]
            lane = lax.broadcasted_iota(jnp.int32, xj.shape, 1) % HEAD_DIM
            rot = jnp.where(lane < HALF_ROT,
                            pltpu.roll(xj, LANES - HALF_ROT, axis=1),
                            pltpu.roll(xj, HALF_ROT, axis=1))
            outs.append(xj * cos + rot * sin)
    return jnp.concatenate(outs, axis=1)


def _residue_major(ref, n, w, dil):
    sub = n // dil
    return jnp.concatenate(
        [jnp.concatenate([ref[pl.ds(c * n + r, sub, stride=dil), :] for c in range(w // LANES)],
                         axis=1) for r in range(dil)], axis=0)


def _even_in_kernel(x_ref, g_ref, w_ref, cos_ref, sin_ref, sgn_ref, sgw_ref, sgb_ref,
                    qn_ref, kn_ref, bd_ref, a_ref, q0_ref, q1_ref, q2_ref, hn_scr):
    x = x_ref[...]
    tm = x.shape[0]
    hn32 = _rms(x, g_ref[...])
    for c in range(ROW_CHUNKS):
        hn_scr[c * tm:(c + 1) * tm, :] = hn32[:, c * LANES:(c + 1) * LANES]
    hn = hn32.astype(jnp.bfloat16)

    u = _dot(hn, w_ref[:, 0:SGU_WIDTH])
    v = _dot(hn, w_ref[:, SGU_WIDTH:2 * SGU_WIDTH])
    gu = jax.nn.gelu(u)
    vn = _rms(jax.nn.gelu(v), sgn_ref[...]).astype(jnp.bfloat16)

    row = lax.broadcasted_iota(jnp.int32, (BLK, BLK), 0)
    col = lax.broadcasted_iota(jnp.int32, (BLK, BLK), 1)
    tril = row >= col
    lane_lo = lax.broadcasted_iota(jnp.int32, (BLK, LANES), 1) < HEAD_DIM
    wts = [jnp.where(tril, sgw_ref[g], 0.0).astype(jnp.bfloat16) for g in range(SGU_GROUPS)]
    for c in range(tm // BLK):
        rows = slice(c * BLK, (c + 1) * BLK)
        for p in range(SGU_GROUPS // 2):
            lanes = slice(p * LANES, (p + 1) * LANES)
            vp = vn[rows, lanes]
            s = jnp.where(lane_lo, _dot(wts[2 * p], vp), _dot(wts[2 * p + 1], vp))
            a_ref[rows, lanes] = (gu[rows, lanes] * (s + sgb_ref[:, lanes])).astype(a_ref.dtype)

    o0 = 2 * SGU_WIDTH
    for g, ((_, dil), o_ref) in enumerate(zip(DIL_GROUPS, (q0_ref, q1_ref, q2_ref))):
        if dil == 1:
            hg, cos, sin = hn, cos_ref[...], sin_ref[...]
        else:
            hg = _residue_major(hn_scr, tm, D_MODEL, dil).astype(jnp.bfloat16)
            cos = _residue_major(cos_ref, tm, LANES, dil)
            sin = _residue_major(sin_ref, tm, LANES, dil)
        gl = slice(g * DIL_GW, (g + 1) * DIL_GW)
        parts = []
        for p, gain_ref in enumerate((qn_ref, kn_ref, None)):
            c0 = o0 + p * DIL_WIDTH + g * DIL_GW
            y = _dot(hg, w_ref[:, c0:c0 + DIL_GW])
            if gain_ref is not None:
                y = _head_norm_rope(y, bd_ref[...], gain_ref[:, gl], cos, sin)
            parts.append(y.astype(o_ref.dtype))
        sub = tm // dil
        for r in range(dil):
            for p, y in enumerate(parts):
                c0 = (r * 3 + p) * DIL_GW
                o_ref[:, c0:c0 + DIL_GW] = y[r * sub:(r + 1) * sub, :]


def _const_spec(shape):
    nd = len(shape)
    return pl.BlockSpec(shape, lambda i: (0,) * nd)


def _even_in(x2, gain, w_in, cos, sin, sgu_norm, sgu_w, sgu_b, qn, kn, ones_bd):
    t = x2.shape[0]
    in_w = w_in.shape[1]
    sgb = jnp.repeat(sgu_b.T, HEAD_DIM, axis=1)
    qn_row = jnp.tile(qn * HEAD_DIM ** -0.5, DIL_WIDTH // HEAD_DIM).reshape(1, DIL_WIDTH)
    kn_row = jnp.tile(kn, DIL_WIDTH // HEAD_DIM).reshape(1, DIL_WIDTH)
    row = lambda n: pl.BlockSpec((TM, n), lambda i: (i, 0))
    return pl.pallas_call(
        _even_in_kernel,
        grid=(t // TM,),
        in_specs=[row(D_MODEL), _const_spec((1, D_MODEL)), _const_spec((D_MODEL, in_w)),
                  row(LANES), row(LANES), _const_spec((1, SGU_WIDTH)),
                  _const_spec((SGU_GROUPS, BLK, BLK)), _const_spec((BLK, SGU_WIDTH)),
                  _const_spec((1, DIL_WIDTH)), _const_spec((1, DIL_WIDTH)),
                  _const_spec(ones_bd.shape)],
        out_specs=[row(SGU_WIDTH)] + [pl.BlockSpec((TM // d, d * 3 * DIL_GW), lambda i: (i, 0))
                                      for _, d in DIL_GROUPS],
        out_shape=[jax.ShapeDtypeStruct((t, SGU_WIDTH), jnp.bfloat16)]
        + [jax.ShapeDtypeStruct((t // d, d * 3 * DIL_GW), jnp.bfloat16) for _, d in DIL_GROUPS],
        scratch_shapes=[pltpu.VMEM((ROW_CHUNKS * TM, LANES), jnp.float32)],
        compiler_params=_cparams(("arbitrary",)),
        name="even_in_proj",
    )(x2, gain.reshape(1, D_MODEL), w_in.astype(jnp.bfloat16), cos, sin,
      sgu_norm.reshape(1, SGU_WIDTH), sgu_w, sgb, qn_row, kn_row, ones_bd)


def _band_mask(max_rel, first):
    i = lax.broadcasted_iota(jnp.int32, (BLK, 2 * BLK), 0)
    j = lax.broadcasted_iota(jnp.int32, (BLK, 2 * BLK), 1)
    rel = BLK + i - j
    first_key = jnp.where(first, BLK, 0)
    return (rel >= 0) & (rel <= max_rel) & (j >= first_key)


def _head_attention(qh, kk, vv, mask, sink):
    s = jnp.where(mask, _dot_nt(qh, kk), NEG)
    m = jnp.max(s, axis=-1, keepdims=True)
    if sink is not None:
        m = jnp.maximum(m, sink)
    p = jnp.exp(s - m)
    denom = jnp.sum(p, axis=-1, keepdims=True)
    if sink is not None:
        denom = denom + jnp.exp(sink - m)
    o = _dot(p.astype(jnp.bfloat16), vv)
    return o / denom, m, denom


def _dil_attn_kernel(q_ref, kc_ref, kp_ref, vc_ref, vp_ref, o_ref, lse_ref, *, max_rel):
    first_tile = pl.program_id(2) == 0
    lane_lo = lax.broadcasted_iota(jnp.int32, (BLK, LANES), 1) < HEAD_DIM
    zero = jnp.zeros((), jnp.bfloat16)
    for j in range(TQ // BLK):
        rows = slice(j * BLK, (j + 1) * BLK)
        if j == 0:
            kprev, vprev = kp_ref[...], vp_ref[...]
            mask = _band_mask(max_rel, first_tile)
        else:
            prows = slice((j - 1) * BLK, j * BLK)
            kprev, vprev = kc_ref[prows, :], vc_ref[prows, :]
            mask = _band_mask(max_rel, False)
        kk = jnp.concatenate([kprev, kc_ref[rows, :]], axis=0)
        vv = jnp.concatenate([vprev, vc_ref[rows, :]], axis=0)
        for p in range(DIL_GW // LANES):
            lanes = slice(p * LANES, (p + 1) * LANES)
            qp = q_ref[rows, lanes]
            oa, ma, da = _head_attention(jnp.where(lane_lo, qp, zero), kk[:, lanes],
                                         vv[:, lanes], mask, None)
            ob, mb, db = _head_attention(jnp.where(lane_lo, zero, qp), kk[:, lanes],
                                         vv[:, lanes], mask, None)
            o_ref[rows, lanes] = jnp.where(lane_lo, oa, ob).astype(o_ref.dtype)
            lse_ref[rows, lanes] = jnp.where(lane_lo, ma + jnp.log(da), mb + jnp.log(db))


def _dilated_group_attention(qkv, batch, seq, dil, window):
    sub = seq // dil
    a = qkv.reshape(batch, sub, dil * 3 * DIL_GW)
    nq = TQ // BLK
    cur = lambda part: pl.BlockSpec((None, TQ, DIL_GW), lambda b, r, i: (b, i, r * 3 + part))
    prev = lambda part: pl.BlockSpec(
        (None, BLK, DIL_GW), lambda b, r, i: (b, jnp.maximum(i * nq - 1, 0), r * 3 + part))
    out = pl.BlockSpec((None, TQ, DIL_GW), lambda b, r, i: (b, i, r))
    o, lse = pl.pallas_call(
        functools.partial(_dil_attn_kernel, max_rel=window // dil),
        grid=(batch, dil, sub // TQ),
        in_specs=[cur(0), cur(1), prev(1), cur(2), prev(2)],
        out_specs=[out, out],
        out_shape=[jax.ShapeDtypeStruct((batch, sub, dil * DIL_GW), jnp.bfloat16),
                   jax.ShapeDtypeStruct((batch, sub, dil * DIL_GW), jnp.float32)],
        compiler_params=_cparams(("arbitrary",) * 3),
        name=f"dilated_attn_d{dil}",
    )(a, a, a, a, a)
    rows = batch * sub
    return o.reshape(rows, dil * DIL_GW), lse.reshape(rows, dil * DIL_GW)


def _swa_kernel(sink_ref, q_ref, kvc_ref, kvp_ref, o_ref):
    first_tile = pl.program_id(1) == 0
    lane_lo = lax.broadcasted_iota(jnp.int32, (BLK, LANES), 1) < HEAD_DIM
    zero = jnp.zeros((), jnp.bfloat16)
    rep = SWA_Q_HEADS // SWA_KV_HEADS
    kw = SWA_KV_HEADS * LANES
    for j in range(TQ // BLK):
        rows = slice(j * BLK, (j + 1) * BLK)
        if j == 0:
            kvprev = kvp_ref[...]
            mask = _band_mask(SWA_WINDOW - 1, first_tile)
        else:
            kvprev = kvc_ref[(j - 1) * BLK:j * BLK, :]
            mask = _band_mask(SWA_WINDOW - 1, False)
        kv = jnp.concatenate([kvprev, kvc_ref[rows, :]], axis=0)
        for g in range(SWA_KV_HEADS):
            kk = kv[:, g * LANES:(g + 1) * LANES]
            vv = kv[:, kw + g * LANES:kw + (g + 1) * LANES]
            for p in range(rep // 2):
                h0 = g * rep + 2 * p
                lanes = slice(h0 * HEAD_DIM, (h0 + 2) * HEAD_DIM)
                qp = q_ref[rows, lanes]
                oa, _, _ = _head_attention(jnp.where(lane_lo, qp, zero), kk, vv, mask,
                                           sink_ref[h0])
                ob, _, _ = _head_attention(jnp.where(lane_lo, zero, qp), kk, vv, mask,
                                           sink_ref[h0 + 1])
                o_ref[rows, lanes] = jnp.where(lane_lo, oa, ob).astype(o_ref.dtype)


def _swa_attention(q, kv, sinks, batch, seq):
    q3 = q.reshape(batch, seq, q.shape[1])
    kv3 = kv.reshape(batch, seq, kv.shape[1])
    nq = TQ // BLK
    o = pl.pallas_call(
        _swa_kernel,
        grid_spec=pltpu.PrefetchScalarGridSpec(
            num_scalar_prefetch=1,
            grid=(batch, seq // TQ),
            in_specs=[pl.BlockSpec((None, TQ, q.shape[1]), lambda b, i, s: (b, i, 0)),
                      pl.BlockSpec((None, TQ, kv.shape[1]), lambda b, i, s: (b, i, 0)),
                      pl.BlockSpec((None, BLK, kv.shape[1]),
                                   lambda b, i, s: (b, jnp.maximum(i * nq - 1, 0), 0))],
            out_specs=pl.BlockSpec((None, TQ, q.shape[1]), lambda b, i, s: (b, i, 0))),
        out_shape=jax.ShapeDtypeStruct(q3.shape, jnp.bfloat16),
        compiler_params=_cparams(("arbitrary",) * 2),
        name="swa_attn",
    )(sinks.astype(jnp.float32), q3, kv3, kv3)
    return o.reshape(batch * seq, q.shape[1])


def _odd_in_kernel(x_ref, g_ref, w_ref, cos_ref, sin_ref, qn_ref, kn_ref, bd_ref, q_ref, kv_ref):
    hn = _rms(x_ref[...], g_ref[...]).astype(jnp.bfloat16)
    qw = SWA_Q_HEADS * HEAD_DIM
    kw = SWA_KV_HEADS * LANES
    cos = cos_ref[...]
    sin = sin_ref[...]
    q = _dot(hn, w_ref[:, 0:qw])
    q_ref[...] = _head_norm_rope(q, bd_ref[...], qn_ref[...], cos, sin).astype(q_ref.dtype)
    k = _dot(hn, w_ref[:, qw:qw + kw])
    kv_ref[:, 0:kw] = _head_norm_rope(k, bd_ref[...], kn_ref[...], cos, sin).astype(kv_ref.dtype)
    kv_ref[:, kw:2 * kw] = _dot(hn, w_ref[:, qw + kw:qw + 2 * kw]).astype(kv_ref.dtype)


def _odd_in(x, gain, w_in, cos, sin, qn, kn, ones_bd):
    t = x.shape[0]
    qw = SWA_Q_HEADS * HEAD_DIM
    kvw = SWA_KV_HEADS * HEAD_DIM
    dup = lambda w: jnp.concatenate(
        [w[:, h * HEAD_DIM:(h + 1) * HEAD_DIM] for h in range(SWA_KV_HEADS) for _ in range(2)],
        axis=1)
    w_all = jnp.concatenate([w_in[:, :qw], dup(w_in[:, qw:qw + kvw]),
                             dup(w_in[:, qw + kvw:qw + 2 * kvw])], axis=1).astype(jnp.bfloat16)
    kw = SWA_KV_HEADS * LANES
    qn_row = jnp.tile(qn * HEAD_DIM ** -0.5, qw // HEAD_DIM).reshape(1, qw)
    kn_row = jnp.tile(kn, kw // HEAD_DIM).reshape(1, kw)
    row = lambda n: pl.BlockSpec((TM, n), lambda i: (i, 0))
    return pl.pallas_call(
        _odd_in_kernel,
        grid=(t // TM,),
        in_specs=[row(D_MODEL), _const_spec((1, D_MODEL)), _const_spec(w_all.shape),
                  row(LANES), row(LANES), _const_spec((1, qw)), _const_spec((1, kw)),
                  _const_spec(ones_bd.shape)],
        out_specs=[row(qw), row(2 * kw)],
        out_shape=[jax.ShapeDtypeStruct((t, qw), jnp.bfloat16),
                   jax.ShapeDtypeStruct((t, 2 * kw), jnp.bfloat16)],
        compiler_params=_cparams(("arbitrary",)),
        name="odd_in_proj",
    )(x, gain.reshape(1, D_MODEL), w_all, cos, sin, qn_row, kn_row, ones_bd)


def _route(hn, wr_ref, br_ref, carry_ref, idx_ref, gate_ref, cnt_ref):
    tm = hn.shape[0]
    h_hi = hn.astype(jnp.bfloat16)
    h_lo = (hn - h_hi.astype(jnp.float32)).astype(jnp.bfloat16)
    hi_pass = _dot(h_hi, wr_ref[...])
    logits = (hi_pass[:, :ROUTER_COLS] + _dot(h_lo, wr_ref[:, :ROUTER_COLS])
              + hi_pass[:, ROUTER_COLS:] + br_ref[...])
    lt = logits.T
    rowf = lax.broadcasted_iota(jnp.int32, (SUBLANES, tm), 0).astype(jnp.float32)

    def first_argmax(x):
        m = jnp.max(x, axis=0, keepdims=True)
        idx = jnp.min(jnp.where(x == m, rowf, float(SUBLANES)), axis=0, keepdims=True)
        return m, idx

    lg = jnp.where(rowf < N_GROUPS, lt[0:SUBLANES], NEG)
    mg, gi = first_argmax(lg)
    p_top = 1.0 / jnp.sum(jnp.exp(lg - mg), axis=0, keepdims=True)
    le = lt[ROUTER_E0:ROUTER_E0 + EPG]
    for g in range(1, N_GROUPS):
        le = jnp.where(gi == float(g), lt[ROUTER_E0 + g * EPG:ROUTER_E0 + (g + 1) * EPG], le)
    m1, i1 = first_argmax(le)
    m2, i2 = first_argmax(jnp.where(rowf == i1, NEG, le))
    e2 = jnp.exp(m2 - m1)
    gate1 = p_top / (1.0 + e2)
    gate2 = p_top * e2 / (1.0 + e2)
    eid1 = gi * EPG + i1
    eid2 = gi * EPG + i2

    erow = lax.broadcasted_iota(jnp.int32, (N_EXPERTS, tm), 0).astype(jnp.float32)
    oh1 = erow == eid1
    oh2 = erow == eid2
    member = jnp.where(oh1 | oh2, 1.0, 0.0)
    s_idx = lax.broadcasted_iota(jnp.int32, (tm, tm), 0)
    t_idx = lax.broadcasted_iota(jnp.int32, (tm, tm), 1)
    upper = jnp.where(s_idx < t_idx, 1.0, 0.0).astype(jnp.bfloat16)
    before = carry_ref[:, 0:1] + _dot(member.astype(jnp.bfloat16), upper)
    rank1 = jnp.sum(jnp.where(oh1, before, 0.0), axis=0, keepdims=True)
    rank2 = jnp.sum(jnp.where(oh2, before, 0.0), axis=0, keepdims=True)
    carry_ref[...] = carry_ref[...] + jnp.sum(member, axis=1, keepdims=True)
    cnt_ref[...] = carry_ref[...].astype(jnp.int32)

    idx = jnp.where(rowf == 0.0, eid1, jnp.where(rowf == 1.0, eid2,
          jnp.where(rowf == 2.0, rank1, jnp.where(rowf == 3.0, rank2, 0.0))))
    idx_ref[...] = idx.astype(jnp.int32)
    gates = jnp.where(rowf == 0.0, gate1, jnp.where(rowf == 1.0, gate2, 0.0))
    padded = jnp.concatenate([gates] + [jnp.zeros_like(gates)] * (LANES // SUBLANES - 1), axis=0)
    gate_ref[...] = padded.T[:, :SUBLANES]


def _store_row_tiles(ref, x):
    n = x.shape[0]
    for c in range(ROW_CHUNKS):
        ref[pl.ds(c, n, stride=ROW_CHUNKS), :] = x[:, c * LANES:(c + 1) * LANES].astype(ref.dtype)


def _load_row_tiles(ref, n):
    return jnp.concatenate([ref[pl.ds(c, n, stride=ROW_CHUNKS), :] for c in range(ROW_CHUNKS)],
                           axis=1)


def _post_proj(y, x_ref, gf_ref, wr_ref, br_ref, carry_ref,
               h_ref, hn_ref, idx_ref, gate_ref, cnt_ref):
    @pl.when(pl.program_id(0) == 0)
    def _():
        carry_ref[...] = jnp.zeros_like(carry_ref)

    h = x_ref[...] + y
    h_ref[...] = h
    hn = _rms(h, gf_ref[...])
    _store_row_tiles(hn_ref, hn)
    _route(hn, wr_ref, br_ref, carry_ref, idx_ref, gate_ref, cnt_ref)


def _token_major(ref, dil, scr):
    if dil == 1:
        return ref[...].astype(jnp.float32)
    sub = ref.shape[0]
    n = sub * dil
    chunks = DIL_GW // LANES
    for r in range(dil):
        for c in range(chunks):
            l0 = r * DIL_GW + c * LANES
            scr[pl.ds(c * n + r, sub, stride=dil), :] = ref[:, l0:l0 + LANES].astype(jnp.float32)
    return jnp.concatenate([scr[c * n:(c + 1) * n, :] for c in range(chunks)], axis=1)


def _even_out_kernel(a_ref, o0_ref, o1_ref, o2_ref, l0_ref, l1_ref, l2_ref, w_ref, x_ref, gf_ref,
                     wr_ref, br_ref, h_ref, hn_ref, idx_ref, gate_ref, cnt_ref,
                     carry_ref, so1, so2, sl1, sl2):
    dils = [d for _, d in DIL_GROUPS]
    o0, o1, o2 = (_token_major(r, d, s) for r, d, s in zip((o0_ref, o1_ref, o2_ref), dils,
                                                            (None, so1, so2)))
    l0, l1, l2 = (_token_major(r, d, s) for r, d, s in zip((l0_ref, l1_ref, l2_ref), dils,
                                                            (None, sl1, sl2)))
    m = jnp.maximum(jnp.maximum(l0, l1), l2)
    e0, e1, e2 = jnp.exp(l0 - m), jnp.exp(l1 - m), jnp.exp(l2 - m)
    b = (e0 * o0 + e1 * o1 + e2 * o2) / (e0 + e1 + e2)
    y = _dot(a_ref[...], w_ref[0:SGU_WIDTH, :]) + _dot(b.astype(jnp.bfloat16),
                                                       w_ref[SGU_WIDTH:, :])
    _post_proj(y, x_ref, gf_ref, wr_ref, br_ref, carry_ref,
               h_ref, hn_ref, idx_ref, gate_ref, cnt_ref)


def _odd_out_kernel(o_ref, w_ref, x_ref, gf_ref, wr_ref, br_ref,
                    h_ref, hn_ref, idx_ref, gate_ref, cnt_ref, carry_ref):
    y = _dot(o_ref[...], w_ref[...])
    _post_proj(y, x_ref, gf_ref, wr_ref, br_ref, carry_ref,
               h_ref, hn_ref, idx_ref, gate_ref, cnt_ref)


def _router_weights(w_rg, b_rg, w_re, b_re):
    d = w_rg.shape[0]
    w = jnp.zeros((d, ROUTER_COLS), jnp.float32)
    w = w.at[:, 0:N_GROUPS].set(w_rg)
    w = w.at[:, ROUTER_E0:ROUTER_E0 + N_EXPERTS].set(
        jnp.transpose(w_re, (1, 0, 2)).reshape(d, N_EXPERTS))
    b = jnp.zeros((1, ROUTER_COLS), jnp.float32)
    b = b.at[0, 0:N_GROUPS].set(b_rg)
    b = b.at[0, ROUTER_E0:ROUTER_E0 + N_EXPERTS].set(b_re.reshape(N_EXPERTS))
    w_hi = w.astype(jnp.bfloat16)
    w_lo = (w - w_hi.astype(jnp.float32)).astype(jnp.bfloat16)
    return jnp.concatenate([w_hi, w_lo], axis=1), b


def _out_proj(kernel, acts, w_out, x2, gain_ffn, router, extra_scratch=()):
    t = x2.shape[0]
    w_router, b = router
    row = lambda n: pl.BlockSpec((TM, n), lambda i: (i, 0))
    colblk = pl.BlockSpec((SUBLANES, TM), lambda i: (0, i))
    return pl.pallas_call(
        kernel,
        grid=(t // TM,),
        in_specs=[pl.BlockSpec((TM * a.shape[0] // t, a.shape[1]), lambda i: (i, 0)) for a in acts]
        + [_const_spec(w_out.shape), row(D_MODEL), _const_spec((1, D_MODEL)),
           _const_spec(w_router.shape), _const_spec(b.shape)],
        out_specs=[row(D_MODEL), pl.BlockSpec((TM * ROW_CHUNKS, LANES), lambda i: (i, 0)),
                   colblk, row(SUBLANES),
                   _const_spec((N_EXPERTS, LANES))],
        out_shape=[jax.ShapeDtypeStruct((t, D_MODEL), jnp.float32),
                   jax.ShapeDtypeStruct((t * ROW_CHUNKS, LANES), ROW_DTYPE),
                   jax.ShapeDtypeStruct((SUBLANES, t), jnp.int32),
                   jax.ShapeDtypeStruct((t, SUBLANES), jnp.float32),
                   jax.ShapeDtypeStruct((N_EXPERTS, LANES), jnp.int32)],
        scratch_shapes=[pltpu.VMEM((N_EXPERTS, LANES), jnp.float32)] + list(extra_scratch),
        compiler_params=_cparams(("arbitrary",)),
        name=kernel.__name__.strip("_"),
    )(*acts, w_out.astype(jnp.bfloat16), x2, gain_ffn.reshape(1, D_MODEL), w_router, b)


ROWS_PER_TILE = TOP_K * TM
DISPATCH_TM = 1024
ISSUE_UNROLL = 8
FFN_SLOTS = 3


def _row_tile(ref, row):
    return ref.at[pl.ds(pl.multiple_of(row * ROW_CHUNKS, ROW_CHUNKS), ROW_CHUNKS), :]


def _wait_rows(copy, n):
    def body(_, c):
        copy.wait()
        return c

    lax.fori_loop(0, n, body, 0, unroll=16)


def _dispatch_kernel(pstart_ref, pend_ref, dest_ref, hn_ref, xs_ref, zero_ref, sem, zsem):
    blk = MOE_BM * ROW_CHUNKS

    def zero_block(row0):
        return pltpu.make_async_copy(
            zero_ref, xs_ref.at[pl.ds(pl.multiple_of(row0 * ROW_CHUNKS, blk), blk), :], zsem)

    @pl.when(pl.program_id(0) == 0)
    def _():
        zero_ref[...] = jnp.zeros_like(zero_ref)

        def start_last(c, _):
            @pl.when(pend_ref[c] > pstart_ref[c])
            def _():
                zero_block(pend_ref[c] - MOE_BM).start()
            return 0

        def wait_last(c, _):
            @pl.when(pend_ref[c] > pstart_ref[c])
            def _():
                zero_block(pend_ref[c] - MOE_BM).wait()
            return 0

        lax.fori_loop(0, N_EXPERTS, start_last, 0)
        lax.fori_loop(0, N_EXPERTS, wait_last, 0)
        first_unused = lax.div(pend_ref[N_EXPERTS - 1], MOE_BM)
        n_blocks = xs_ref.shape[0] // blk

        def start_tail(b, _):
            zero_block(b * MOE_BM).start()
            return 0

        def wait_tail(b, _):
            zero_block(b * MOE_BM).wait()
            return 0

        lax.fori_loop(first_unused, n_blocks, start_tail, 0)
        lax.fori_loop(first_unused, n_blocks, wait_tail, 0)

    def row_copy(src, dst):
        return pltpu.make_async_copy(_row_tile(hn_ref, src), _row_tile(xs_ref, dst), sem)

    def issue(b, c):
        for j in range(ISSUE_UNROLL):
            for k in range(TOP_K):
                dst = dest_ref[0, b * (ISSUE_UNROLL * TOP_K) + j * TOP_K + k]
                row_copy(b * ISSUE_UNROLL + j, dst).start(priority=k)
        return c

    lax.fori_loop(0, DISPATCH_TM // ISSUE_UNROLL, issue, 0)
    _wait_rows(row_copy(0, 0), TOP_K * DISPATCH_TM)


def _dispatch(hn, dest, pstart, pend, rows):
    t = hn.shape[0] // ROW_CHUNKS
    n = t // DISPATCH_TM
    return pl.pallas_call(
        _dispatch_kernel,
        grid_spec=pltpu.PrefetchScalarGridSpec(
            num_scalar_prefetch=2,
            grid=(n,),
            in_specs=[pl.BlockSpec((None, 1, TOP_K * DISPATCH_TM), lambda i, ps, pe: (i, 0, 0),
                                   memory_space=pltpu.SMEM),
                      pl.BlockSpec((DISPATCH_TM * ROW_CHUNKS, LANES), lambda i, ps, pe: (i, 0))],
            out_specs=pl.BlockSpec(memory_space=pl.ANY),
            scratch_shapes=[pltpu.VMEM((MOE_BM * ROW_CHUNKS, LANES), ROW_DTYPE),
                            pltpu.SemaphoreType.DMA(()), pltpu.SemaphoreType.DMA(())]),
        out_shape=jax.ShapeDtypeStruct((rows * ROW_CHUNKS, LANES), ROW_DTYPE),
        compiler_params=_cparams(("arbitrary",)),
        name="moe_dispatch",
    )(pstart, pend, dest.reshape(n, 1, TOP_K * DISPATCH_TM), hn)


def _expert_kernel(blk_e_ref, wslot_ref, next_e_ref, nused_ref, xs_ref, wg_ref, wu_ref, wd_ref,
                   ys_ref, wg_s, wu_s, wd_s, wg_buf, wu_buf, wd_buf, xbuf, ybuf, zbuf,
                   w_sems, in_sems, out_sems, zsem, *, layer):
    i = pl.program_id(0)
    n_steps = pl.num_programs(0)
    nu = nused_ref[0]
    blk = MOE_BM * ROW_CHUNKS

    def weight_copies(e, s):
        return [pltpu.make_async_copy(w_ref.at[layer, e], buf.at[s], w_sems.at[s])
                for w_ref, buf in ((wg_ref, wg_buf), (wu_ref, wu_buf), (wd_ref, wd_buf))]

    def block(ref, b):
        return ref.at[pl.ds(pl.multiple_of(b * blk, blk), blk), :]

    def load(b):
        s = lax.rem(b, FFN_SLOTS)
        return pltpu.make_async_copy(block(xs_ref, b), xbuf.at[s], in_sems.at[s])

    def store(b):
        s = lax.rem(b, FFN_SLOTS)
        return pltpu.make_async_copy(ybuf.at[s], block(ys_ref, b), out_sems.at[s])

    @pl.when(i == 0)
    def _():
        for c in weight_copies(blk_e_ref[0], wslot_ref[0]):
            c.start()
        for b in range(FFN_SLOTS - 1):
            @pl.when(b < nu)
            def _():
                load(i + b).start()

    @pl.when(i + (FFN_SLOTS - 1) < nu)
    def _():
        load(i + (FFN_SLOTS - 1)).start()

    @pl.when(i < nu)
    def _():
        @pl.when((i == 0) | (blk_e_ref[i] != blk_e_ref[jnp.maximum(i - 1, 0)]))
        def _():
            ws = wslot_ref[i]
            for c in weight_copies(blk_e_ref[i], ws):
                c.wait()

            @pl.when(next_e_ref[i] >= 0)
            def _():
                for c in weight_copies(next_e_ref[i], 1 - ws):
                    c.start()

            wg_s[...] = wg_buf[ws].astype(jnp.bfloat16)
            wu_s[...] = wu_buf[ws].astype(jnp.bfloat16)
            wd_s[...] = wd_buf[ws].astype(jnp.bfloat16)

        slot = lax.rem(i, FFN_SLOTS)
        load(i).wait()

        @pl.when(i >= FFN_SLOTS)
        def _():
            store(i - FFN_SLOTS).wait()

        x = _load_row_tiles(xbuf.at[slot], MOE_BM).astype(jnp.bfloat16)
        hid = jax.nn.silu(_dot(x, wg_s[...])) * _dot(x, wu_s[...])
        _store_row_tiles(ybuf.at[slot], _dot(hid.astype(jnp.bfloat16), wd_s[...]))
        store(i).start()

    @pl.when(i >= nu)
    def _():
        @pl.when(i == nu)
        def _():
            zbuf[...] = jnp.zeros_like(zbuf)

        fill = pltpu.make_async_copy(zbuf, block(ys_ref, i), zsem)
        fill.start()
        fill.wait()

    @pl.when(i == n_steps - 1)
    def _():
        for d in range(FFN_SLOTS):
            b = nu - 1 - d

            @pl.when(b >= 0)
            def _():
                store(b).wait()


def _expert_ffn(xs, blk_e, wslot, next_e, nused, layer, w_gate, w_up, w_down):
    rows = xs.shape[0] // ROW_CHUNKS
    blk = MOE_BM * ROW_CHUNKS
    ring = pltpu.VMEM((FFN_SLOTS, blk, LANES), ROW_DTYPE)
    any_spec = pl.BlockSpec(memory_space=pl.ANY)
    return pl.pallas_call(
        functools.partial(_expert_kernel, layer=layer),
        grid_spec=pltpu.PrefetchScalarGridSpec(
            num_scalar_prefetch=4,
            grid=(rows // MOE_BM,),
            in_specs=[any_spec] * 4,
            out_specs=any_spec,
            scratch_shapes=[pltpu.VMEM((D_MODEL, EXPERT_FF), jnp.bfloat16),
                            pltpu.VMEM((D_MODEL, EXPERT_FF), jnp.bfloat16),
                            pltpu.VMEM((EXPERT_FF, D_MODEL), jnp.bfloat16),
                            pltpu.VMEM((2, D_MODEL, EXPERT_FF), jnp.float32),
                            pltpu.VMEM((2, D_MODEL, EXPERT_FF), jnp.float32),
                            pltpu.VMEM((2, EXPERT_FF, D_MODEL), jnp.float32),
                            ring, ring, pltpu.VMEM((blk, LANES), ROW_DTYPE),
                            pltpu.SemaphoreType.DMA((2,)),
                            pltpu.SemaphoreType.DMA((FFN_SLOTS,)),
                            pltpu.SemaphoreType.DMA((FFN_SLOTS,)),
                            pltpu.SemaphoreType.DMA(())]),
        out_shape=jax.ShapeDtypeStruct((rows * ROW_CHUNKS, LANES), ROW_DTYPE),
        compiler_params=_cparams(("arbitrary",)),
        name="moe_expert_ffn",
    )(blk_e, wslot, next_e, nused, xs, w_gate, w_up, w_down)


def _gather_combine(dest_ref, dest_next_ref, h_ref, gate_ref, ys_ref, buf_ref, sems):
    i = pl.program_id(0)
    n = pl.num_programs(0)

    def row_copy(src, slot, k, t):
        return pltpu.make_async_copy(_row_tile(ys_ref, src), _row_tile(buf_ref.at[slot, k], t),
                                     sems.at[slot])

    def issue_tile(d_ref, slot):
        def issue(b, c):
            for j in range(ISSUE_UNROLL):
                for k in range(TOP_K):
                    src = d_ref[0, b * (ISSUE_UNROLL * TOP_K) + j * TOP_K + k]
                    row_copy(src, slot, k, b * ISSUE_UNROLL + j).start(priority=k)
            return c

        lax.fori_loop(0, TM // ISSUE_UNROLL, issue, 0)

    slot = lax.rem(i, 2)

    @pl.when(i == 0)
    def _():
        issue_tile(dest_ref, 0)

    @pl.when(i + 1 < n)
    def _():
        issue_tile(dest_next_ref, 1 - slot)

    _wait_rows(row_copy(0, slot, 0, 0), ROWS_PER_TILE)
    g = gate_ref[...]
    r1 = _load_row_tiles(buf_ref.at[slot, 0], TM)
    r2 = _load_row_tiles(buf_ref.at[slot, 1], TM)
    return h_ref[...] + (r1 * g[:, 0:1] + r2 * g[:, 1:2])


def _combine_kernel(dest_ref, dest_next_ref, h_ref, gate_ref, ys_ref, out_ref, buf_ref, sems):
    out_ref[...] = _gather_combine(dest_ref, dest_next_ref, h_ref, gate_ref, ys_ref, buf_ref, sems)


def _combine(h, gate_cols, ys, dest):
    t = h.shape[0]
    n = t // TM
    dest_tiles = dest.reshape(n, 1, ROWS_PER_TILE)
    dspec = lambda f: pl.BlockSpec((None, 1, ROWS_PER_TILE), f, memory_space=pltpu.SMEM)
    return pl.pallas_call(
        _combine_kernel,
        grid=(n,),
        in_specs=[dspec(lambda i: (i, 0, 0)), dspec(lambda i: (jnp.minimum(i + 1, n - 1), 0, 0)),
                  pl.BlockSpec((TM, D_MODEL), lambda i: (i, 0)),
                  pl.BlockSpec((TM, SUBLANES), lambda i: (i, 0)),
                  pl.BlockSpec(memory_space=pl.ANY)],
        out_specs=pl.BlockSpec((TM, D_MODEL), lambda i: (i, 0)),
        scratch_shapes=[pltpu.VMEM((2, TOP_K, TM * ROW_CHUNKS, LANES), ROW_DTYPE),
                        pltpu.SemaphoreType.DMA((2,))],
        out_shape=jax.ShapeDtypeStruct((t, D_MODEL), jnp.float32),
        compiler_params=_cparams(("arbitrary",)),
        name="moe_combine",
    )(dest_tiles, dest_tiles, h, gate_cols, ys)


def _moe(h, hn, idx, gates, counts, layer, w_gate, w_up, w_down):
    t = h.shape[0]
    rows = t * TOP_K + N_EXPERTS * MOE_BM
    nblk = rows // MOE_BM
    cnt = counts[:, 0]
    padded = (cnt + MOE_BM - 1) // MOE_BM * MOE_BM
    pend = jnp.cumsum(padded).astype(jnp.int32)
    pstart = pend - padded
    blk_row0 = jnp.arange(nblk, dtype=jnp.int32) * MOE_BM
    blk_e = jnp.minimum(jnp.sum(pend[None, :] <= blk_row0[:, None], axis=1),
                        N_EXPERTS - 1).astype(jnp.int32)
    nused = (pend[-1:] // MOE_BM).astype(jnp.int32)
    experts = jnp.arange(N_EXPERTS, dtype=jnp.int32)
    nonempty = cnt > 0
    wslot_e = ((jnp.cumsum(nonempty) - nonempty) % 2).astype(jnp.int32)
    later = jnp.where(nonempty[None, :] & (experts[None, :] > experts[:, None]),
                      experts[None, :], N_EXPERTS)
    next_e_e = jnp.min(later, axis=1)
    next_e_e = jnp.where(next_e_e == N_EXPERTS, -1, next_e_e).astype(jnp.int32)
    wslot, next_e = wslot_e[blk_e], next_e_e[blk_e]
    eid, rank = idx[0:TOP_K], idx[TOP_K:2 * TOP_K]
    seg_start = jnp.sum(jnp.where(eid[..., None] == jnp.arange(N_EXPERTS), pstart, 0), axis=-1)
    dest = (seg_start + rank).T.reshape(t * TOP_K)
    xs = _dispatch(hn, dest, pstart, pend, rows)
    ys = _expert_ffn(xs, blk_e, wslot, next_e, nused, layer, w_gate, w_up, w_down)
    return _combine(h, gates, ys, dest)


def kernel(x, positions, norm_mix, norm_ffn, w_in_even, w_out_even, sgu_norm, sgu_w, sgu_b,
           qn_dil, kn_dil, w_in_odd, w_out_odd, qn_swa, kn_swa, sinks,
           w_router_g, b_router_g, w_router_e, b_router_e, w_gate, w_up, w_down):
    batch, seq, d = x.shape
    t = batch * seq
    depth = norm_mix.shape[0]
    cos, sin = _rope_tables(positions)
    bd = 2 * LANES
    ones_bd = (jnp.arange(bd)[:, None] // HEAD_DIM == jnp.arange(bd)[None, :] // HEAD_DIM
               ).astype(jnp.bfloat16)
    h = x.reshape(t, d)
    for layer in range(depth):
        i = layer // 2
        router = _router_weights(w_router_g[layer], b_router_g[layer], w_router_e[layer],
                                 b_router_e[layer])
        if layer % 2 == 0:
            a, *qkv = _even_in(h, norm_mix[layer], w_in_even[i], cos, sin, sgu_norm[i],
                               sgu_w[i], sgu_b[i], qn_dil[i], kn_dil[i], ones_bd)
            outs, lses = [], []
            for g, (window, dil) in enumerate(DIL_GROUPS):
                o, lse = _dilated_group_attention(qkv[g], batch, seq, dil, window)
                outs.append(o)
                lses.append(lse)
            stage = [pltpu.VMEM((TM * DIL_GW // LANES, LANES), jnp.float32)] * 4
            res = _out_proj(_even_out_kernel, [a] + outs + lses, w_out_even[i], h,
                            norm_ffn[layer], router, stage)
        else:
            q, kv = _odd_in(h, norm_mix[layer], w_in_odd[i], cos, sin, qn_swa[i], kn_swa[i],
                            ones_bd)
            o = _swa_attention(q, kv, sinks[i], batch, seq)
            res = _out_proj(_odd_out_kernel, [o], w_out_odd[i], h, norm_ffn[layer], router)
        h_mid, hn, idx, gates, counts = res
        h = _moe(h_mid, hn, idx, gates, counts, layer, w_gate, w_up, w_down)
    return h.reshape(batch, seq, d)
```

```python
import functools

import jax
import jax.numpy as jnp
from jax import lax
from jax.experimental import pallas as pl
from jax.experimental.pallas import tpu as pltpu

D_MODEL = 1024
HEAD_DIM = 64
BLK = 128
ROPE_THETA = 500000.0
ROT_DIM = HEAD_DIM // 4
HALF_ROT = ROT_DIM // 2
EPS = 1e-6
SGU_GROUPS = 8
SGU_WIDTH = SGU_GROUPS * HEAD_DIM
DIL_GROUPS = ((128, 1), (512, 4), (2048, 16))
DIL_HPG = 4
DIL_GW = DIL_HPG * HEAD_DIM
DIL_WIDTH = DIL_GW * len(DIL_GROUPS)
SWA_Q_HEADS = 16
SWA_KV_HEADS = 2
SWA_WINDOW = 128
N_GROUPS = 4
EPG = 8
N_EXPERTS = N_GROUPS * EPG
EXPERT_FF = 256
TOP_K = 2

LANES = 128
SUBLANES = 8
VMEM_LIMIT_BYTES = 48 * 1024 * 1024

TM = 512
TQ = 512
MOE_BM = 256
NEG = -1e30

ROW_CHUNKS = D_MODEL // LANES
ROW_DTYPE = jnp.float32
ROUTER_COLS = LANES
ROUTER_E0 = SUBLANES


def _cparams(sem):
    return pltpu.CompilerParams(dimension_semantics=sem, vmem_limit_bytes=VMEM_LIMIT_BYTES)


def _rms(x, gain_row):
    return x * lax.rsqrt(jnp.mean(x * x, axis=-1, keepdims=True) + EPS) * gain_row


def _dot(a, b):
    return jnp.dot(a, b, preferred_element_type=jnp.float32)


def _dot_nt(a, b):
    return lax.dot_general(a, b, (((1,), (1,)), ((), ())), preferred_element_type=jnp.float32)


def _rope_kernel(pos_ref, inv_ref, cos_ref, sin_ref):
    ang = inv_ref[...] * pos_ref[...].astype(jnp.float32)
    c = jnp.cos(ang)
    s = jnp.sin(ang)
    rest = HEAD_DIM // SUBLANES - 2
    head_c = [c, c] + [jnp.ones_like(c)] * rest
    head_s = [-s, s] + [jnp.zeros_like(s)] * rest
    cos_ref[...] = jnp.concatenate(head_c * 2, axis=0).T
    sin_ref[...] = jnp.concatenate(head_s * 2, axis=0).T


def _rope_tables(positions):
    t = positions.size
    inv = ROPE_THETA ** (-jnp.arange(0, ROT_DIM, 2, dtype=jnp.float32) / ROT_DIM)
    tm = 512
    return pl.pallas_call(
        _rope_kernel,
        grid=(t // tm,),
        in_specs=[pl.BlockSpec((1, tm), lambda i: (0, i)),
                  pl.BlockSpec((HALF_ROT, 1), lambda i: (0, 0))],
        out_specs=[pl.BlockSpec((tm, LANES), lambda i: (i, 0)),
                   pl.BlockSpec((tm, LANES), lambda i: (i, 0))],
        out_shape=[jax.ShapeDtypeStruct((t, LANES), jnp.float32)] * 2,
        compiler_params=_cparams(("arbitrary",)),
        name="rope_tables",
    )(positions.reshape(1, t), inv.reshape(HALF_ROT, 1))


def _head_norm_rope(x, ones_bd, gain_row, cos, sin):
    tm, w = x.shape
    sq = (x * x).astype(jnp.bfloat16)
    outs = []
    bd = ones_bd.shape[0]
    for c in range(w // bd):
        sl = slice(c * bd, (c + 1) * bd)
        ss = _dot(sq[:, sl], ones_bd)
        xn = x[:, sl] * lax.rsqrt(ss * (1.0 / HEAD_DIM) + EPS) * gain_row[:, sl]
        for j in range(bd // LANES):
            xj = xn[:, j * LANES:(j + 1) * LANES]
            lane = lax.broadcasted_iota(jnp.int32, xj.shape, 1) % HEAD_DIM
            rot = jnp.where(lane < HALF_ROT,
                            pltpu.roll(xj, LANES - HALF_ROT, axis=1),
                            pltpu.roll(xj, HALF_ROT, axis=1))
            outs.append(xj * cos + rot * sin)
    return jnp.concatenate(outs, axis=1)


def _residue_major(ref, n, w, dil):
    sub = n // dil
    return jnp.concatenate(
        [jnp.concatenate([ref[pl.ds(c * n + r, sub, stride=dil), :] for c in range(w // LANES)],
                         axis=1) for r in range(dil)], axis=0)


def _even_in_kernel(x_ref, g_ref, w_ref, cos_ref, sin_ref, sgn_ref, sgw_ref, sgb_ref,
                    qn_ref, kn_ref, bd_ref, a_ref, q0_ref, q1_ref, q2_ref, hn_scr):
    x = x_ref[...]
    tm = x.shape[0]
    hn32 = _rms(x, g_ref[...])
    for c in range(ROW_CHUNKS):
        hn_scr[c * tm:(c + 1) * tm, :] = hn32[:, c * LANES:(c + 1) * LANES]
    hn = hn32.astype(jnp.bfloat16)

    u = _dot(hn, w_ref[:, 0:SGU_WIDTH])
    v = _dot(hn, w_ref[:, SGU_WIDTH:2 * SGU_WIDTH])
    gu = jax.nn.gelu(u)
    vn = _rms(jax.nn.gelu(v), sgn_ref[...]).astype(jnp.bfloat16)

    row = lax.broadcasted_iota(jnp.int32, (BLK, BLK), 0)
    col = lax.broadcasted_iota(jnp.int32, (BLK, BLK), 1)
    tril = row >= col
    lane_lo = lax.broadcasted_iota(jnp.int32, (BLK, LANES), 1) < HEAD_DIM
    wts = [jnp.where(tril, sgw_ref[g], 0.0).astype(jnp.bfloat16) for g in range(SGU_GROUPS)]
    for c in range(tm // BLK):
        rows = slice(c * BLK, (c + 1) * BLK)
        for p in range(SGU_GROUPS // 2):
            lanes = slice(p * LANES, (p + 1) * LANES)
            vp = vn[rows, lanes]
            s = jnp.where(lane_lo, _dot(wts[2 * p], vp), _dot(wts[2 * p + 1], vp))
            a_ref[rows, lanes] = (gu[rows, lanes] * (s + sgb_ref[:, lanes])).astype(a_ref.dtype)

    o0 = 2 * SGU_WIDTH
    for g, ((_, dil), o_ref) in enumerate(zip(DIL_GROUPS, (q0_ref, q1_ref, q2_ref))):
        if dil == 1:
            hg, cos, sin = hn, cos_ref[...], sin_ref[...]
        else:
            hg = _residue_major(hn_scr, tm, D_MODEL, dil).astype(jnp.bfloat16)
            cos = _residue_major(cos_ref, tm, LANES, dil)
            sin = _residue_major(sin_ref, tm, LANES, dil)
        gl = slice(g * DIL_GW, (g + 1) * DIL_GW)
        parts = []
        for p, gain_ref in enumerate((qn_ref, kn_ref, None)):
            c0 = o0 + p * DIL_WIDTH + g * DIL_GW
            y = _dot(hg, w_ref[:, c0:c0 + DIL_GW])
            if gain_ref is not None:
                y = _head_norm_rope(y, bd_ref[...], gain_ref[:, gl], cos, sin)
            parts.append(y.astype(o_ref.dtype))
        sub = tm // dil
        for r in range(dil):
            for p, y in enumerate(parts):
                c0 = (r * 3 + p) * DIL_GW
                o_ref[:, c0:c0 + DIL_GW] = y[r * sub:(r + 1) * sub, :]


def _const_spec(shape):
    nd = len(shape)
    return pl.BlockSpec(shape, lambda i: (0,) * nd)


def _even_in(x2, gain, w_in, cos, sin, sgu_norm, sgu_w, sgu_b, qn, kn, ones_bd):
    t = x2.shape[0]
    in_w = w_in.shape[1]
    sgb = jnp.repeat(sgu_b.T, HEAD_DIM, axis=1)
    qn_row = jnp.tile(qn * HEAD_DIM ** -0.5, DIL_WIDTH // HEAD_DIM).reshape(1, DIL_WIDTH)
    kn_row = jnp.tile(kn, DIL_WIDTH // HEAD_DIM).reshape(1, DIL_WIDTH)
    row = lambda n: pl.BlockSpec((TM, n), lambda i: (i, 0))
    return pl.pallas_call(
        _even_in_kernel,
        grid=(t // TM,),
        in_specs=[row(D_MODEL), _const_spec((1, D_MODEL)), _const_spec((D_MODEL, in_w)),
                  row(LANES), row(LANES), _const_spec((1, SGU_WIDTH)),
                  _const_spec((SGU_GROUPS, BLK, BLK)), _const_spec((BLK, SGU_WIDTH)),
                  _const_spec((1, DIL_WIDTH)), _const_spec((1, DIL_WIDTH)),
                  _const_spec(ones_bd.shape)],
        out_specs=[row(SGU_WIDTH)] + [pl.BlockSpec((TM // d, d * 3 * DIL_GW), lambda i: (i, 0))
                                      for _, d in DIL_GROUPS],
        out_shape=[jax.ShapeDtypeStruct((t, SGU_WIDTH), jnp.bfloat16)]
        + [jax.ShapeDtypeStruct((t // d, d * 3 * DIL_GW), jnp.bfloat16) for _, d in DIL_GROUPS],
        scratch_shapes=[pltpu.VMEM((ROW_CHUNKS * TM, LANES), jnp.float32)],
        compiler_params=_cparams(("arbitrary",)),
        name="even_in_proj",
    )(x2, gain.reshape(1, D_MODEL), w_in.astype(jnp.bfloat16), cos, sin,
      sgu_norm.reshape(1, SGU_WIDTH), sgu_w, sgb, qn_row, kn_row, ones_bd)


def _band_mask(max_rel, first):
    i = lax.broadcasted_iota(jnp.int32, (BLK, 2 * BLK), 0)
    j = lax.broadcasted_iota(jnp.int32, (BLK, 2 * BLK), 1)
    rel = BLK + i - j
    first_key = jnp.where(first, BLK, 0)
    return (rel >= 0) & (rel <= max_rel) & (j >= first_key)


def _head_attention(qh, kk, vv, mask, sink):
    s = jnp.where(mask, _dot_nt(qh, kk), NEG)
    m = jnp.max(s, axis=-1, keepdims=True)
    if sink is not None:
        m = jnp.maximum(m, sink)
    p = jnp.exp(s - m)
    denom = jnp.sum(p, axis=-1, keepdims=True)
    if sink is not None:
        denom = denom + jnp.exp(sink - m)
    o = _dot(p.astype(jnp.bfloat16), vv)
    return o / denom, m, denom


def _dil_attn_kernel(q_ref, kc_ref, kp_ref, vc_ref, vp_ref, o_ref, lse_ref, *, max_rel):
    first_tile = pl.program_id(2) == 0
    lane_lo = lax.broadcasted_iota(jnp.int32, (BLK, LANES), 1) < HEAD_DIM
    zero = jnp.zeros((), jnp.bfloat16)
    for j in range(TQ // BLK):
        rows = slice(j * BLK, (j + 1) * BLK)
        if j == 0:
            kprev, vprev = kp_ref[...], vp_ref[...]
            mask = _band_mask(max_rel, first_tile)
        else:
            prows = slice((j - 1) * BLK, j * BLK)
            kprev, vprev = kc_ref[prows, :], vc_ref[prows, :]
            mask = _band_mask(max_rel, False)
        kk = jnp.concatenate([kprev, kc_ref[rows, :]], axis=0)
        vv = jnp.concatenate([vprev, vc_ref[rows, :]], axis=0)
        for p in range(DIL_GW // LANES):
            lanes = slice(p * LANES, (p + 1) * LANES)
            qp = q_ref[rows, lanes]
            oa, ma, da = _head_attention(jnp.where(lane_lo, qp, zero), kk[:, lanes],
                                         vv[:, lanes], mask, None)
            ob, mb, db = _head_attention(jnp.where(lane_lo, zero, qp), kk[:, lanes],
                                         vv[:, lanes], mask, None)
            o_ref[rows, lanes] = jnp.where(lane_lo, oa, ob).astype(o_ref.dtype)
            lse_ref[rows, lanes] = jnp.where(lane_lo, ma + jnp.log(da), mb + jnp.log(db))


def _dilated_group_attention(qkv, batch, seq, dil, window):
    sub = seq // dil
    a = qkv.reshape(batch, sub, dil * 3 * DIL_GW)
    nq = TQ // BLK
    cur = lambda part: pl.BlockSpec((None, TQ, DIL_GW), lambda b, r, i: (b, i, r * 3 + part))
    prev = lambda part: pl.BlockSpec(
        (None, BLK, DIL_GW), lambda b, r, i: (b, jnp.maximum(i * nq - 1, 0), r * 3 + part))
    out = pl.BlockSpec((None, TQ, DIL_GW), lambda b, r, i: (b, i, r))
    o, lse = pl.pallas_call(
        functools.partial(_dil_attn_kernel, max_rel=window // dil),
        grid=(batch, dil, sub // TQ),
        in_specs=[cur(0), cur(1), prev(1), cur(2), prev(2)],
        out_specs=[out, out],
        out_shape=[jax.ShapeDtypeStruct((batch, sub, dil * DIL_GW), jnp.bfloat16),
                   jax.ShapeDtypeStruct((batch, sub, dil * DIL_GW), jnp.float32)],
        compiler_params=_cparams(("arbitrary",) * 3),
        name=f"dilated_attn_d{dil}",
    )(a, a, a, a, a)
    rows = batch * sub
    return o.reshape(rows, dil * DIL_GW), lse.reshape(rows, dil * DIL_GW)


def _swa_kernel(sink_ref, q_ref, kvc_ref, kvp_ref, o_ref):
    first_tile = pl.program_id(1) == 0
    lane_lo = lax.broadcasted_iota(jnp.int32, (BLK, LANES), 1) < HEAD_DIM
    zero = jnp.zeros((), jnp.bfloat16)
    rep = SWA_Q_HEADS // SWA_KV_HEADS
    kw = SWA_KV_HEADS * LANES
    for j in range(TQ // BLK):
        rows = slice(j * BLK, (j + 1) * BLK)
        if j == 0:
            kvprev = kvp_ref[...]
            mask = _band_mask(SWA_WINDOW - 1, first_tile)
        else:
            kvprev = kvc_ref[(j - 1) * BLK:j * BLK, :]
            mask = _band_mask(SWA_WINDOW - 1, False)
        kv = jnp.concatenate([kvprev, kvc_ref[rows, :]], axis=0)
        for g in range(SWA_KV_HEADS):
            kk = kv[:, g * LANES:(g + 1) * LANES]
            vv = kv[:, kw + g * LANES:kw + (g + 1) * LANES]
            for p in range(rep // 2):
                h0 = g * rep + 2 * p
                lanes = slice(h0 * HEAD_DIM, (h0 + 2) * HEAD_DIM)
                qp = q_ref[rows, lanes]
                oa, _, _ = _head_attention(jnp.where(lane_lo, qp, zero), kk, vv, mask,
                                           sink_ref[h0])
                ob, _, _ = _head_attention(jnp.where(lane_lo, zero, qp), kk, vv, mask,
                                           sink_ref[h0 + 1])
                o_ref[rows, lanes] = jnp.where(lane_lo, oa, ob).astype(o_ref.dtype)


def _swa_attention(q, kv, sinks, batch, seq):
    q3 = q.reshape(batch, seq, q.shape[1])
    kv3 = kv.reshape(batch, seq, kv.shape[1])
    nq = TQ // BLK
    o = pl.pallas_call(
        _swa_kernel,
        grid_spec=pltpu.PrefetchScalarGridSpec(
            num_scalar_prefetch=1,
            grid=(batch, seq // TQ),
            in_specs=[pl.BlockSpec((None, TQ, q.shape[1]), lambda b, i, s: (b, i, 0)),
                      pl.BlockSpec((None, TQ, kv.shape[1]), lambda b, i, s: (b, i, 0)),
                      pl.BlockSpec((None, BLK, kv.shape[1]),
                                   lambda b, i, s: (b, jnp.maximum(i * nq - 1, 0), 0))],
            out_specs=pl.BlockSpec((None, TQ, q.shape[1]), lambda b, i, s: (b, i, 0))),
        out_shape=jax.ShapeDtypeStruct(q3.shape, jnp.bfloat16),
        compiler_params=_cparams(("arbitrary",) * 2),
        name="swa_attn",
    )(sinks.astype(jnp.float32), q3, kv3, kv3)
    return o.reshape(batch * seq, q.shape[1])


def _odd_in_kernel(x_ref, g_ref, w_ref, cos_ref, sin_ref, qn_ref, kn_ref, bd_ref, q_ref, kv_ref):
    hn = _rms(x_ref[...], g_ref[...]).astype(jnp.bfloat16)
    qw = SWA_Q_HEADS * HEAD_DIM
    kw = SWA_KV_HEADS * LANES
    cos = cos_ref[...]
    sin = sin_ref[...]
    q = _dot(hn, w_ref[:, 0:qw])
    q_ref[...] = _head_norm_rope(q, bd_ref[...], qn_ref[...], cos, sin).astype(q_ref.dtype)
    k = _dot(hn, w_ref[:, qw:qw + kw])
    kv_ref[:, 0:kw] = _head_norm_rope(k, bd_ref[...], kn_ref[...], cos, sin).astype(kv_ref.dtype)
    kv_ref[:, kw:2 * kw] = _dot(hn, w_ref[:, qw + kw:qw + 2 * kw]).astype(kv_ref.dtype)


def _odd_in(x, gain, w_in, cos, sin, qn, kn, ones_bd):
    t = x.shape[0]
    qw = SWA_Q_HEADS * HEAD_DIM
    kvw = SWA_KV_HEADS * HEAD_DIM
    dup = lambda w: jnp.concatenate(
        [w[:, h * HEAD_DIM:(h + 1) * HEAD_DIM] for h in range(SWA_KV_HEADS) for _ in range(2)],
        axis=1)
    w_all = jnp.concatenate([w_in[:, :qw], dup(w_in[:, qw:qw + kvw]),
                             dup(w_in[:, qw + kvw:qw + 2 * kvw])], axis=1).astype(jnp.bfloat16)
    kw = SWA_KV_HEADS * LANES
    qn_row = jnp.tile(qn * HEAD_DIM ** -0.5, qw // HEAD_DIM).reshape(1, qw)
    kn_row = jnp.tile(kn, kw // HEAD_DIM).reshape(1, kw)
    row = lambda n: pl.BlockSpec((TM, n), lambda i: (i, 0))
    return pl.pallas_call(
        _odd_in_kernel,
        grid=(t // TM,),
        in_specs=[row(D_MODEL), _const_spec((1, D_MODEL)), _const_spec(w_all.shape),
                  row(LANES), row(LANES), _const_spec((1, qw)), _const_spec((1, kw)),
                  _const_spec(ones_bd.shape)],
        out_specs=[row(qw), row(2 * kw)],
        out_shape=[jax.ShapeDtypeStruct((t, qw), jnp.bfloat16),
                   jax.ShapeDtypeStruct((t, 2 * kw), jnp.bfloat16)],
        compiler_params=_cparams(("arbitrary",)),
        name="odd_in_proj",
    )(x, gain.reshape(1, D_MODEL), w_all, cos, sin, qn_row, kn_row, ones_bd)


def _route(hn, wr_ref, br_ref, carry_ref, idx_ref, gate_ref, cnt_ref):
    tm = hn.shape[0]
    h_hi = hn.astype(jnp.bfloat16)
    h_lo = (hn - h_hi.astype(jnp.float32)).astype(jnp.bfloat16)
    hi_pass = _dot(h_hi, wr_ref[...])
    logits = (hi_pass[:, :ROUTER_COLS] + _dot(h_lo, wr_ref[:, :ROUTER_COLS])
              + hi_pass[:, ROUTER_COLS:] + br_ref[...])
    lt = logits.T
    rowf = lax.broadcasted_iota(jnp.int32, (SUBLANES, tm), 0).astype(jnp.float32)

    def first_argmax(x):
        m = jnp.max(x, axis=0, keepdims=True)
        idx = jnp.min(jnp.where(x == m, rowf, float(SUBLANES)), axis=0, keepdims=True)
        return m, idx

    lg = jnp.where(rowf < N_GROUPS, lt[0:SUBLANES], NEG)
    mg, gi = first_argmax(lg)
    p_top = 1.0 / jnp.sum(jnp.exp(lg - mg), axis=0, keepdims=True)
    le = lt[ROUTER_E0:ROUTER_E0 + EPG]
    for g in range(1, N_GROUPS):
        le = jnp.where(gi == float(g), lt[ROUTER_E0 + g * EPG:ROUTER_E0 + (g + 1) * EPG], le)
    m1, i1 = first_argmax(le)
    m2, i2 = first_argmax(jnp.where(rowf == i1, NEG, le))
    e2 = jnp.exp(m2 - m1)
    gate1 = p_top / (1.0 + e2)
    gate2 = p_top * e2 / (1.0 + e2)
    eid1 = gi * EPG + i1
    eid2 = gi * EPG + i2

    erow = lax.broadcasted_iota(jnp.int32, (N_EXPERTS, tm), 0).astype(jnp.float32)
    oh1 = erow == eid1
    oh2 = erow == eid2
    member = jnp.where(oh1 | oh2, 1.0, 0.0)
    s_idx = lax.broadcasted_iota(jnp.int32, (tm, tm), 0)
    t_idx = lax.broadcasted_iota(jnp.int32, (tm, tm), 1)
    upper = jnp.where(s_idx < t_idx, 1.0, 0.0).astype(jnp.bfloat16)
    before = carry_ref[:, 0:1] + _dot(member.astype(jnp.bfloat16), upper)
    rank1 = jnp.sum(jnp.where(oh1, before, 0.0), axis=0, keepdims=True)
    rank2 = jnp.sum(jnp.where(oh2, before, 0.0), axis=0, keepdims=True)
    carry_ref[...] = carry_ref[...] + jnp.sum(member, axis=1, keepdims=True)
    cnt_ref[...] = carry_ref[...].astype(jnp.int32)

    idx = jnp.where(rowf == 0.0, eid1, jnp.where(rowf == 1.0, eid2,
          jnp.where(rowf == 2.0, rank1, jnp.where(rowf == 3.0, rank2, 0.0))))
    idx_ref[...] = idx.astype(jnp.int32)
    gates = jnp.where(rowf == 0.0, gate1, jnp.where(rowf == 1.0, gate2, 0.0))
    padded = jnp.concatenate([gates] + [jnp.zeros_like(gates)] * (LANES // SUBLANES - 1), axis=0)
    gate_ref[...] = padded.T[:, :SUBLANES]


def _store_row_tiles(ref, x):
    n = x.shape[0]
    for c in range(ROW_CHUNKS):
        ref[pl.ds(c, n, stride=ROW_CHUNKS), :] = x[:, c * LANES:(c + 1) * LANES].astype(ref.dtype)


def _load_row_tiles(ref, n):
    return jnp.concatenate([ref[pl.ds(c, n, stride=ROW_CHUNKS), :] for c in range(ROW_CHUNKS)],
                           axis=1)


def _post_proj(y, x_ref, gf_ref, wr_ref, br_ref, carry_ref,
               h_ref, hn_ref, idx_ref, gate_ref, cnt_ref):
    @pl.when(pl.program_id(0) == 0)
    def _():
        carry_ref[...] = jnp.zeros_like(carry_ref)

    h = x_ref[...] + y
    h_ref[...] = h
    hn = _rms(h, gf_ref[...])
    _store_row_tiles(hn_ref, hn)
    _route(hn, wr_ref, br_ref, carry_ref, idx_ref, gate_ref, cnt_ref)


def _token_major(ref, dil, scr):
    if dil == 1:
        return ref[...].astype(jnp.float32)
    sub = ref.shape[0]
    n = sub * dil
    chunks = DIL_GW // LANES
    for r in range(dil):
        for c in range(chunks):
            l0 = r * DIL_GW + c * LANES
            scr[pl.ds(c * n + r, sub, stride=dil), :] = ref[:, l0:l0 + LANES].astype(jnp.float32)
    return jnp.concatenate([scr[c * n:(c + 1) * n, :] for c in range(chunks)], axis=1)


def _even_out_kernel(a_ref, o0_ref, o1_ref, o2_ref, l0_ref, l1_ref, l2_ref, w_ref, x_ref, gf_ref,
                     wr_ref, br_ref, h_ref, hn_ref, idx_ref, gate_ref, cnt_ref,
                     carry_ref, so1, so2, sl1, sl2):
    dils = [d for _, d in DIL_GROUPS]
    o0, o1, o2 = (_token_major(r, d, s) for r, d, s in zip((o0_ref, o1_ref, o2_ref), dils,
                                                            (None, so1, so2)))
    l0, l1, l2 = (_token_major(r, d, s) for r, d, s in zip((l0_ref, l1_ref, l2_ref), dils,
                                                            (None, sl1, sl2)))
    m = jnp.maximum(jnp.maximum(l0, l1), l2)
    e0, e1, e2 = jnp.exp(l0 - m), jnp.exp(l1 - m), jnp.exp(l2 - m)
    b = (e0 * o0 + e1 * o1 + e2 * o2) / (e0 + e1 + e2)
    y = _dot(a_ref[...], w_ref[0:SGU_WIDTH, :]) + _dot(b.astype(jnp.bfloat16),
                                                       w_ref[SGU_WIDTH:, :])
    _post_proj(y, x_ref, gf_ref, wr_ref, br_ref, carry_ref,
               h_ref, hn_ref, idx_ref, gate_ref, cnt_ref)


def _odd_out_kernel(o_ref, w_ref, x_ref, gf_ref, wr_ref, br_ref,
                    h_ref, hn_ref, idx_ref, gate_ref, cnt_ref, carry_ref):
    y = _dot(o_ref[...], w_ref[...])
    _post_proj(y, x_ref, gf_ref, wr_ref, br_ref, carry_ref,
               h_ref, hn_ref, idx_ref, gate_ref, cnt_ref)


def _router_weights(w_rg, b_rg, w_re, b_re):
    d = w_rg.shape[0]
    w = jnp.zeros((d, ROUTER_COLS), jnp.float32)
    w = w.at[:, 0:N_GROUPS].set(w_rg)
    w = w.at[:, ROUTER_E0:ROUTER_E0 + N_EXPERTS].set(
        jnp.transpose(w_re, (1, 0, 2)).reshape(d, N_EXPERTS))
    b = jnp.zeros((1, ROUTER_COLS), jnp.float32)
    b = b.at[0, 0:N_GROUPS].set(b_rg)
    b = b.at[0, ROUTER_E0:ROUTER_E0 + N_EXPERTS].set(b_re.reshape(N_EXPERTS))
    w_hi = w.astype(jnp.bfloat16)
    w_lo = (w - w_hi.astype(jnp.float32)).astype(jnp.bfloat16)
    return jnp.concatenate([w_hi, w_lo], axis=1), b


def _out_proj(kernel, acts, w_out, x2, gain_ffn, router, extra_scratch=()):
    t = x2.shape[0]
    w_router, b = router
    row = lambda n: pl.BlockSpec((TM, n), lambda i: (i, 0))
    colblk = pl.BlockSpec((SUBLANES, TM), lambda i: (0, i))
    return pl.pallas_call(
        kernel,
        grid=(t // TM,),
        in_specs=[pl.BlockSpec((TM * a.shape[0] // t, a.shape[1]), lambda i: (i, 0)) for a in acts]
        + [_const_spec(w_out.shape), row(D_MODEL), _const_spec((1, D_MODEL)),
           _const_spec(w_router.shape), _const_spec(b.shape)],
        out_specs=[row(D_MODEL), pl.BlockSpec((TM * ROW_CHUNKS, LANES), lambda i: (i, 0)),
                   colblk, row(SUBLANES),
                   _const_spec((N_EXPERTS, LANES))],
        out_shape=[jax.ShapeDtypeStruct((t, D_MODEL), jnp.float32),
                   jax.ShapeDtypeStruct((t * ROW_CHUNKS, LANES), ROW_DTYPE),
                   jax.ShapeDtypeStruct((SUBLANES, t), jnp.int32),
                   jax.ShapeDtypeStruct((t, SUBLANES), jnp.float32),
                   jax.ShapeDtypeStruct((N_EXPERTS, LANES), jnp.int32)],
        scratch_shapes=[pltpu.VMEM((N_EXPERTS, LANES), jnp.float32)] + list(extra_scratch),
        compiler_params=_cparams(("arbitrary",)),
        name=kernel.__name__.strip("_"),
    )(*acts, w_out.astype(jnp.bfloat16), x2, gain_ffn.reshape(1, D_MODEL), w_router, b)


ROWS_PER_TILE = TOP_K * TM
DISPATCH_TM = 1024
ISSUE_UNROLL = 8
FFN_SLOTS = 3


def _row_tile(ref, row):
    return ref.at[pl.ds(pl.multiple_of(row * ROW_CHUNKS, ROW_CHUNKS), ROW_CHUNKS), :]


def _wait_rows(copy, n):
    def body(_, c):
        copy.wait()
        return c

    lax.fori_loop(0, n, body, 0, unroll=16)


def _dispatch_kernel(pstart_ref, pend_ref, dest0_ref, dest1_ref, hn_ref, xs_ref, zero_ref, sem,
                     zsem):
    blk = MOE_BM * ROW_CHUNKS

    def zero_block(row0):
        return pltpu.make_async_copy(
            zero_ref, xs_ref.at[pl.ds(pl.multiple_of(row0 * ROW_CHUNKS, blk), blk), :], zsem)

    @pl.when(pl.program_id(0) == 0)
    def _():
        zero_ref[...] = jnp.zeros_like(zero_ref)

        def start_last(c, _):
            @pl.when(pend_ref[c] > pstart_ref[c])
            def _():
                zero_block(pend_ref[c] - MOE_BM).start()
            return 0

        def wait_last(c, _):
            @pl.when(pend_ref[c] > pstart_ref[c])
            def _():
                zero_block(pend_ref[c] - MOE_BM).wait()
            return 0

        lax.fori_loop(0, N_EXPERTS, start_last, 0)
        lax.fori_loop(0, N_EXPERTS, wait_last, 0)
        first_unused = lax.div(pend_ref[N_EXPERTS - 1], MOE_BM)
        n_blocks = xs_ref.shape[0] // blk

        def start_tail(b, _):
            zero_block(b * MOE_BM).start()
            return 0

        def wait_tail(b, _):
            zero_block(b * MOE_BM).wait()
            return 0

        lax.fori_loop(first_unused, n_blocks, start_tail, 0)
        lax.fori_loop(first_unused, n_blocks, wait_tail, 0)

    dest_refs = (dest0_ref, dest1_ref)

    def row_copy(src, dst):
        return pltpu.make_async_copy(_row_tile(hn_ref, src), _row_tile(xs_ref, dst), sem)

    def issue(b, c):
        for j in range(ISSUE_UNROLL):
            t = b * ISSUE_UNROLL + j
            for k, d_ref in enumerate(dest_refs):
                row_copy(t, d_ref[0, t]).start(priority=k)
        return c

    lax.fori_loop(0, DISPATCH_TM // ISSUE_UNROLL, issue, 0)
    _wait_rows(row_copy(0, 0), TOP_K * DISPATCH_TM)


def _dispatch(hn, dest, pstart, pend, rows):
    t = hn.shape[0] // ROW_CHUNKS
    n = t // DISPATCH_TM
    dspec = pl.BlockSpec((None, 1, DISPATCH_TM), lambda i, ps, pe: (i, 0, 0),
                         memory_space=pltpu.SMEM)
    return pl.pallas_call(
        _dispatch_kernel,
        grid_spec=pltpu.PrefetchScalarGridSpec(
            num_scalar_prefetch=2,
            grid=(n,),
            in_specs=[dspec] * TOP_K + [
                      pl.BlockSpec((DISPATCH_TM * ROW_CHUNKS, LANES), lambda i, ps, pe: (i, 0))],
            out_specs=pl.BlockSpec(memory_space=pl.ANY),
            scratch_shapes=[pltpu.VMEM((MOE_BM * ROW_CHUNKS, LANES), ROW_DTYPE),
                            pltpu.SemaphoreType.DMA(()), pltpu.SemaphoreType.DMA(())]),
        out_shape=jax.ShapeDtypeStruct((rows * ROW_CHUNKS, LANES), ROW_DTYPE),
        compiler_params=_cparams(("arbitrary",)),
        name="moe_dispatch",
    )(pstart, pend, *[dest[k].reshape(n, 1, DISPATCH_TM) for k in range(TOP_K)], hn)


def _expert_kernel(blk_e_ref, wslot_ref, next_e_ref, nused_ref, xs_ref, wg_ref, wu_ref, wd_ref,
                   ys_ref, wg_s, wu_s, wd_s, wg_buf, wu_buf, wd_buf, xbuf, ybuf, zbuf,
                   w_sems, in_sems, out_sems, zsem, *, layer):
    i = pl.program_id(0)
    n_steps = pl.num_programs(0)
    nu = nused_ref[0]
    blk = MOE_BM * ROW_CHUNKS

    def weight_copies(e, s):
        return [pltpu.make_async_copy(w_ref.at[layer, e], buf.at[s], w_sems.at[s])
                for w_ref, buf in ((wg_ref, wg_buf), (wu_ref, wu_buf), (wd_ref, wd_buf))]

    def block(ref, b):
        return ref.at[pl.ds(pl.multiple_of(b * blk, blk), blk), :]

    def load(b):
        s = lax.rem(b, FFN_SLOTS)
        return pltpu.make_async_copy(block(xs_ref, b), xbuf.at[s], in_sems.at[s])

    def store(b):
        s = lax.rem(b, FFN_SLOTS)
        return pltpu.make_async_copy(ybuf.at[s], block(ys_ref, b), out_sems.at[s])

    @pl.when(i == 0)
    def _():
        for c in weight_copies(blk_e_ref[0], wslot_ref[blk_e_ref[0]]):
            c.start()
        for b in range(FFN_SLOTS - 1):
            @pl.when(b < nu)
            def _():
                load(i + b).start()

    @pl.when(i + (FFN_SLOTS - 1) < nu)
    def _():
        load(i + (FFN_SLOTS - 1)).start()

    @pl.when(i < nu)
    def _():
        @pl.when((i == 0) | (blk_e_ref[i] != blk_e_ref[jnp.maximum(i - 1, 0)]))
        def _():
            e = blk_e_ref[i]
            ws = wslot_ref[e]
            for c in weight_copies(e, ws):
                c.wait()

            @pl.when(next_e_ref[e] >= 0)
            def _():
                for c in weight_copies(next_e_ref[e], 1 - ws):
                    c.start()

            wg_s[...] = wg_buf[ws].astype(jnp.bfloat16)
            wu_s[...] = wu_buf[ws].astype(jnp.bfloat16)
            wd_s[...] = wd_buf[ws].astype(jnp.bfloat16)

        slot = lax.rem(i, FFN_SLOTS)
        load(i).wait()

        @pl.when(i >= FFN_SLOTS)
        def _():
            store(i - FFN_SLOTS).wait()

        x = _load_row_tiles(xbuf.at[slot], MOE_BM).astype(jnp.bfloat16)
        hid = jax.nn.silu(_dot(x, wg_s[...])) * _dot(x, wu_s[...])
        _store_row_tiles(ybuf.at[slot], _dot(hid.astype(jnp.bfloat16), wd_s[...]))
        store(i).start()

    @pl.when(i >= nu)
    def _():
        @pl.when(i == nu)
        def _():
            zbuf[...] = jnp.zeros_like(zbuf)

        fill = pltpu.make_async_copy(zbuf, block(ys_ref, i), zsem)
        fill.start()
        fill.wait()

    @pl.when(i == n_steps - 1)
    def _():
        for d in range(FFN_SLOTS):
            b = nu - 1 - d

            @pl.when(b >= 0)
            def _():
                store(b).wait()


def _expert_ffn(xs, blk_e, wslot, next_e, nused, layer, w_gate, w_up, w_down):
    rows = xs.shape[0] // ROW_CHUNKS
    blk = MOE_BM * ROW_CHUNKS
    ring = pltpu.VMEM((FFN_SLOTS, blk, LANES), ROW_DTYPE)
    any_spec = pl.BlockSpec(memory_space=pl.ANY)
    return pl.pallas_call(
        functools.partial(_expert_kernel, layer=layer),
        grid_spec=pltpu.PrefetchScalarGridSpec(
            num_scalar_prefetch=4,
            grid=(rows // MOE_BM,),
            in_specs=[any_spec] * 4,
            out_specs=any_spec,
            scratch_shapes=[pltpu.VMEM((D_MODEL, EXPERT_FF), jnp.bfloat16),
                            pltpu.VMEM((D_MODEL, EXPERT_FF), jnp.bfloat16),
                            pltpu.VMEM((EXPERT_FF, D_MODEL), jnp.bfloat16),
                            pltpu.VMEM((2, D_MODEL, EXPERT_FF), jnp.float32),
                            pltpu.VMEM((2, D_MODEL, EXPERT_FF), jnp.float32),
                            pltpu.VMEM((2, EXPERT_FF, D_MODEL), jnp.float32),
                            ring, ring, pltpu.VMEM((blk, LANES), ROW_DTYPE),
                            pltpu.SemaphoreType.DMA((2,)),
                            pltpu.SemaphoreType.DMA((FFN_SLOTS,)),
                            pltpu.SemaphoreType.DMA((FFN_SLOTS,)),
                            pltpu.SemaphoreType.DMA(())]),
        out_shape=jax.ShapeDtypeStruct((rows * ROW_CHUNKS, LANES), ROW_DTYPE),
        compiler_params=_cparams(("arbitrary",)),
        name="moe_expert_ffn",
    )(blk_e, wslot, next_e, nused, xs, w_gate, w_up, w_down)


def _combine_kernel(d0_ref, d1_ref, d0_next_ref, d1_next_ref, h_ref, gate_ref, ys_ref, out_ref,
                    buf_ref, sems):
    i = pl.program_id(0)
    n = pl.num_programs(0)

    def row_copy(src, slot, k, t):
        return pltpu.make_async_copy(_row_tile(ys_ref, src), _row_tile(buf_ref.at[slot, k], t),
                                     sems.at[slot])

    def issue_tile(d_refs, slot):
        def issue(b, c):
            for j in range(ISSUE_UNROLL):
                t = b * ISSUE_UNROLL + j
                for k, d_ref in enumerate(d_refs):
                    row_copy(d_ref[0, t], slot, k, t).start(priority=k)
            return c

        lax.fori_loop(0, TM // ISSUE_UNROLL, issue, 0)

    slot = lax.rem(i, 2)

    @pl.when(i == 0)
    def _():
        issue_tile((d0_ref, d1_ref), 0)

    @pl.when(i + 1 < n)
    def _():
        issue_tile((d0_next_ref, d1_next_ref), 1 - slot)

    _wait_rows(row_copy(0, slot, 0, 0), ROWS_PER_TILE)
    g = gate_ref[...]
    r1 = _load_row_tiles(buf_ref.at[slot, 0], TM)
    r2 = _load_row_tiles(buf_ref.at[slot, 1], TM)
    out_ref[...] = h_ref[...] + (r1 * g[:, 0:1] + r2 * g[:, 1:2])


def _combine(h, gate_cols, ys, dest):
    t = h.shape[0]
    n = t // TM
    dest_tiles = [dest[k].reshape(n, 1, TM) for k in range(TOP_K)]
    dspec = lambda f: pl.BlockSpec((None, 1, TM), f, memory_space=pltpu.SMEM)
    cur = dspec(lambda i: (i, 0, 0))
    nxt = dspec(lambda i: (jnp.minimum(i + 1, n - 1), 0, 0))
    return pl.pallas_call(
        _combine_kernel,
        grid=(n,),
        in_specs=[cur] * TOP_K + [nxt] * TOP_K + [
                  pl.BlockSpec((TM, D_MODEL), lambda i: (i, 0)),
                  pl.BlockSpec((TM, SUBLANES), lambda i: (i, 0)),
                  pl.BlockSpec(memory_space=pl.ANY)],
        out_specs=pl.BlockSpec((TM, D_MODEL), lambda i: (i, 0)),
        scratch_shapes=[pltpu.VMEM((2, TOP_K, TM * ROW_CHUNKS, LANES), ROW_DTYPE),
                        pltpu.SemaphoreType.DMA((2,))],
        out_shape=jax.ShapeDtypeStruct((t, D_MODEL), jnp.float32),
        compiler_params=_cparams(("arbitrary",)),
        name="moe_combine",
    )(*dest_tiles, *dest_tiles, h, gate_cols, ys)


def _moe(h, hn, idx, gates, counts, layer, w_gate, w_up, w_down):
    t = h.shape[0]
    rows = t * TOP_K + N_EXPERTS * MOE_BM
    nblk = rows // MOE_BM
    cnt = counts[:, 0]
    padded = (cnt + MOE_BM - 1) // MOE_BM * MOE_BM
    pend = jnp.cumsum(padded).astype(jnp.int32)
    pstart = pend - padded
    blk_row0 = jnp.arange(nblk, dtype=jnp.int32) * MOE_BM
    blk_e = jnp.minimum(jnp.sum(pend[None, :] <= blk_row0[:, None], axis=1),
                        N_EXPERTS - 1).astype(jnp.int32)
    nused = (pend[-1:] // MOE_BM).astype(jnp.int32)
    experts = jnp.arange(N_EXPERTS, dtype=jnp.int32)
    nonempty = cnt > 0
    wslot = ((jnp.cumsum(nonempty) - nonempty) % 2).astype(jnp.int32)
    later = jnp.where(nonempty[None, :] & (experts[None, :] > experts[:, None]),
                      experts[None, :], N_EXPERTS)
    next_e = jnp.min(later, axis=1)
    next_e = jnp.where(next_e == N_EXPERTS, -1, next_e).astype(jnp.int32)
    eid, rank = idx[0:TOP_K], idx[TOP_K:2 * TOP_K]
    seg_start = jnp.sum(jnp.where(eid[..., None] == jnp.arange(N_EXPERTS), pstart, 0), axis=-1)
    dest = seg_start + rank
    xs = _dispatch(hn, dest, pstart, pend, rows)
    ys = _expert_ffn(xs, blk_e, wslot, next_e, nused, layer, w_gate, w_up, w_down)
    return _combine(h, gates, ys, dest)


def kernel(x, positions, norm_mix, norm_ffn, w_in_even, w_out_even, sgu_norm, sgu_w, sgu_b,
           qn_dil, kn_dil, w_in_odd, w_out_odd, qn_swa, kn_swa, sinks,
           w_router_g, b_router_g, w_router_e, b_router_e, w_gate, w_up, w_down):
    batch, seq, d = x.shape
    t = batch * seq
    depth = norm_mix.shape[0]
    cos, sin = _rope_tables(positions)
    bd = 2 * LANES
    ones_bd = (jnp.arange(bd)[:, None] // HEAD_DIM == jnp.arange(bd)[None, :] // HEAD_DIM
               ).astype(jnp.bfloat16)
    h = x.reshape(t, d)
    for layer in range(depth):
        i = layer // 2
        router = _router_weights(w_router_g[layer], b_router_g[layer], w_router_e[layer],
                                 b_router_e[layer])
        if layer % 2 == 0:
            a, *qkv = _even_in(h, norm_mix[layer], w_in_even[i], cos, sin, sgu_norm[i],
                               sgu_w[i], sgu_b[i], qn_dil[i], kn_dil[i], ones_bd)
            outs, lses = [], []
            for g, (window, dil) in enumerate(DIL_GROUPS):
                o, lse = _dilated_group_attention(qkv[g], batch, seq, dil, window)
                outs.append(o)
                lses.append(lse)
            stage = [pltpu.VMEM((TM * DIL_GW // LANES, LANES), jnp.float32)] * 4
            res = _out_proj(_even_out_kernel, [a] + outs + lses, w_out_even[i], h,
                            norm_ffn[layer], router, stage)
        else:
            q, kv = _odd_in(h, norm_mix[layer], w_in_odd[i], cos, sin, qn_swa[i], kn_swa[i],
                            ones_bd)
            o = _swa_attention(q, kv, sinks[i], batch, seq)
            res = _out_proj(_odd_out_kernel, [o], w_out_odd[i], h, norm_ffn[layer], router)
        h_mid, hn, idx, gates, counts = res
        h = _moe(h_mid, hn, idx, gates, counts, layer, w_gate, w_up, w_down)
    return h.reshape(batch, seq, d)
```

```python
import functools

import jax
import jax.numpy as jnp
from jax import lax
from jax.experimental import pallas as pl
from jax.experimental.pallas import tpu as pltpu

D_MODEL = 1024
HEAD_DIM = 64
BLK = 128
ROPE_THETA = 500000.0
ROT_DIM = HEAD_DIM // 4
HALF_ROT = ROT_DIM // 2
EPS = 1e-6
SGU_GROUPS = 8
SGU_WIDTH = SGU_GROUPS * HEAD_DIM
DIL_GROUPS = ((128, 1), (512, 4), (2048, 16))
DIL_HPG = 4
DIL_GW = DIL_HPG * HEAD_DIM
DIL_WIDTH = DIL_GW * len(DIL_GROUPS)
SWA_Q_HEADS = 16
SWA_KV_HEADS = 2
SWA_WINDOW = 128
N_GROUPS = 4
EPG = 8
N_EXPERTS = N_GROUPS * EPG
EXPERT_FF = 256
TOP_K = 2

LANES = 128
SUBLANES = 8
VMEM_LIMIT_BYTES = 48 * 1024 * 1024

TM = 512
TQ = 512
DIL_TQ = 1024
MOE_BM = 256
NEG = -1e30

ROW_CHUNKS = D_MODEL // LANES
ROW_DTYPE = jnp.float32
ROUTER_COLS = LANES
ROUTER_E0 = SUBLANES


def _cparams(sem):
    return pltpu.CompilerParams(dimension_semantics=sem, vmem_limit_bytes=VMEM_LIMIT_BYTES)


def _rms(x, gain_row):
    return x * lax.rsqrt(jnp.mean(x * x, axis=-1, keepdims=True) + EPS) * gain_row


def _dot(a, b):
    return jnp.dot(a, b, preferred_element_type=jnp.float32)


def _dot_nt(a, b):
    return lax.dot_general(a, b, (((1,), (1,)), ((), ())), preferred_element_type=jnp.float32)


def _rope_kernel(pos_ref, inv_ref, cos_ref, sin_ref):
    ang = inv_ref[...] * pos_ref[...].astype(jnp.float32)
    c = jnp.cos(ang)
    s = jnp.sin(ang)
    rest = HEAD_DIM // SUBLANES - 2
    head_c = [c, c] + [jnp.ones_like(c)] * rest
    head_s = [-s, s] + [jnp.zeros_like(s)] * rest
    cos_ref[...] = jnp.concatenate(head_c * 2, axis=0).T
    sin_ref[...] = jnp.concatenate(head_s * 2, axis=0).T


def _rope_tables(positions):
    t = positions.size
    inv = ROPE_THETA ** (-jnp.arange(0, ROT_DIM, 2, dtype=jnp.float32) / ROT_DIM)
    tm = 2048
    return pl.pallas_call(
        _rope_kernel,
        grid=(t // tm,),
        in_specs=[pl.BlockSpec((1, tm), lambda i: (0, i)),
                  pl.BlockSpec((HALF_ROT, 1), lambda i: (0, 0))],
        out_specs=[pl.BlockSpec((tm, LANES), lambda i: (i, 0)),
                   pl.BlockSpec((tm, LANES), lambda i: (i, 0))],
        out_shape=[jax.ShapeDtypeStruct((t, LANES), jnp.float32)] * 2,
        compiler_params=_cparams(("arbitrary",)),
        name="rope_tables",
    )(positions.reshape(1, t), inv.reshape(HALF_ROT, 1))


def _head_norm_rope(x, ones_bd, gain_row, cos, sin):
    tm, w = x.shape
    sq = (x * x).astype(jnp.bfloat16)
    outs = []
    bd = ones_bd.shape[0]
    for c in range(w // bd):
        sl = slice(c * bd, (c + 1) * bd)
        ss = _dot(sq[:, sl], ones_bd)
        xn = x[:, sl] * lax.rsqrt(ss * (1.0 / HEAD_DIM) + EPS) * gain_row[:, sl]
        for j in range(bd // LANES):
            xj = xn[:, j * LANES:(j + 1) * LANES]
            lane = lax.broadcasted_iota(jnp.int32, xj.shape, 1) % HEAD_DIM
            rot = jnp.where(lane < HALF_ROT,
                            pltpu.roll(xj, LANES - HALF_ROT, axis=1),
                            pltpu.roll(xj, HALF_ROT, axis=1))
            outs.append(xj * cos + rot * sin)
    return jnp.concatenate(outs, axis=1)


def _residue_major(ref, n, w, dil):
    sub = n // dil
    return jnp.concatenate(
        [jnp.concatenate([ref[pl.ds(c * n + r, sub, stride=dil), :] for c in range(w // LANES)],
                         axis=1) for r in range(dil)], axis=0)


def _even_in_kernel(x_ref, g_ref, w_ref, cos_ref, sin_ref, sgn_ref, sgw_ref, sgb_ref,
                    qn_ref, kn_ref, bd_ref, a_ref, q0_ref, q1_ref, q2_ref, hn_scr):
    x = x_ref[...]
    tm = x.shape[0]
    hn32 = _rms(x, g_ref[...])
    for c in range(ROW_CHUNKS):
        hn_scr[c * tm:(c + 1) * tm, :] = hn32[:, c * LANES:(c + 1) * LANES]
    hn = hn32.astype(jnp.bfloat16)

    u = _dot(hn, w_ref[:, 0:SGU_WIDTH])
    v = _dot(hn, w_ref[:, SGU_WIDTH:2 * SGU_WIDTH])
    gu = jax.nn.gelu(u)
    vn = _rms(jax.nn.gelu(v), sgn_ref[...]).astype(jnp.bfloat16)

    row = lax.broadcasted_iota(jnp.int32, (BLK, BLK), 0)
    col = lax.broadcasted_iota(jnp.int32, (BLK, BLK), 1)
    tril = row >= col
    lane_lo = lax.broadcasted_iota(jnp.int32, (BLK, LANES), 1) < HEAD_DIM
    wts = [jnp.where(tril, sgw_ref[g], 0.0).astype(jnp.bfloat16) for g in range(SGU_GROUPS)]
    for c in range(tm // BLK):
        rows = slice(c * BLK, (c + 1) * BLK)
        for p in range(SGU_GROUPS // 2):
            lanes = slice(p * LANES, (p + 1) * LANES)
            vp = vn[rows, lanes]
            s = jnp.where(lane_lo, _dot(wts[2 * p], vp), _dot(wts[2 * p + 1], vp))
            a_ref[rows, lanes] = (gu[rows, lanes] * (s + sgb_ref[:, lanes])).astype(a_ref.dtype)

    o0 = 2 * SGU_WIDTH
    for g, ((_, dil), o_ref) in enumerate(zip(DIL_GROUPS, (q0_ref, q1_ref, q2_ref))):
        if dil == 1:
            hg, cos, sin = hn, cos_ref[...], sin_ref[...]
        else:
            hg = _residue_major(hn_scr, tm, D_MODEL, dil).astype(jnp.bfloat16)
            cos = _residue_major(cos_ref, tm, LANES, dil)
            sin = _residue_major(sin_ref, tm, LANES, dil)
        gl = slice(g * DIL_GW, (g + 1) * DIL_GW)
        parts = []
        for p, gain_ref in enumerate((qn_ref, kn_ref, None)):
            c0 = o0 + p * DIL_WIDTH + g * DIL_GW
            y = _dot(hg, w_ref[:, c0:c0 + DIL_GW])
            if gain_ref is not None:
                y = _head_norm_rope(y, bd_ref[...], gain_ref[:, gl], cos, sin)
            parts.append(y.astype(o_ref.dtype))
        sub = tm // dil
        for r in range(dil):
            for p, y in enumerate(parts):
                c0 = (r * 3 + p) * DIL_GW
                o_ref[:, c0:c0 + DIL_GW] = y[r * sub:(r + 1) * sub, :]


def _const_spec(shape):
    nd = len(shape)
    return pl.BlockSpec(shape, lambda i: (0,) * nd)


def _even_in(x2, gain, w_in, cos, sin, sgu_norm, sgu_w, sgu_b, qn, kn, ones_bd):
    t = x2.shape[0]
    in_w = w_in.shape[1]
    sgb = jnp.repeat(sgu_b.T, HEAD_DIM, axis=1)
    qn_row = jnp.tile(qn * HEAD_DIM ** -0.5, DIL_WIDTH // HEAD_DIM).reshape(1, DIL_WIDTH)
    kn_row = jnp.tile(kn, DIL_WIDTH // HEAD_DIM).reshape(1, DIL_WIDTH)
    row = lambda n: pl.BlockSpec((TM, n), lambda i: (i, 0))
    return pl.pallas_call(
        _even_in_kernel,
        grid=(t // TM,),
        in_specs=[row(D_MODEL), _const_spec((1, D_MODEL)), _const_spec((D_MODEL, in_w)),
                  row(LANES), row(LANES), _const_spec((1, SGU_WIDTH)),
                  _const_spec((SGU_GROUPS, BLK, BLK)), _const_spec((BLK, SGU_WIDTH)),
                  _const_spec((1, DIL_WIDTH)), _const_spec((1, DIL_WIDTH)),
                  _const_spec(ones_bd.shape)],
        out_specs=[row(SGU_WIDTH)] + [pl.BlockSpec((TM // d, d * 3 * DIL_GW), lambda i: (i, 0))
                                      for _, d in DIL_GROUPS],
        out_shape=[jax.ShapeDtypeStruct((t, SGU_WIDTH), jnp.bfloat16)]
        + [jax.ShapeDtypeStruct((t // d, d * 3 * DIL_GW), jnp.bfloat16) for _, d in DIL_GROUPS],
        scratch_shapes=[pltpu.VMEM((ROW_CHUNKS * TM, LANES), jnp.float32)],
        compiler_params=_cparams(("arbitrary",)),
        name="even_in_proj",
    )(x2, gain.reshape(1, D_MODEL), w_in.astype(jnp.bfloat16), cos, sin,
      sgu_norm.reshape(1, SGU_WIDTH), sgu_w, sgb, qn_row, kn_row, ones_bd)


def _band_mask(max_rel, first):
    i = lax.broadcasted_iota(jnp.int32, (BLK, 2 * BLK), 0)
    j = lax.broadcasted_iota(jnp.int32, (BLK, 2 * BLK), 1)
    rel = BLK + i - j
    first_key = jnp.where(first, BLK, 0)
    return (rel >= 0) & (rel <= max_rel) & (j >= first_key)


def _head_attention(qh, kk, vv, mask, sink):
    s = jnp.where(mask, _dot_nt(qh, kk), NEG)
    m = jnp.max(s, axis=-1, keepdims=True)
    if sink is not None:
        m = jnp.maximum(m, sink)
    p = jnp.exp(s - m)
    denom = jnp.sum(p, axis=-1, keepdims=True)
    if sink is not None:
        denom = denom + jnp.exp(sink - m)
    o = _dot(p.astype(jnp.bfloat16), vv)
    return o / denom, m, denom


def _dil_attn_kernel(q_ref, kc_ref, kp_ref, vc_ref, vp_ref, o_ref, lse_ref, *, max_rel):
    first_tile = pl.program_id(2) == 0
    lane_lo = lax.broadcasted_iota(jnp.int32, (BLK, LANES), 1) < HEAD_DIM
    zero = jnp.zeros((), jnp.bfloat16)
    for j in range(q_ref.shape[0] // BLK):
        rows = slice(j * BLK, (j + 1) * BLK)
        if j == 0:
            kprev, vprev = kp_ref[...], vp_ref[...]
            mask = _band_mask(max_rel, first_tile)
        else:
            prows = slice((j - 1) * BLK, j * BLK)
            kprev, vprev = kc_ref[prows, :], vc_ref[prows, :]
            mask = _band_mask(max_rel, False)
        kk = jnp.concatenate([kprev, kc_ref[rows, :]], axis=0)
        vv = jnp.concatenate([vprev, vc_ref[rows, :]], axis=0)
        for p in range(DIL_GW // LANES):
            lanes = slice(p * LANES, (p + 1) * LANES)
            qp = q_ref[rows, lanes]
            oa, ma, da = _head_attention(jnp.where(lane_lo, qp, zero), kk[:, lanes],
                                         vv[:, lanes], mask, None)
            ob, mb, db = _head_attention(jnp.where(lane_lo, zero, qp), kk[:, lanes],
                                         vv[:, lanes], mask, None)
            o_ref[rows, lanes] = jnp.where(lane_lo, oa, ob).astype(o_ref.dtype)
            lse_ref[rows, lanes] = jnp.where(lane_lo, ma + jnp.log(da), mb + jnp.log(db))


def _dilated_group_attention(qkv, batch, seq, dil, window):
    sub = seq // dil
    a = qkv.reshape(batch, sub, dil * 3 * DIL_GW)
    tq = min(sub, DIL_TQ)
    nq = tq // BLK
    cur = lambda part: pl.BlockSpec((None, tq, DIL_GW), lambda b, r, i: (b, i, r * 3 + part))
    prev = lambda part: pl.BlockSpec(
        (None, BLK, DIL_GW), lambda b, r, i: (b, jnp.maximum(i * nq - 1, 0), r * 3 + part))
    out = pl.BlockSpec((None, tq, DIL_GW), lambda b, r, i: (b, i, r))
    o, lse = pl.pallas_call(
        functools.partial(_dil_attn_kernel, max_rel=window // dil),
        grid=(batch, dil, sub // tq),
        in_specs=[cur(0), cur(1), prev(1), cur(2), prev(2)],
        out_specs=[out, out],
        out_shape=[jax.ShapeDtypeStruct((batch, sub, dil * DIL_GW), jnp.bfloat16),
                   jax.ShapeDtypeStruct((batch, sub, dil * DIL_GW), jnp.float32)],
        compiler_params=_cparams(("arbitrary",) * 3),
        name=f"dilated_attn_d{dil}",
    )(a, a, a, a, a)
    rows = batch * sub
    return o.reshape(rows, dil * DIL_GW), lse.reshape(rows, dil * DIL_GW)


def _swa_kernel(sink_ref, q_ref, kvc_ref, kvp_ref, o_ref):
    first_tile = pl.program_id(1) == 0
    lane_lo = lax.broadcasted_iota(jnp.int32, (BLK, LANES), 1) < HEAD_DIM
    zero = jnp.zeros((), jnp.bfloat16)
    rep = SWA_Q_HEADS // SWA_KV_HEADS
    kw = SWA_KV_HEADS * LANES
    for j in range(TQ // BLK):
        rows = slice(j * BLK, (j + 1) * BLK)
        if j == 0:
            kvprev = kvp_ref[...]
            mask = _band_mask(SWA_WINDOW - 1, first_tile)
        else:
            kvprev = kvc_ref[(j - 1) * BLK:j * BLK, :]
            mask = _band_mask(SWA_WINDOW - 1, False)
        kv = jnp.concatenate([kvprev, kvc_ref[rows, :]], axis=0)
        for g in range(SWA_KV_HEADS):
            kk = kv[:, g * LANES:(g + 1) * LANES]
            vv = kv[:, kw + g * LANES:kw + (g + 1) * LANES]
            for p in range(rep // 2):
                h0 = g * rep + 2 * p
                lanes = slice(h0 * HEAD_DIM, (h0 + 2) * HEAD_DIM)
                qp = q_ref[rows, lanes]
                oa, _, _ = _head_attention(jnp.where(lane_lo, qp, zero), kk, vv, mask,
                                           sink_ref[h0])
                ob, _, _ = _head_attention(jnp.where(lane_lo, zero, qp), kk, vv, mask,
                                           sink_ref[h0 + 1])
                o_ref[rows, lanes] = jnp.where(lane_lo, oa, ob).astype(o_ref.dtype)


def _swa_attention(q, kv, sinks, batch, seq):
    q3 = q.reshape(batch, seq, q.shape[1])
    kv3 = kv.reshape(batch, seq, kv.shape[1])
    nq = TQ // BLK
    o = pl.pallas_call(
        _swa_kernel,
        grid_spec=pltpu.PrefetchScalarGridSpec(
            num_scalar_prefetch=1,
            grid=(batch, seq // TQ),
            in_specs=[pl.BlockSpec((None, TQ, q.shape[1]), lambda b, i, s: (b, i, 0)),
                      pl.BlockSpec((None, TQ, kv.shape[1]), lambda b, i, s: (b, i, 0)),
                      pl.BlockSpec((None, BLK, kv.shape[1]),
                                   lambda b, i, s: (b, jnp.maximum(i * nq - 1, 0), 0))],
            out_specs=pl.BlockSpec((None, TQ, q.shape[1]), lambda b, i, s: (b, i, 0))),
        out_shape=jax.ShapeDtypeStruct(q3.shape, jnp.bfloat16),
        compiler_params=_cparams(("arbitrary",) * 2),
        name="swa_attn",
    )(sinks.astype(jnp.float32), q3, kv3, kv3)
    return o.reshape(batch * seq, q.shape[1])


def _odd_in_kernel(x_ref, g_ref, w_ref, cos_ref, sin_ref, qn_ref, kn_ref, bd_ref, q_ref, kv_ref):
    hn = _rms(x_ref[...], g_ref[...]).astype(jnp.bfloat16)
    qw = SWA_Q_HEADS * HEAD_DIM
    kw = SWA_KV_HEADS * LANES
    cos = cos_ref[...]
    sin = sin_ref[...]
    q = _dot(hn, w_ref[:, 0:qw])
    q_ref[...] = _head_norm_rope(q, bd_ref[...], qn_ref[...], cos, sin).astype(q_ref.dtype)
    k = _dot(hn, w_ref[:, qw:qw + kw])
    kv_ref[:, 0:kw] = _head_norm_rope(k, bd_ref[...], kn_ref[...], cos, sin).astype(kv_ref.dtype)
    kv_ref[:, kw:2 * kw] = _dot(hn, w_ref[:, qw + kw:qw + 2 * kw]).astype(kv_ref.dtype)


def _odd_in(x, gain, w_in, cos, sin, qn, kn, ones_bd):
    t = x.shape[0]
    qw = SWA_Q_HEADS * HEAD_DIM
    kvw = SWA_KV_HEADS * HEAD_DIM
    dup = lambda w: jnp.concatenate(
        [w[:, h * HEAD_DIM:(h + 1) * HEAD_DIM] for h in range(SWA_KV_HEADS) for _ in range(2)],
        axis=1)
    w_all = jnp.concatenate([w_in[:, :qw], dup(w_in[:, qw:qw + kvw]),
                             dup(w_in[:, qw + kvw:qw + 2 * kvw])], axis=1).astype(jnp.bfloat16)
    kw = SWA_KV_HEADS * LANES
    qn_row = jnp.tile(qn * HEAD_DIM ** -0.5, qw // HEAD_DIM).reshape(1, qw)
    kn_row = jnp.tile(kn, kw // HEAD_DIM).reshape(1, kw)
    row = lambda n: pl.BlockSpec((TM, n), lambda i: (i, 0))
    return pl.pallas_call(
        _odd_in_kernel,
        grid=(t // TM,),
        in_specs=[row(D_MODEL), _const_spec((1, D_MODEL)), _const_spec(w_all.shape),
                  row(LANES), row(LANES), _const_spec((1, qw)), _const_spec((1, kw)),
                  _const_spec(ones_bd.shape)],
        out_specs=[row(qw), row(2 * kw)],
        out_shape=[jax.ShapeDtypeStruct((t, qw), jnp.bfloat16),
                   jax.ShapeDtypeStruct((t, 2 * kw), jnp.bfloat16)],
        compiler_params=_cparams(("arbitrary",)),
        name="odd_in_proj",
    )(x, gain.reshape(1, D_MODEL), w_all, cos, sin, qn_row, kn_row, ones_bd)


def _route(hn, wr_ref, br_ref, carry_ref, idx_ref, gate_ref, cnt_ref):
    tm = hn.shape[0]
    h_hi = hn.astype(jnp.bfloat16)
    h_lo = (hn - h_hi.astype(jnp.float32)).astype(jnp.bfloat16)
    hi_pass = _dot(h_hi, wr_ref[...])
    logits = (hi_pass[:, :ROUTER_COLS] + _dot(h_lo, wr_ref[:, :ROUTER_COLS])
              + hi_pass[:, ROUTER_COLS:] + br_ref[...])
    lt = logits.T
    rowf = lax.broadcasted_iota(jnp.int32, (SUBLANES, tm), 0).astype(jnp.float32)

    def first_argmax(x):
        m = jnp.max(x, axis=0, keepdims=True)
        idx = jnp.min(jnp.where(x == m, rowf, float(SUBLANES)), axis=0, keepdims=True)
        return m, idx

    lg = jnp.where(rowf < N_GROUPS, lt[0:SUBLANES], NEG)
    mg, gi = first_argmax(lg)
    p_top = 1.0 / jnp.sum(jnp.exp(lg - mg), axis=0, keepdims=True)
    le = lt[ROUTER_E0:ROUTER_E0 + EPG]
    for g in range(1, N_GROUPS):
        le = jnp.where(gi == float(g), lt[ROUTER_E0 + g * EPG:ROUTER_E0 + (g + 1) * EPG], le)
    m1, i1 = first_argmax(le)
    m2, i2 = first_argmax(jnp.where(rowf == i1, NEG, le))
    e2 = jnp.exp(m2 - m1)
    gate1 = p_top / (1.0 + e2)
    gate2 = p_top * e2 / (1.0 + e2)
    eid1 = gi * EPG + i1
    eid2 = gi * EPG + i2

    erow = lax.broadcasted_iota(jnp.int32, (N_EXPERTS, tm), 0).astype(jnp.float32)
    oh1 = erow == eid1
    oh2 = erow == eid2
    member = jnp.where(oh1 | oh2, 1.0, 0.0)
    s_idx = lax.broadcasted_iota(jnp.int32, (tm, tm), 0)
    t_idx = lax.broadcasted_iota(jnp.int32, (tm, tm), 1)
    upper = jnp.where(s_idx < t_idx, 1.0, 0.0).astype(jnp.bfloat16)
    before = carry_ref[:, 0:1] + _dot(member.astype(jnp.bfloat16), upper)
    rank1 = jnp.sum(jnp.where(oh1, before, 0.0), axis=0, keepdims=True)
    rank2 = jnp.sum(jnp.where(oh2, before, 0.0), axis=0, keepdims=True)
    carry_ref[...] = carry_ref[...] + jnp.sum(member, axis=1, keepdims=True)
    cnt_ref[...] = carry_ref[...].astype(jnp.int32)

    idx = jnp.where(rowf == 0.0, eid1, jnp.where(rowf == 1.0, eid2,
          jnp.where(rowf == 2.0, rank1, jnp.where(rowf == 3.0, rank2, 0.0))))
    idx_ref[...] = idx.astype(jnp.int32)
    gates = jnp.where(rowf == 0.0, gate1, jnp.where(rowf == 1.0, gate2, 0.0))
    padded = jnp.concatenate([gates] + [jnp.zeros_like(gates)] * (LANES // SUBLANES - 1), axis=0)
    gate_ref[...] = padded.T[:, :SUBLANES]


def _store_row_tiles(ref, x):
    n = x.shape[0]
    for c in range(ROW_CHUNKS):
        ref[pl.ds(c, n, stride=ROW_CHUNKS), :] = x[:, c * LANES:(c + 1) * LANES].astype(ref.dtype)


def _load_row_tiles(ref, n):
    return jnp.concatenate([ref[pl.ds(c, n, stride=ROW_CHUNKS), :] for c in range(ROW_CHUNKS)],
                           axis=1)


def _post_proj(y, x_ref, gf_ref, wr_ref, br_ref, carry_ref,
               h_ref, hn_ref, idx_ref, gate_ref, cnt_ref):
    @pl.when(pl.program_id(0) == 0)
    def _():
        carry_ref[...] = jnp.zeros_like(carry_ref)

    h = x_ref[...] + y
    h_ref[...] = h
    hn = _rms(h, gf_ref[...])
    _store_row_tiles(hn_ref, hn)
    _route(hn, wr_ref, br_ref, carry_ref, idx_ref, gate_ref, cnt_ref)


def _token_major(ref, dil, scr):
    if dil == 1:
        return ref[...].astype(jnp.float32)
    sub = ref.shape[0]
    n = sub * dil
    chunks = DIL_GW // LANES
    for r in range(dil):
        for c in range(chunks):
            l0 = r * DIL_GW + c * LANES
            scr[pl.ds(c * n + r, sub, stride=dil), :] = ref[:, l0:l0 + LANES].astype(jnp.float32)
    return jnp.concatenate([scr[c * n:(c + 1) * n, :] for c in range(chunks)], axis=1)


def _even_out_kernel(a_ref, o0_ref, o1_ref, o2_ref, l0_ref, l1_ref, l2_ref, w_ref, x_ref, gf_ref,
                     wr_ref, br_ref, h_ref, hn_ref, idx_ref, gate_ref, cnt_ref,
                     carry_ref, so1, so2, sl1, sl2):
    dils = [d for _, d in DIL_GROUPS]
    o0, o1, o2 = (_token_major(r, d, s) for r, d, s in zip((o0_ref, o1_ref, o2_ref), dils,
                                                            (None, so1, so2)))
    l0, l1, l2 = (_token_major(r, d, s) for r, d, s in zip((l0_ref, l1_ref, l2_ref), dils,
                                                            (None, sl1, sl2)))
    m = jnp.maximum(jnp.maximum(l0, l1), l2)
    e0, e1, e2 = jnp.exp(l0 - m), jnp.exp(l1 - m), jnp.exp(l2 - m)
    b = (e0 * o0 + e1 * o1 + e2 * o2) / (e0 + e1 + e2)
    y = _dot(a_ref[...], w_ref[0:SGU_WIDTH, :]) + _dot(b.astype(jnp.bfloat16),
                                                       w_ref[SGU_WIDTH:, :])
    _post_proj(y, x_ref, gf_ref, wr_ref, br_ref, carry_ref,
               h_ref, hn_ref, idx_ref, gate_ref, cnt_ref)


def _odd_out_kernel(o_ref, w_ref, x_ref, gf_ref, wr_ref, br_ref,
                    h_ref, hn_ref, idx_ref, gate_ref, cnt_ref, carry_ref):
    y = _dot(o_ref[...], w_ref[...])
    _post_proj(y, x_ref, gf_ref, wr_ref, br_ref, carry_ref,
               h_ref, hn_ref, idx_ref, gate_ref, cnt_ref)


def _router_weights(w_rg, b_rg, w_re, b_re):
    d = w_rg.shape[0]
    w = jnp.zeros((d, ROUTER_COLS), jnp.float32)
    w = w.at[:, 0:N_GROUPS].set(w_rg)
    w = w.at[:, ROUTER_E0:ROUTER_E0 + N_EXPERTS].set(
        jnp.transpose(w_re, (1, 0, 2)).reshape(d, N_EXPERTS))
    b = jnp.zeros((1, ROUTER_COLS), jnp.float32)
    b = b.at[0, 0:N_GROUPS].set(b_rg)
    b = b.at[0, ROUTER_E0:ROUTER_E0 + N_EXPERTS].set(b_re.reshape(N_EXPERTS))
    w_hi = w.astype(jnp.bfloat16)
    w_lo = (w - w_hi.astype(jnp.float32)).astype(jnp.bfloat16)
    return jnp.concatenate([w_hi, w_lo], axis=1), b


def _out_proj(kernel, acts, w_out, x2, gain_ffn, router, extra_scratch=()):
    t = x2.shape[0]
    w_router, b = router
    row = lambda n: pl.BlockSpec((TM, n), lambda i: (i, 0))
    colblk = pl.BlockSpec((SUBLANES, TM), lambda i: (0, i))
    return pl.pallas_call(
        kernel,
        grid=(t // TM,),
        in_specs=[pl.BlockSpec((TM * a.shape[0] // t, a.shape[1]), lambda i: (i, 0)) for a in acts]
        + [_const_spec(w_out.shape), row(D_MODEL), _const_spec((1, D_MODEL)),
           _const_spec(w_router.shape), _const_spec(b.shape)],
        out_specs=[row(D_MODEL), pl.BlockSpec((TM * ROW_CHUNKS, LANES), lambda i: (i, 0)),
                   colblk, row(SUBLANES),
                   _const_spec((N_EXPERTS, LANES))],
        out_shape=[jax.ShapeDtypeStruct((t, D_MODEL), jnp.float32),
                   jax.ShapeDtypeStruct((t * ROW_CHUNKS, LANES), ROW_DTYPE),
                   jax.ShapeDtypeStruct((SUBLANES, t), jnp.int32),
                   jax.ShapeDtypeStruct((t, SUBLANES), jnp.float32),
                   jax.ShapeDtypeStruct((N_EXPERTS, LANES), jnp.int32)],
        scratch_shapes=[pltpu.VMEM((N_EXPERTS, LANES), jnp.float32)] + list(extra_scratch),
        compiler_params=_cparams(("arbitrary",)),
        name=kernel.__name__.strip("_"),
    )(*acts, w_out.astype(jnp.bfloat16), x2, gain_ffn.reshape(1, D_MODEL), w_router, b)


ROWS_PER_TILE = TOP_K * TM
DISPATCH_TM = 1024
DISPATCH_SLOTS = 3
ISSUE_UNROLL = 8
FFN_SLOTS = 3


def _row_tile(ref, row):
    return ref.at[pl.ds(pl.multiple_of(row * ROW_CHUNKS, ROW_CHUNKS), ROW_CHUNKS), :]


def _wait_rows(copy, n):
    def body(_, c):
        copy.wait()
        return c

    lax.fori_loop(0, n, body, 0, unroll=16)


def _dispatch_kernel(pstart_ref, pend_ref, dest0_ref, dest1_ref, hn_ref, xs_ref, zero_ref, hbuf,
                     ld_sems, row_sems, zsem):
    blk = MOE_BM * ROW_CHUNKS
    i = pl.program_id(0)
    n = pl.num_programs(0)
    tile = DISPATCH_TM * ROW_CHUNKS

    def load(b):
        s = lax.rem(b, DISPATCH_SLOTS)
        return pltpu.make_async_copy(hn_ref.at[pl.ds(pl.multiple_of(b * tile, tile), tile), :],
                                     hbuf.at[s], ld_sems.at[s])

    def zero_block(row0):
        return pltpu.make_async_copy(
            zero_ref, xs_ref.at[pl.ds(pl.multiple_of(row0 * ROW_CHUNKS, blk), blk), :], zsem)

    @pl.when(pl.program_id(0) == 0)
    def _():
        zero_ref[...] = jnp.zeros_like(zero_ref)

        def start_last(c, _):
            @pl.when(pend_ref[c] > pstart_ref[c])
            def _():
                zero_block(pend_ref[c] - MOE_BM).start()
            return 0

        def wait_last(c, _):
            @pl.when(pend_ref[c] > pstart_ref[c])
            def _():
                zero_block(pend_ref[c] - MOE_BM).wait()
            return 0

        lax.fori_loop(0, N_EXPERTS, start_last, 0)
        lax.fori_loop(0, N_EXPERTS, wait_last, 0)
        first_unused = lax.div(pend_ref[N_EXPERTS - 1], MOE_BM)
        n_blocks = xs_ref.shape[0] // blk

        def start_tail(b, _):
            zero_block(b * MOE_BM).start()
            return 0

        def wait_tail(b, _):
            zero_block(b * MOE_BM).wait()
            return 0

        lax.fori_loop(first_unused, n_blocks, start_tail, 0)
        lax.fori_loop(first_unused, n_blocks, wait_tail, 0)

        for b in range(DISPATCH_SLOTS - 1):
            @pl.when(b < n)
            def _():
                load(i + b).start()

    dest_refs = (dest0_ref, dest1_ref)

    def row_copy(slot, src, dst):
        return pltpu.make_async_copy(_row_tile(hbuf.at[slot], src), _row_tile(xs_ref, dst),
                                     row_sems.at[slot])

    slot = lax.rem(i, DISPATCH_SLOTS)
    load(i).wait()

    def issue(b, c):
        for j in range(ISSUE_UNROLL):
            t = b * ISSUE_UNROLL + j
            for k, d_ref in enumerate(dest_refs):
                row_copy(slot, t, d_ref[0, t]).start(priority=k)
        return c

    lax.fori_loop(0, DISPATCH_TM // ISSUE_UNROLL, issue, 0)

    @pl.when(i > 0)
    def _():
        _wait_rows(row_copy(lax.rem(i - 1, DISPATCH_SLOTS), 0, 0), TOP_K * DISPATCH_TM)

    @pl.when(i + (DISPATCH_SLOTS - 1) < n)
    def _():
        load(i + (DISPATCH_SLOTS - 1)).start()

    @pl.when(i == n - 1)
    def _():
        _wait_rows(row_copy(slot, 0, 0), TOP_K * DISPATCH_TM)


def _dispatch(hn, dest, pstart, pend, rows):
    t = hn.shape[0] // ROW_CHUNKS
    n = t // DISPATCH_TM
    dspec = pl.BlockSpec((None, 1, DISPATCH_TM), lambda i, ps, pe: (i, 0, 0),
                         memory_space=pltpu.SMEM)
    return pl.pallas_call(
        _dispatch_kernel,
        grid_spec=pltpu.PrefetchScalarGridSpec(
            num_scalar_prefetch=2,
            grid=(n,),
            in_specs=[dspec] * TOP_K + [pl.BlockSpec(memory_space=pl.ANY)],
            out_specs=pl.BlockSpec(memory_space=pl.ANY),
            scratch_shapes=[pltpu.VMEM((MOE_BM * ROW_CHUNKS, LANES), ROW_DTYPE),
                            pltpu.VMEM((DISPATCH_SLOTS, DISPATCH_TM * ROW_CHUNKS, LANES),
                                       ROW_DTYPE),
                            pltpu.SemaphoreType.DMA((DISPATCH_SLOTS,)),
                            pltpu.SemaphoreType.DMA((DISPATCH_SLOTS,)),
                            pltpu.SemaphoreType.DMA(())]),
        out_shape=jax.ShapeDtypeStruct((rows * ROW_CHUNKS, LANES), ROW_DTYPE),
        compiler_params=_cparams(("arbitrary",)),
        name="moe_dispatch",
    )(pstart, pend, *[dest[k].reshape(n, 1, DISPATCH_TM) for k in range(TOP_K)], hn)


def _expert_kernel(blk_e_ref, wslot_ref, next_e_ref, nused_ref, xs_ref, wg_ref, wu_ref, wd_ref,
                   ys_ref, wg_s, wu_s, wd_s, wg_buf, wu_buf, wd_buf, xbuf, ybuf, zbuf,
                   w_sems, in_sems, out_sems, zsem, *, layer):
    i = pl.program_id(0)
    n_steps = pl.num_programs(0)
    nu = nused_ref[0]
    blk = MOE_BM * ROW_CHUNKS

    def weight_copies(e, s):
        return [pltpu.make_async_copy(w_ref.at[layer, e], buf.at[s], w_sems.at[s])
                for w_ref, buf in ((wg_ref, wg_buf), (wu_ref, wu_buf), (wd_ref, wd_buf))]

    def block(ref, b):
        return ref.at[pl.ds(pl.multiple_of(b * blk, blk), blk), :]

    def load(b):
        s = lax.rem(b, FFN_SLOTS)
        return pltpu.make_async_copy(block(xs_ref, b), xbuf.at[s], in_sems.at[s])

    def store(b):
        s = lax.rem(b, FFN_SLOTS)
        return pltpu.make_async_copy(ybuf.at[s], block(ys_ref, b), out_sems.at[s])

    @pl.when(i == 0)
    def _():
        for c in weight_copies(blk_e_ref[0], wslot_ref[blk_e_ref[0]]):
            c.start()
        for b in range(FFN_SLOTS - 1):
            @pl.when(b < nu)
            def _():
                load(i + b).start()

    @pl.when(i + (FFN_SLOTS - 1) < nu)
    def _():
        load(i + (FFN_SLOTS - 1)).start()

    @pl.when(i < nu)
    def _():
        @pl.when((i == 0) | (blk_e_ref[i] != blk_e_ref[jnp.maximum(i - 1, 0)]))
        def _():
            e = blk_e_ref[i]
            ws = wslot_ref[e]
            for c in weight_copies(e, ws):
                c.wait()

            @pl.when(next_e_ref[e] >= 0)
            def _():
                for c in weight_copies(next_e_ref[e], 1 - ws):
                    c.start()

            wg_s[...] = wg_buf[ws].astype(jnp.bfloat16)
            wu_s[...] = wu_buf[ws].astype(jnp.bfloat16)
            wd_s[...] = wd_buf[ws].astype(jnp.bfloat16)

        slot = lax.rem(i, FFN_SLOTS)
        load(i).wait()

        @pl.when(i >= FFN_SLOTS)
        def _():
            store(i - FFN_SLOTS).wait()

        x = _load_row_tiles(xbuf.at[slot], MOE_BM).astype(jnp.bfloat16)
        hid = jax.nn.silu(_dot(x, wg_s[...])) * _dot(x, wu_s[...])
        _store_row_tiles(ybuf.at[slot], _dot(hid.astype(jnp.bfloat16), wd_s[...]))
        store(i).start()

    @pl.when(i >= nu)
    def _():
        @pl.when(i == nu)
        def _():
            zbuf[...] = jnp.zeros_like(zbuf)

        fill = pltpu.make_async_copy(zbuf, block(ys_ref, i), zsem)
        fill.start()
        fill.wait()

    @pl.when(i == n_steps - 1)
    def _():
        for d in range(FFN_SLOTS):
            b = nu - 1 - d

            @pl.when(b >= 0)
            def _():
                store(b).wait()


def _expert_ffn(xs, blk_e, wslot, next_e, nused, layer, w_gate, w_up, w_down):
    rows = xs.shape[0] // ROW_CHUNKS
    blk = MOE_BM * ROW_CHUNKS
    ring = pltpu.VMEM((FFN_SLOTS, blk, LANES), ROW_DTYPE)
    any_spec = pl.BlockSpec(memory_space=pl.ANY)
    return pl.pallas_call(
        functools.partial(_expert_kernel, layer=layer),
        grid_spec=pltpu.PrefetchScalarGridSpec(
            num_scalar_prefetch=4,
            grid=(rows // MOE_BM,),
            in_specs=[any_spec] * 4,
            out_specs=any_spec,
            scratch_shapes=[pltpu.VMEM((D_MODEL, EXPERT_FF), jnp.bfloat16),
                            pltpu.VMEM((D_MODEL, EXPERT_FF), jnp.bfloat16),
                            pltpu.VMEM((EXPERT_FF, D_MODEL), jnp.bfloat16),
                            pltpu.VMEM((2, D_MODEL, EXPERT_FF), jnp.float32),
                            pltpu.VMEM((2, D_MODEL, EXPERT_FF), jnp.float32),
                            pltpu.VMEM((2, EXPERT_FF, D_MODEL), jnp.float32),
                            ring, ring, pltpu.VMEM((blk, LANES), ROW_DTYPE),
                            pltpu.SemaphoreType.DMA((2,)),
                            pltpu.SemaphoreType.DMA((FFN_SLOTS,)),
                            pltpu.SemaphoreType.DMA((FFN_SLOTS,)),
                            pltpu.SemaphoreType.DMA(())]),
        out_shape=jax.ShapeDtypeStruct((rows * ROW_CHUNKS, LANES), ROW_DTYPE),
        compiler_params=_cparams(("arbitrary",)),
        name="moe_expert_ffn",
    )(blk_e, wslot, next_e, nused, xs, w_gate, w_up, w_down)


def _combine_kernel(d0_ref, d1_ref, d0_next_ref, d1_next_ref, h_ref, gate_ref, ys_ref, out_ref,
                    buf_ref, sems):
    i = pl.program_id(0)
    n = pl.num_programs(0)

    def row_copy(src, slot, k, t):
        return pltpu.make_async_copy(_row_tile(ys_ref, src), _row_tile(buf_ref.at[slot, k], t),
                                     sems.at[slot])

    def issue_tile(d_refs, slot):
        def issue(b, c):
            for j in range(ISSUE_UNROLL):
                t = b * ISSUE_UNROLL + j
                for k, d_ref in enumerate(d_refs):
                    row_copy(d_ref[0, t], slot, k, t).start(priority=k)
            return c

        lax.fori_loop(0, TM // ISSUE_UNROLL, issue, 0)

    slot = lax.rem(i, 2)

    @pl.when(i == 0)
    def _():
        issue_tile((d0_ref, d1_ref), 0)

    @pl.when(i + 1 < n)
    def _():
        issue_tile((d0_next_ref, d1_next_ref), 1 - slot)

    _wait_rows(row_copy(0, slot, 0, 0), ROWS_PER_TILE)
    g = gate_ref[...]
    r1 = _load_row_tiles(buf_ref.at[slot, 0], TM)
    r2 = _load_row_tiles(buf_ref.at[slot, 1], TM)
    out_ref[...] = h_ref[...] + (r1 * g[:, 0:1] + r2 * g[:, 1:2])


def _combine(h, gate_cols, ys, dest):
    t = h.shape[0]
    n = t // TM
    dest_tiles = [dest[k].reshape(n, 1, TM) for k in range(TOP_K)]
    dspec = lambda f: pl.BlockSpec((None, 1, TM), f, memory_space=pltpu.SMEM)
    cur = dspec(lambda i: (i, 0, 0))
    nxt = dspec(lambda i: (jnp.minimum(i + 1, n - 1), 0, 0))
    return pl.pallas_call(
        _combine_kernel,
        grid=(n,),
        in_specs=[cur] * TOP_K + [nxt] * TOP_K + [
                  pl.BlockSpec((TM, D_MODEL), lambda i: (i, 0)),
                  pl.BlockSpec((TM, SUBLANES), lambda i: (i, 0)),
                  pl.BlockSpec(memory_space=pl.ANY)],
        out_specs=pl.BlockSpec((TM, D_MODEL), lambda i: (i, 0)),
        scratch_shapes=[pltpu.VMEM((2, TOP_K, TM * ROW_CHUNKS, LANES), ROW_DTYPE),
                        pltpu.SemaphoreType.DMA((2,))],
        out_shape=jax.ShapeDtypeStruct((t, D_MODEL), jnp.float32),
        compiler_params=_cparams(("arbitrary",)),
        name="moe_combine",
    )(*dest_tiles, *dest_tiles, h, gate_cols, ys)


def _moe(h, hn, idx, gates, counts, layer, w_gate, w_up, w_down):
    t = h.shape[0]
    rows = t * TOP_K + N_EXPERTS * MOE_BM
    nblk = rows // MOE_BM
    cnt = counts[:, 0]
    padded = (cnt + MOE_BM - 1) // MOE_BM * MOE_BM
    pend = jnp.cumsum(padded).astype(jnp.int32)
    pstart = pend - padded
    blk_row0 = jnp.arange(nblk, dtype=jnp.int32) * MOE_BM
    blk_e = jnp.minimum(jnp.sum(pend[None, :] <= blk_row0[:, None], axis=1),
                        N_EXPERTS - 1).astype(jnp.int32)
    nused = (pend[-1:] // MOE_BM).astype(jnp.int32)
    experts = jnp.arange(N_EXPERTS, dtype=jnp.int32)
    nonempty = cnt > 0
    wslot = ((jnp.cumsum(nonempty) - nonempty) % 2).astype(jnp.int32)
    later = jnp.where(nonempty[None, :] & (experts[None, :] > experts[:, None]),
                      experts[None, :], N_EXPERTS)
    next_e = jnp.min(later, axis=1)
    next_e = jnp.where(next_e == N_EXPERTS, -1, next_e).astype(jnp.int32)
    eid, rank = idx[0:TOP_K], idx[TOP_K:2 * TOP_K]
    seg_start = jnp.sum(jnp.where(eid[..., None] == jnp.arange(N_EXPERTS), pstart, 0), axis=-1)
    dest = seg_start + rank
    xs = _dispatch(hn, dest, pstart, pend, rows)
    ys = _expert_ffn(xs, blk_e, wslot, next_e, nused, layer, w_gate, w_up, w_down)
    return _combine(h, gates, ys, dest)


def kernel(x, positions, norm_mix, norm_ffn, w_in_even, w_out_even, sgu_norm, sgu_w, sgu_b,
           qn_dil, kn_dil, w_in_odd, w_out_odd, qn_swa, kn_swa, sinks,
           w_router_g, b_router_g, w_router_e, b_router_e, w_gate, w_up, w_down):
    batch, seq, d = x.shape
    t = batch * seq
    depth = norm_mix.shape[0]
    cos, sin = _rope_tables(positions)
    bd = 2 * LANES
    ones_bd = (jnp.arange(bd)[:, None] // HEAD_DIM == jnp.arange(bd)[None, :] // HEAD_DIM
               ).astype(jnp.bfloat16)
    h = x.reshape(t, d)
    for layer in range(depth):
        i = layer // 2
        router = _router_weights(w_router_g[layer], b_router_g[layer], w_router_e[layer],
                                 b_router_e[layer])
        if layer % 2 == 0:
            a, *qkv = _even_in(h, norm_mix[layer], w_in_even[i], cos, sin, sgu_norm[i],
                               sgu_w[i], sgu_b[i], qn_dil[i], kn_dil[i], ones_bd)
            outs, lses = [], []
            for g, (window, dil) in enumerate(DIL_GROUPS):
                o, lse = _dilated_group_attention(qkv[g], batch, seq, dil, window)
                outs.append(o)
                lses.append(lse)
            stage = [pltpu.VMEM((TM * DIL_GW // LANES, LANES), jnp.float32)] * 4
            res = _out_proj(_even_out_kernel, [a] + outs + lses, w_out_even[i], h,
                            norm_ffn[layer], router, stage)
        else:
            q, kv = _odd_in(h, norm_mix[layer], w_in_odd[i], cos, sin, qn_swa[i], kn_swa[i],
                            ones_bd)
            o = _swa_attention(q, kv, sinks[i], batch, seq)
            res = _out_proj(_odd_out_kernel, [o], w_out_odd[i], h, norm_ffn[layer], router)
        h_mid, hn, idx, gates, counts = res
        h = _moe(h_mid, hn, idx, gates, counts, layer, w_gate, w_up, w_down)
    return h.reshape(batch, seq, d)
```

```python
import functools

import jax
import jax.numpy as jnp
from jax import lax
from jax.experimental import pallas as pl
from jax.experimental.pallas import tpu as pltpu

D_MODEL = 1024
HEAD_DIM = 64
BLK = 128
ROPE_THETA = 500000.0
ROT_DIM = HEAD_DIM // 4
HALF_ROT = ROT_DIM // 2
EPS = 1e-6
SGU_GROUPS = 8
SGU_WIDTH = SGU_GROUPS * HEAD_DIM
DIL_GROUPS = ((128, 1), (512, 4), (2048, 16))
DIL_HPG = 4
DIL_GW = DIL_HPG * HEAD_DIM
DIL_WIDTH = DIL_GW * len(DIL_GROUPS)
SWA_Q_HEADS = 16
SWA_KV_HEADS = 2
SWA_WINDOW = 128
N_GROUPS = 4
EPG = 8
N_EXPERTS = N_GROUPS * EPG
EXPERT_FF = 256
TOP_K = 2

LANES = 128
SUBLANES = 8
VMEM_LIMIT_BYTES = 48 * 1024 * 1024

TM = 512
TQ = 512
DIL_TQ = 1024
MOE_BM = 256
NEG = -1e30

ROW_CHUNKS = D_MODEL // LANES
ROW_DTYPE = jnp.float32
ROUTER_COLS = LANES
ROUTER_E0 = SUBLANES


def _cparams(sem):
    return pltpu.CompilerParams(dimension_semantics=sem, vmem_limit_bytes=VMEM_LIMIT_BYTES)


def _rms(x, gain_row):
    return x * lax.rsqrt(jnp.mean(x * x, axis=-1, keepdims=True) + EPS) * gain_row


def _dot(a, b):
    return jnp.dot(a, b, preferred_element_type=jnp.float32)


def _dot_nt(a, b):
    return lax.dot_general(a, b, (((1,), (1,)), ((), ())), preferred_element_type=jnp.float32)


def _rope_kernel(pos_ref, inv_ref, cos_ref, sin_ref):
    ang = inv_ref[...] * pos_ref[...].astype(jnp.float32)
    c = jnp.cos(ang)
    s = jnp.sin(ang)
    rest = HEAD_DIM // SUBLANES - 2
    head_c = [c, c] + [jnp.ones_like(c)] * rest
    head_s = [-s, s] + [jnp.zeros_like(s)] * rest
    cos_ref[...] = jnp.concatenate(head_c * 2, axis=0).T
    sin_ref[...] = jnp.concatenate(head_s * 2, axis=0).T


def _rope_tables(positions):
    t = positions.size
    inv = ROPE_THETA ** (-jnp.arange(0, ROT_DIM, 2, dtype=jnp.float32) / ROT_DIM)
    tm = 2048
    return pl.pallas_call(
        _rope_kernel,
        grid=(t // tm,),
        in_specs=[pl.BlockSpec((1, tm), lambda i: (0, i)),
                  pl.BlockSpec((HALF_ROT, 1), lambda i: (0, 0))],
        out_specs=[pl.BlockSpec((tm, LANES), lambda i: (i, 0)),
                   pl.BlockSpec((tm, LANES), lambda i: (i, 0))],
        out_shape=[jax.ShapeDtypeStruct((t, LANES), jnp.float32)] * 2,
        compiler_params=_cparams(("arbitrary",)),
        name="rope_tables",
    )(positions.reshape(1, t), inv.reshape(HALF_ROT, 1))


def _head_norm_rope(x, ones_bd, gain_row, cos, sin):
    tm, w = x.shape
    sq = (x * x).astype(jnp.bfloat16)
    outs = []
    bd = ones_bd.shape[0]
    for c in range(w // bd):
        sl = slice(c * bd, (c + 1) * bd)
        ss = _dot(sq[:, sl], ones_bd)
        xn = x[:, sl] * lax.rsqrt(ss * (1.0 / HEAD_DIM) + EPS) * gain_row[:, sl]
        for j in range(bd // LANES):
            xj = xn[:, j * LANES:(j + 1) * LANES]
            lane = lax.broadcasted_iota(jnp.int32, xj.shape, 1) % HEAD_DIM
            rot = jnp.where(lane < HALF_ROT,
                            pltpu.roll(xj, LANES - HALF_ROT, axis=1),
                            pltpu.roll(xj, HALF_ROT, axis=1))
            outs.append(xj * cos + rot * sin)
    return jnp.concatenate(outs, axis=1)


def _residue_major(ref, n, w, dil):
    sub = n // dil
    return jnp.concatenate(
        [jnp.concatenate([ref[pl.ds(c * n + r, sub, stride=dil), :] for c in range(w // LANES)],
                         axis=1) for r in range(dil)], axis=0)


def _even_in_kernel(x_ref, g_ref, w_ref, cos_ref, sin_ref, sgn_ref, sgw_ref, sgb_ref,
                    qn_ref, kn_ref, bd_ref, a_ref, q0_ref, q1_ref, q2_ref, hn_scr):
    x = x_ref[...]
    tm = x.shape[0]
    hn32 = _rms(x, g_ref[...])
    for c in range(ROW_CHUNKS):
        hn_scr[c * tm:(c + 1) * tm, :] = hn32[:, c * LANES:(c + 1) * LANES]
    hn = hn32.astype(jnp.bfloat16)

    u = _dot(hn, w_ref[:, 0:SGU_WIDTH])
    v = _dot(hn, w_ref[:, SGU_WIDTH:2 * SGU_WIDTH])
    gu = jax.nn.gelu(u)
    vn = _rms(jax.nn.gelu(v), sgn_ref[...]).astype(jnp.bfloat16)

    row = lax.broadcasted_iota(jnp.int32, (BLK, BLK), 0)
    col = lax.broadcasted_iota(jnp.int32, (BLK, BLK), 1)
    tril = row >= col
    lane_lo = lax.broadcasted_iota(jnp.int32, (BLK, LANES), 1) < HEAD_DIM
    wts = [jnp.where(tril, sgw_ref[g], 0.0).astype(jnp.bfloat16) for g in range(SGU_GROUPS)]
    for c in range(tm // BLK):
        rows = slice(c * BLK, (c + 1) * BLK)
        for p in range(SGU_GROUPS // 2):
            lanes = slice(p * LANES, (p + 1) * LANES)
            vp = vn[rows, lanes]
            s = jnp.where(lane_lo, _dot(wts[2 * p], vp), _dot(wts[2 * p + 1], vp))
            a_ref[rows, lanes] = (gu[rows, lanes] * (s + sgb_ref[:, lanes])).astype(a_ref.dtype)

    o0 = 2 * SGU_WIDTH
    for g, ((_, dil), o_ref) in enumerate(zip(DIL_GROUPS, (q0_ref, q1_ref, q2_ref))):
        if dil == 1:
            hg, cos, sin = hn, cos_ref[...], sin_ref[...]
        else:
            hg = _residue_major(hn_scr, tm, D_MODEL, dil).astype(jnp.bfloat16)
            cos = _residue_major(cos_ref, tm, LANES, dil)
            sin = _residue_major(sin_ref, tm, LANES, dil)
        gl = slice(g * DIL_GW, (g + 1) * DIL_GW)
        parts = []
        for p, gain_ref in enumerate((qn_ref, kn_ref, None)):
            c0 = o0 + p * DIL_WIDTH + g * DIL_GW
            y = _dot(hg, w_ref[:, c0:c0 + DIL_GW])
            if gain_ref is not None:
                y = _head_norm_rope(y, bd_ref[...], gain_ref[:, gl], cos, sin)
            parts.append(y.astype(o_ref.dtype))
        sub = tm // dil
        for r in range(dil):
            for p, y in enumerate(parts):
                c0 = (r * 3 + p) * DIL_GW
                o_ref[:, c0:c0 + DIL_GW] = y[r * sub:(r + 1) * sub, :]


def _const_spec(shape):
    nd = len(shape)
    return pl.BlockSpec(shape, lambda i: (0,) * nd)


def _even_in(x2, gain, w_in, cos, sin, sgu_norm, sgu_w, sgu_b, qn, kn, ones_bd):
    t = x2.shape[0]
    in_w = w_in.shape[1]
    sgb = jnp.repeat(sgu_b.T, HEAD_DIM, axis=1)
    qn_row = jnp.tile(qn * HEAD_DIM ** -0.5, DIL_WIDTH // HEAD_DIM).reshape(1, DIL_WIDTH)
    kn_row = jnp.tile(kn, DIL_WIDTH // HEAD_DIM).reshape(1, DIL_WIDTH)
    row = lambda n: pl.BlockSpec((TM, n), lambda i: (i, 0))
    return pl.pallas_call(
        _even_in_kernel,
        grid=(t // TM,),
        in_specs=[row(D_MODEL), _const_spec((1, D_MODEL)), _const_spec((D_MODEL, in_w)),
                  row(LANES), row(LANES), _const_spec((1, SGU_WIDTH)),
                  _const_spec((SGU_GROUPS, BLK, BLK)), _const_spec((BLK, SGU_WIDTH)),
                  _const_spec((1, DIL_WIDTH)), _const_spec((1, DIL_WIDTH)),
                  _const_spec(ones_bd.shape)],
        out_specs=[row(SGU_WIDTH)] + [pl.BlockSpec((TM // d, d * 3 * DIL_GW), lambda i: (i, 0))
                                      for _, d in DIL_GROUPS],
        out_shape=[jax.ShapeDtypeStruct((t, SGU_WIDTH), jnp.bfloat16)]
        + [jax.ShapeDtypeStruct((t // d, d * 3 * DIL_GW), jnp.bfloat16) for _, d in DIL_GROUPS],
        scratch_shapes=[pltpu.VMEM((ROW_CHUNKS * TM, LANES), jnp.float32)],
        compiler_params=_cparams(("arbitrary",)),
        name="even_in_proj",
    )(x2, gain.reshape(1, D_MODEL), w_in.astype(jnp.bfloat16), cos, sin,
      sgu_norm.reshape(1, SGU_WIDTH), sgu_w, sgb, qn_row, kn_row, ones_bd)


def _band_mask(max_rel, first):
    i = lax.broadcasted_iota(jnp.int32, (BLK, 2 * BLK), 0)
    j = lax.broadcasted_iota(jnp.int32, (BLK, 2 * BLK), 1)
    rel = BLK + i - j
    first_key = jnp.where(first, BLK, 0)
    return (rel >= 0) & (rel <= max_rel) & (j >= first_key)


def _head_attention(qh, kk, vv, mask, sink):
    s = jnp.where(mask, _dot_nt(qh, kk), NEG)
    m = jnp.max(s, axis=-1, keepdims=True)
    if sink is not None:
        m = jnp.maximum(m, sink)
    p = jnp.exp(s - m)
    denom = jnp.sum(p, axis=-1, keepdims=True)
    if sink is not None:
        denom = denom + jnp.exp(sink - m)
    o = _dot(p.astype(jnp.bfloat16), vv)
    return o / denom, m, denom


def _dil_attn_kernel(q_ref, kc_ref, kp_ref, vc_ref, vp_ref, o_ref, lse_ref, *, max_rel):
    first_tile = pl.program_id(2) == 0
    lane_lo = lax.broadcasted_iota(jnp.int32, (BLK, LANES), 1) < HEAD_DIM
    zero = jnp.zeros((), jnp.bfloat16)
    for j in range(q_ref.shape[0] // BLK):
        rows = slice(j * BLK, (j + 1) * BLK)
        if j == 0:
            kprev, vprev = kp_ref[...], vp_ref[...]
            mask = _band_mask(max_rel, first_tile)
        else:
            prows = slice((j - 1) * BLK, j * BLK)
            kprev, vprev = kc_ref[prows, :], vc_ref[prows, :]
            mask = _band_mask(max_rel, False)
        kk = jnp.concatenate([kprev, kc_ref[rows, :]], axis=0)
        vv = jnp.concatenate([vprev, vc_ref[rows, :]], axis=0)
        for p in range(DIL_GW // LANES):
            lanes = slice(p * LANES, (p + 1) * LANES)
            qp = q_ref[rows, lanes]
            oa, ma, da = _head_attention(jnp.where(lane_lo, qp, zero), kk[:, lanes],
                                         vv[:, lanes], mask, None)
            ob, mb, db = _head_attention(jnp.where(lane_lo, zero, qp), kk[:, lanes],
                                         vv[:, lanes], mask, None)
            o_ref[rows, lanes] = jnp.where(lane_lo, oa, ob).astype(o_ref.dtype)
            lse_ref[rows, lanes] = jnp.where(lane_lo, ma + jnp.log(da), mb + jnp.log(db))


def _dilated_group_attention(qkv, batch, seq, dil, window):
    sub = seq // dil
    a = qkv.reshape(batch, sub, dil * 3 * DIL_GW)
    tq = min(sub, DIL_TQ)
    nq = tq // BLK
    cur = lambda part: pl.BlockSpec((None, tq, DIL_GW), lambda b, r, i: (b, i, r * 3 + part))
    prev = lambda part: pl.BlockSpec(
        (None, BLK, DIL_GW), lambda b, r, i: (b, jnp.maximum(i * nq - 1, 0), r * 3 + part))
    out = pl.BlockSpec((None, tq, DIL_GW), lambda b, r, i: (b, i, r))
    o, lse = pl.pallas_call(
        functools.partial(_dil_attn_kernel, max_rel=window // dil),
        grid=(batch, dil, sub // tq),
        in_specs=[cur(0), cur(1), prev(1), cur(2), prev(2)],
        out_specs=[out, out],
        out_shape=[jax.ShapeDtypeStruct((batch, sub, dil * DIL_GW), jnp.bfloat16),
                   jax.ShapeDtypeStruct((batch, sub, dil * DIL_GW), jnp.float32)],
        compiler_params=_cparams(("arbitrary",) * 3),
        name=f"dilated_attn_d{dil}",
    )(a, a, a, a, a)
    rows = batch * sub
    return o.reshape(rows, dil * DIL_GW), lse.reshape(rows, dil * DIL_GW)


def _swa_kernel(sink_ref, q_ref, kvc_ref, kvp_ref, o_ref):
    first_tile = pl.program_id(1) == 0
    lane_lo = lax.broadcasted_iota(jnp.int32, (BLK, LANES), 1) < HEAD_DIM
    zero = jnp.zeros((), jnp.bfloat16)
    rep = SWA_Q_HEADS // SWA_KV_HEADS
    kw = SWA_KV_HEADS * LANES
    for j in range(TQ // BLK):
        rows = slice(j * BLK, (j + 1) * BLK)
        if j == 0:
            kvprev = kvp_ref[...]
            mask = _band_mask(SWA_WINDOW - 1, first_tile)
        else:
            kvprev = kvc_ref[(j - 1) * BLK:j * BLK, :]
            mask = _band_mask(SWA_WINDOW - 1, False)
        kv = jnp.concatenate([kvprev, kvc_ref[rows, :]], axis=0)
        for g in range(SWA_KV_HEADS):
            kk = kv[:, g * LANES:(g + 1) * LANES]
            vv = kv[:, kw + g * LANES:kw + (g + 1) * LANES]
            for p in range(rep // 2):
                h0 = g * rep + 2 * p
                lanes = slice(h0 * HEAD_DIM, (h0 + 2) * HEAD_DIM)
                qp = q_ref[rows, lanes]
                oa, _, _ = _head_attention(jnp.where(lane_lo, qp, zero), kk, vv, mask,
                                           sink_ref[h0])
                ob, _, _ = _head_attention(jnp.where(lane_lo, zero, qp), kk, vv, mask,
                                           sink_ref[h0 + 1])
                o_ref[rows, lanes] = jnp.where(lane_lo, oa, ob).astype(o_ref.dtype)


def _swa_attention(q, kv, sinks, batch, seq):
    q3 = q.reshape(batch, seq, q.shape[1])
    kv3 = kv.reshape(batch, seq, kv.shape[1])
    nq = TQ // BLK
    o = pl.pallas_call(
        _swa_kernel,
        grid_spec=pltpu.PrefetchScalarGridSpec(
            num_scalar_prefetch=1,
            grid=(batch, seq // TQ),
            in_specs=[pl.BlockSpec((None, TQ, q.shape[1]), lambda b, i, s: (b, i, 0)),
                      pl.BlockSpec((None, TQ, kv.shape[1]), lambda b, i, s: (b, i, 0)),
                      pl.BlockSpec((None, BLK, kv.shape[1]),
                                   lambda b, i, s: (b, jnp.maximum(i * nq - 1, 0), 0))],
            out_specs=pl.BlockSpec((None, TQ, q.shape[1]), lambda b, i, s: (b, i, 0))),
        out_shape=jax.ShapeDtypeStruct(q3.shape, jnp.bfloat16),
        compiler_params=_cparams(("arbitrary",) * 2),
        name="swa_attn",
    )(sinks.astype(jnp.float32), q3, kv3, kv3)
    return o.reshape(batch * seq, q.shape[1])


def _odd_in_kernel(x_ref, g_ref, w_ref, cos_ref, sin_ref, qn_ref, kn_ref, bd_ref, q_ref, kv_ref):
    hn = _rms(x_ref[...], g_ref[...]).astype(jnp.bfloat16)
    qw = SWA_Q_HEADS * HEAD_DIM
    kw = SWA_KV_HEADS * LANES
    cos = cos_ref[...]
    sin = sin_ref[...]
    q = _dot(hn, w_ref[:, 0:qw])
    q_ref[...] = _head_norm_rope(q, bd_ref[...], qn_ref[...], cos, sin).astype(q_ref.dtype)
    k = _dot(hn, w_ref[:, qw:qw + kw])
    kv_ref[:, 0:kw] = _head_norm_rope(k, bd_ref[...], kn_ref[...], cos, sin).astype(kv_ref.dtype)
    kv_ref[:, kw:2 * kw] = _dot(hn, w_ref[:, qw + kw:qw + 2 * kw]).astype(kv_ref.dtype)


def _odd_in(x, gain, w_in, cos, sin, qn, kn, ones_bd):
    t = x.shape[0]
    qw = SWA_Q_HEADS * HEAD_DIM
    kvw = SWA_KV_HEADS * HEAD_DIM
    dup = lambda w: jnp.concatenate(
        [w[:, h * HEAD_DIM:(h + 1) * HEAD_DIM] for h in range(SWA_KV_HEADS) for _ in range(2)],
        axis=1)
    w_all = jnp.concatenate([w_in[:, :qw], dup(w_in[:, qw:qw + kvw]),
                             dup(w_in[:, qw + kvw:qw + 2 * kvw])], axis=1).astype(jnp.bfloat16)
    kw = SWA_KV_HEADS * LANES
    qn_row = jnp.tile(qn * HEAD_DIM ** -0.5, qw // HEAD_DIM).reshape(1, qw)
    kn_row = jnp.tile(kn, kw // HEAD_DIM).reshape(1, kw)
    row = lambda n: pl.BlockSpec((TM, n), lambda i: (i, 0))
    return pl.pallas_call(
        _odd_in_kernel,
        grid=(t // TM,),
        in_specs=[row(D_MODEL), _const_spec((1, D_MODEL)), _const_spec(w_all.shape),
                  row(LANES), row(LANES), _const_spec((1, qw)), _const_spec((1, kw)),
                  _const_spec(ones_bd.shape)],
        out_specs=[row(qw), row(2 * kw)],
        out_shape=[jax.ShapeDtypeStruct((t, qw), jnp.bfloat16),
                   jax.ShapeDtypeStruct((t, 2 * kw), jnp.bfloat16)],
        compiler_params=_cparams(("arbitrary",)),
        name="odd_in_proj",
    )(x, gain.reshape(1, D_MODEL), w_all, cos, sin, qn_row, kn_row, ones_bd)


def _route(hn, wr_ref, br_ref, carry_ref, idx_ref, gate_ref, cnt_ref):
    tm = hn.shape[0]
    h_hi = hn.astype(jnp.bfloat16)
    h_lo = (hn - h_hi.astype(jnp.float32)).astype(jnp.bfloat16)
    hi_pass = _dot(h_hi, wr_ref[...])
    logits = (hi_pass[:, :ROUTER_COLS] + _dot(h_lo, wr_ref[:, :ROUTER_COLS])
              + hi_pass[:, ROUTER_COLS:] + br_ref[...])
    lt = logits.T
    rowf = lax.broadcasted_iota(jnp.int32, (SUBLANES, tm), 0).astype(jnp.float32)

    def first_argmax(x):
        m = jnp.max(x, axis=0, keepdims=True)
        idx = jnp.min(jnp.where(x == m, rowf, float(SUBLANES)), axis=0, keepdims=True)
        return m, idx

    lg = jnp.where(rowf < N_GROUPS, lt[0:SUBLANES], NEG)
    mg, gi = first_argmax(lg)
    p_top = 1.0 / jnp.sum(jnp.exp(lg - mg), axis=0, keepdims=True)
    le = lt[ROUTER_E0:ROUTER_E0 + EPG]
    for g in range(1, N_GROUPS):
        le = jnp.where(gi == float(g), lt[ROUTER_E0 + g * EPG:ROUTER_E0 + (g + 1) * EPG], le)
    m1, i1 = first_argmax(le)
    m2, i2 = first_argmax(jnp.where(rowf == i1, NEG, le))
    e2 = jnp.exp(m2 - m1)
    gate1 = p_top / (1.0 + e2)
    gate2 = p_top * e2 / (1.0 + e2)
    eid1 = gi * EPG + i1
    eid2 = gi * EPG + i2

    erow = lax.broadcasted_iota(jnp.int32, (N_EXPERTS, tm), 0).astype(jnp.float32)
    oh1 = erow == eid1
    oh2 = erow == eid2
    member = jnp.where(oh1 | oh2, 1.0, 0.0)
    s_idx = lax.broadcasted_iota(jnp.int32, (tm, tm), 0)
    t_idx = lax.broadcasted_iota(jnp.int32, (tm, tm), 1)
    upper = jnp.where(s_idx < t_idx, 1.0, 0.0).astype(jnp.bfloat16)
    before = carry_ref[:, 0:1] + _dot(member.astype(jnp.bfloat16), upper)
    rank1 = jnp.sum(jnp.where(oh1, before, 0.0), axis=0, keepdims=True)
    rank2 = jnp.sum(jnp.where(oh2, before, 0.0), axis=0, keepdims=True)
    carry_ref[...] = carry_ref[...] + jnp.sum(member, axis=1, keepdims=True)
    cnt_ref[...] = carry_ref[...].astype(jnp.int32)

    idx = jnp.where(rowf == 0.0, eid1, jnp.where(rowf == 1.0, eid2,
          jnp.where(rowf == 2.0, rank1, jnp.where(rowf == 3.0, rank2, 0.0))))
    idx_ref[...] = idx.astype(jnp.int32)
    gates = jnp.where(rowf == 0.0, gate1, jnp.where(rowf == 1.0, gate2, 0.0))
    padded = jnp.concatenate([gates] + [jnp.zeros_like(gates)] * (LANES // SUBLANES - 1), axis=0)
    gate_ref[...] = padded.T[:, :SUBLANES]


def _store_row_tiles(ref, x):
    n = x.shape[0]
    for c in range(ROW_CHUNKS):
        ref[pl.ds(c, n, stride=ROW_CHUNKS), :] = x[:, c * LANES:(c + 1) * LANES].astype(ref.dtype)


def _load_row_tiles(ref, n):
    return jnp.concatenate([ref[pl.ds(c, n, stride=ROW_CHUNKS), :] for c in range(ROW_CHUNKS)],
                           axis=1)


def _post_proj(y, x_ref, gf_ref, wr_ref, br_ref, carry_ref,
               h_ref, hn_ref, idx_ref, gate_ref, cnt_ref):
    @pl.when(pl.program_id(0) == 0)
    def _():
        carry_ref[...] = jnp.zeros_like(carry_ref)

    h = x_ref[...] + y
    h_ref[...] = h
    hn = _rms(h, gf_ref[...])
    _store_row_tiles(hn_ref, hn)
    _route(hn, wr_ref, br_ref, carry_ref, idx_ref, gate_ref, cnt_ref)


def _token_major(ref, dil, scr):
    if dil == 1:
        return ref[...].astype(jnp.float32)
    sub = ref.shape[0]
    n = sub * dil
    chunks = DIL_GW // LANES
    for r in range(dil):
        for c in range(chunks):
            l0 = r * DIL_GW + c * LANES
            scr[pl.ds(c * n + r, sub, stride=dil), :] = ref[:, l0:l0 + LANES].astype(jnp.float32)
    return jnp.concatenate([scr[c * n:(c + 1) * n, :] for c in range(chunks)], axis=1)


def _even_out_kernel(a_ref, o0_ref, o1_ref, o2_ref, l0_ref, l1_ref, l2_ref, w_ref, x_ref, gf_ref,
                     wr_ref, br_ref, h_ref, hn_ref, idx_ref, gate_ref, cnt_ref,
                     carry_ref, so1, so2, sl1, sl2):
    dils = [d for _, d in DIL_GROUPS]
    o0, o1, o2 = (_token_major(r, d, s) for r, d, s in zip((o0_ref, o1_ref, o2_ref), dils,
                                                            (None, so1, so2)))
    l0, l1, l2 = (_token_major(r, d, s) for r, d, s in zip((l0_ref, l1_ref, l2_ref), dils,
                                                            (None, sl1, sl2)))
    m = jnp.maximum(jnp.maximum(l0, l1), l2)
    e0, e1, e2 = jnp.exp(l0 - m), jnp.exp(l1 - m), jnp.exp(l2 - m)
    b = (e0 * o0 + e1 * o1 + e2 * o2) / (e0 + e1 + e2)
    y = _dot(a_ref[...], w_ref[0:SGU_WIDTH, :]) + _dot(b.astype(jnp.bfloat16),
                                                       w_ref[SGU_WIDTH:, :])
    _post_proj(y, x_ref, gf_ref, wr_ref, br_ref, carry_ref,
               h_ref, hn_ref, idx_ref, gate_ref, cnt_ref)


def _odd_out_kernel(o_ref, w_ref, x_ref, gf_ref, wr_ref, br_ref,
                    h_ref, hn_ref, idx_ref, gate_ref, cnt_ref, carry_ref):
    y = _dot(o_ref[...], w_ref[...])
    _post_proj(y, x_ref, gf_ref, wr_ref, br_ref, carry_ref,
               h_ref, hn_ref, idx_ref, gate_ref, cnt_ref)


def _router_weights(w_rg, b_rg, w_re, b_re):
    d = w_rg.shape[0]
    w = jnp.zeros((d, ROUTER_COLS), jnp.float32)
    w = w.at[:, 0:N_GROUPS].set(w_rg)
    w = w.at[:, ROUTER_E0:ROUTER_E0 + N_EXPERTS].set(
        jnp.transpose(w_re, (1, 0, 2)).reshape(d, N_EXPERTS))
    b = jnp.zeros((1, ROUTER_COLS), jnp.float32)
    b = b.at[0, 0:N_GROUPS].set(b_rg)
    b = b.at[0, ROUTER_E0:ROUTER_E0 + N_EXPERTS].set(b_re.reshape(N_EXPERTS))
    w_hi = w.astype(jnp.bfloat16)
    w_lo = (w - w_hi.astype(jnp.float32)).astype(jnp.bfloat16)
    return jnp.concatenate([w_hi, w_lo], axis=1), b


def _out_proj(kernel, acts, w_out, x2, gain_ffn, router, extra_scratch=()):
    t = x2.shape[0]
    w_router, b = router
    row = lambda n: pl.BlockSpec((TM, n), lambda i: (i, 0))
    colblk = pl.BlockSpec((SUBLANES, TM), lambda i: (0, i))
    return pl.pallas_call(
        kernel,
        grid=(t // TM,),
        in_specs=[pl.BlockSpec((TM * a.shape[0] // t, a.shape[1]), lambda i: (i, 0)) for a in acts]
        + [_const_spec(w_out.shape), row(D_MODEL), _const_spec((1, D_MODEL)),
           _const_spec(w_router.shape), _const_spec(b.shape)],
        out_specs=[row(D_MODEL), pl.BlockSpec((TM * ROW_CHUNKS, LANES), lambda i: (i, 0)),
                   colblk, row(SUBLANES),
                   _const_spec((N_EXPERTS, LANES))],
        out_shape=[jax.ShapeDtypeStruct((t, D_MODEL), jnp.float32),
                   jax.ShapeDtypeStruct((t * ROW_CHUNKS, LANES), ROW_DTYPE),
                   jax.ShapeDtypeStruct((SUBLANES, t), jnp.int32),
                   jax.ShapeDtypeStruct((t, SUBLANES), jnp.float32),
                   jax.ShapeDtypeStruct((N_EXPERTS, LANES), jnp.int32)],
        scratch_shapes=[pltpu.VMEM((N_EXPERTS, LANES), jnp.float32)] + list(extra_scratch),
        compiler_params=_cparams(("arbitrary",)),
        name=kernel.__name__.strip("_"),
    )(*acts, w_out.astype(jnp.bfloat16), x2, gain_ffn.reshape(1, D_MODEL), w_router, b)


ROWS_PER_TILE = TOP_K * TM
DISPATCH_TM = 2048
DISPATCH_SLOTS = 3
ISSUE_UNROLL = 8
FFN_SLOTS = 4


def _row_tile(ref, row):
    return ref.at[pl.ds(pl.multiple_of(row * ROW_CHUNKS, ROW_CHUNKS), ROW_CHUNKS), :]


def _wait_rows(copy, n):
    def body(_, c):
        copy.wait()
        return c

    lax.fori_loop(0, n, body, 0, unroll=16)


def _dispatch_kernel(pstart_ref, pend_ref, dest0_ref, dest1_ref, hn_ref, xs_ref, zero_ref, hbuf,
                     ld_sems, row_sems, zsem):
    blk = MOE_BM * ROW_CHUNKS
    i = pl.program_id(0)
    n = pl.num_programs(0)
    tile = DISPATCH_TM * ROW_CHUNKS

    def load(b):
        s = lax.rem(b, DISPATCH_SLOTS)
        return pltpu.make_async_copy(hn_ref.at[pl.ds(pl.multiple_of(b * tile, tile), tile), :],
                                     hbuf.at[s], ld_sems.at[s])

    def zero_block(row0):
        return pltpu.make_async_copy(
            zero_ref, xs_ref.at[pl.ds(pl.multiple_of(row0 * ROW_CHUNKS, blk), blk), :], zsem)

    @pl.when(pl.program_id(0) == 0)
    def _():
        zero_ref[...] = jnp.zeros_like(zero_ref)

        def start_last(c, _):
            @pl.when(pend_ref[c] > pstart_ref[c])
            def _():
                zero_block(pend_ref[c] - MOE_BM).start()
            return 0

        def wait_last(c, _):
            @pl.when(pend_ref[c] > pstart_ref[c])
            def _():
                zero_block(pend_ref[c] - MOE_BM).wait()
            return 0

        lax.fori_loop(0, N_EXPERTS, start_last, 0)
        lax.fori_loop(0, N_EXPERTS, wait_last, 0)
        first_unused = lax.div(pend_ref[N_EXPERTS - 1], MOE_BM)
        n_blocks = xs_ref.shape[0] // blk

        def start_tail(b, _):
            zero_block(b * MOE_BM).start()
            return 0

        def wait_tail(b, _):
            zero_block(b * MOE_BM).wait()
            return 0

        lax.fori_loop(first_unused, n_blocks, start_tail, 0)
        lax.fori_loop(first_unused, n_blocks, wait_tail, 0)

        for b in range(DISPATCH_SLOTS - 1):
            @pl.when(b < n)
            def _():
                load(i + b).start()

    dest_refs = (dest0_ref, dest1_ref)

    def row_copy(slot, src, dst):
        return pltpu.make_async_copy(_row_tile(hbuf.at[slot], src), _row_tile(xs_ref, dst),
                                     row_sems.at[slot])

    slot = lax.rem(i, DISPATCH_SLOTS)
    load(i).wait()

    def issue(b, c):
        for j in range(ISSUE_UNROLL):
            t = b * ISSUE_UNROLL + j
            for k, d_ref in enumerate(dest_refs):
                row_copy(slot, t, d_ref[0, t]).start(priority=k)
        return c

    lax.fori_loop(0, DISPATCH_TM // ISSUE_UNROLL, issue, 0)

    @pl.when(i > 0)
    def _():
        _wait_rows(row_copy(lax.rem(i - 1, DISPATCH_SLOTS), 0, 0), TOP_K * DISPATCH_TM)

    @pl.when(i + (DISPATCH_SLOTS - 1) < n)
    def _():
        load(i + (DISPATCH_SLOTS - 1)).start()

    @pl.when(i == n - 1)
    def _():
        _wait_rows(row_copy(slot, 0, 0), TOP_K * DISPATCH_TM)


def _dispatch(hn, dest, pstart, pend, rows):
    t = hn.shape[0] // ROW_CHUNKS
    n = t // DISPATCH_TM
    dspec = pl.BlockSpec((None, 1, DISPATCH_TM), lambda i, ps, pe: (i, 0, 0),
                         memory_space=pltpu.SMEM)
    return pl.pallas_call(
        _dispatch_kernel,
        grid_spec=pltpu.PrefetchScalarGridSpec(
            num_scalar_prefetch=2,
            grid=(n,),
            in_specs=[dspec] * TOP_K + [pl.BlockSpec(memory_space=pl.ANY)],
            out_specs=pl.BlockSpec(memory_space=pl.ANY),
            scratch_shapes=[pltpu.VMEM((MOE_BM * ROW_CHUNKS, LANES), ROW_DTYPE),
                            pltpu.VMEM((DISPATCH_SLOTS, DISPATCH_TM * ROW_CHUNKS, LANES),
                                       ROW_DTYPE),
                            pltpu.SemaphoreType.DMA((DISPATCH_SLOTS,)),
                            pltpu.SemaphoreType.DMA((DISPATCH_SLOTS,)),
                            pltpu.SemaphoreType.DMA(())]),
        out_shape=jax.ShapeDtypeStruct((rows * ROW_CHUNKS, LANES), ROW_DTYPE),
        compiler_params=_cparams(("arbitrary",)),
        name="moe_dispatch",
    )(pstart, pend, *[dest[k].reshape(n, 1, DISPATCH_TM) for k in range(TOP_K)], hn)


def _expert_kernel(blk_e_ref, wslot_ref, next_e_ref, nused_ref, xs_ref, wg_ref, wu_ref, wd_ref,
                   ys_ref, wg_s, wu_s, wd_s, wg_buf, wu_buf, wd_buf, xbuf, ybuf, zbuf,
                   w_sems, in_sems, out_sems, zsem, *, layer):
    i = pl.program_id(0)
    n_steps = pl.num_programs(0)
    nu = nused_ref[0]
    blk = MOE_BM * ROW_CHUNKS

    def weight_copies(e, s):
        return [pltpu.make_async_copy(w_ref.at[layer, e], buf.at[s], w_sems.at[s])
                for w_ref, buf in ((wg_ref, wg_buf), (wu_ref, wu_buf), (wd_ref, wd_buf))]

    def block(ref, b):
        return ref.at[pl.ds(pl.multiple_of(b * blk, blk), blk), :]

    def load(b):
        s = lax.rem(b, FFN_SLOTS)
        return pltpu.make_async_copy(block(xs_ref, b), xbuf.at[s], in_sems.at[s])

    def store(b):
        s = lax.rem(b, FFN_SLOTS)
        return pltpu.make_async_copy(ybuf.at[s], block(ys_ref, b), out_sems.at[s])

    @pl.when(i == 0)
    def _():
        for c in weight_copies(blk_e_ref[0], wslot_ref[blk_e_ref[0]]):
            c.start()
        for b in range(FFN_SLOTS - 1):
            @pl.when(b < nu)
            def _():
                load(i + b).start()

    @pl.when(i + (FFN_SLOTS - 1) < nu)
    def _():
        load(i + (FFN_SLOTS - 1)).start()

    @pl.when(i < nu)
    def _():
        @pl.when((i == 0) | (blk_e_ref[i] != blk_e_ref[jnp.maximum(i - 1, 0)]))
        def _():
            e = blk_e_ref[i]
            ws = wslot_ref[e]
            for c in weight_copies(e, ws):
                c.wait()

            @pl.when(next_e_ref[e] >= 0)
            def _():
                for c in weight_copies(next_e_ref[e], 1 - ws):
                    c.start()

            wg_s[...] = wg_buf[ws].astype(jnp.bfloat16)
            wu_s[...] = wu_buf[ws].astype(jnp.bfloat16)
            wd_s[...] = wd_buf[ws].astype(jnp.bfloat16)

        slot = lax.rem(i, FFN_SLOTS)
        load(i).wait()

        @pl.when(i >= FFN_SLOTS)
        def _():
            store(i - FFN_SLOTS).wait()

        x = _load_row_tiles(xbuf.at[slot], MOE_BM).astype(jnp.bfloat16)
        hid = jax.nn.silu(_dot(x, wg_s[...])) * _dot(x, wu_s[...])
        _store_row_tiles(ybuf.at[slot], _dot(hid.astype(jnp.bfloat16), wd_s[...]))
        store(i).start()

    @pl.when(i >= nu)
    def _():
        @pl.when(i == nu)
        def _():
            zbuf[...] = jnp.zeros_like(zbuf)

        fill = pltpu.make_async_copy(zbuf, block(ys_ref, i), zsem)
        fill.start()
        fill.wait()

    @pl.when(i == n_steps - 1)
    def _():
        for d in range(FFN_SLOTS):
            b = nu - 1 - d

            @pl.when(b >= 0)
            def _():
                store(b).wait()


def _expert_ffn(xs, blk_e, wslot, next_e, nused, layer, w_gate, w_up, w_down):
    rows = xs.shape[0] // ROW_CHUNKS
    blk = MOE_BM * ROW_CHUNKS
    ring = pltpu.VMEM((FFN_SLOTS, blk, LANES), ROW_DTYPE)
    any_spec = pl.BlockSpec(memory_space=pl.ANY)
    return pl.pallas_call(
        functools.partial(_expert_kernel, layer=layer),
        grid_spec=pltpu.PrefetchScalarGridSpec(
            num_scalar_prefetch=4,
            grid=(rows // MOE_BM,),
            in_specs=[any_spec] * 4,
            out_specs=any_spec,
            scratch_shapes=[pltpu.VMEM((D_MODEL, EXPERT_FF), jnp.bfloat16),
                            pltpu.VMEM((D_MODEL, EXPERT_FF), jnp.bfloat16),
                            pltpu.VMEM((EXPERT_FF, D_MODEL), jnp.bfloat16),
                            pltpu.VMEM((2, D_MODEL, EXPERT_FF), jnp.float32),
                            pltpu.VMEM((2, D_MODEL, EXPERT_FF), jnp.float32),
                            pltpu.VMEM((2, EXPERT_FF, D_MODEL), jnp.float32),
                            ring, ring, pltpu.VMEM((blk, LANES), ROW_DTYPE),
                            pltpu.SemaphoreType.DMA((2,)),
                            pltpu.SemaphoreType.DMA((FFN_SLOTS,)),
                            pltpu.SemaphoreType.DMA((FFN_SLOTS,)),
                            pltpu.SemaphoreType.DMA(())]),
        out_shape=jax.ShapeDtypeStruct((rows * ROW_CHUNKS, LANES), ROW_DTYPE),
        compiler_params=_cparams(("arbitrary",)),
        name="moe_expert_ffn",
    )(blk_e, wslot, next_e, nused, xs, w_gate, w_up, w_down)


def _combine_kernel(d0_ref, d1_ref, d0_next_ref, d1_next_ref, h_ref, gate_ref, ys_ref, out_ref,
                    buf_ref, sems):
    i = pl.program_id(0)
    n = pl.num_programs(0)

    def row_copy(src, slot, k, t):
        return pltpu.make_async_copy(_row_tile(ys_ref, src), _row_tile(buf_ref.at[slot, k], t),
                                     sems.at[slot])

    def issue_tile(d_refs, slot):
        def issue(b, c):
            for j in range(ISSUE_UNROLL):
                t = b * ISSUE_UNROLL + j
                for k, d_ref in enumerate(d_refs):
                    row_copy(d_ref[0, t], slot, k, t).start(priority=k)
            return c

        lax.fori_loop(0, TM // ISSUE_UNROLL, issue, 0)

    slot = lax.rem(i, 2)

    @pl.when(i == 0)
    def _():
        issue_tile((d0_ref, d1_ref), 0)

    @pl.when(i + 1 < n)
    def _():
        issue_tile((d0_next_ref, d1_next_ref), 1 - slot)

    _wait_rows(row_copy(0, slot, 0, 0), ROWS_PER_TILE)
    g = gate_ref[...]
    r1 = _load_row_tiles(buf_ref.at[slot, 0], TM)
    r2 = _load_row_tiles(buf_ref.at[slot, 1], TM)
    out_ref[...] = h_ref[...] + (r1 * g[:, 0:1] + r2 * g[:, 1:2])


def _combine(h, gate_cols, ys, dest):
    t = h.shape[0]
    n = t // TM
    dest_tiles = [dest[k].reshape(n, 1, TM) for k in range(TOP_K)]
    dspec = lambda f: pl.BlockSpec((None, 1, TM), f, memory_space=pltpu.SMEM)
    cur = dspec(lambda i: (i, 0, 0))
    nxt = dspec(lambda i: (jnp.minimum(i + 1, n - 1), 0, 0))
    return pl.pallas_call(
        _combine_kernel,
        grid=(n,),
        in_specs=[cur] * TOP_K + [nxt] * TOP_K + [
                  pl.BlockSpec((TM, D_MODEL), lambda i: (i, 0)),
                  pl.BlockSpec((TM, SUBLANES), lambda i: (i, 0)),
                  pl.BlockSpec(memory_space=pl.ANY)],
        out_specs=pl.BlockSpec((TM, D_MODEL), lambda i: (i, 0)),
        scratch_shapes=[pltpu.VMEM((2, TOP_K, TM * ROW_CHUNKS, LANES), ROW_DTYPE),
                        pltpu.SemaphoreType.DMA((2,))],
        out_shape=jax.ShapeDtypeStruct((t, D_MODEL), jnp.float32),
        compiler_params=_cparams(("arbitrary",)),
        name="moe_combine",
    )(*dest_tiles, *dest_tiles, h, gate_cols, ys)


def _moe(h, hn, idx, gates, counts, layer, w_gate, w_up, w_down):
    t = h.shape[0]
    rows = t * TOP_K + N_EXPERTS * MOE_BM
    nblk = rows // MOE_BM
    cnt = counts[:, 0]
    padded = (cnt + MOE_BM - 1) // MOE_BM * MOE_BM
    pend = jnp.cumsum(padded).astype(jnp.int32)
    pstart = pend - padded
    blk_row0 = jnp.arange(nblk, dtype=jnp.int32) * MOE_BM
    blk_e = jnp.minimum(jnp.sum(pend[None, :] <= blk_row0[:, None], axis=1),
                        N_EXPERTS - 1).astype(jnp.int32)
    nused = (pend[-1:] // MOE_BM).astype(jnp.int32)
    experts = jnp.arange(N_EXPERTS, dtype=jnp.int32)
    nonempty = cnt > 0
    wslot = ((jnp.cumsum(nonempty) - nonempty) % 2).astype(jnp.int32)
    later = jnp.where(nonempty[None, :] & (experts[None, :] > experts[:, None]),
                      experts[None, :], N_EXPERTS)
    next_e = jnp.min(later, axis=1)
    next_e = jnp.where(next_e == N_EXPERTS, -1, next_e).astype(jnp.int32)
    eid, rank = idx[0:TOP_K], idx[TOP_K:2 * TOP_K]
    seg_start = jnp.sum(jnp.where(eid[..., None] == jnp.arange(N_EXPERTS), pstart, 0), axis=-1)
    dest = seg_start + rank
    xs = _dispatch(hn, dest, pstart, pend, rows)
    ys = _expert_ffn(xs, blk_e, wslot, next_e, nused, layer, w_gate, w_up, w_down)
    return _combine(h, gates, ys, dest)


def kernel(x, positions, norm_mix, norm_ffn, w_in_even, w_out_even, sgu_norm, sgu_w, sgu_b,
           qn_dil, kn_dil, w_in_odd, w_out_odd, qn_swa, kn_swa, sinks,
           w_router_g, b_router_g, w_router_e, b_router_e, w_gate, w_up, w_down):
    batch, seq, d = x.shape
    t = batch * seq
    depth = norm_mix.shape[0]
    cos, sin = _rope_tables(positions)
    bd = 2 * LANES
    ones_bd = (jnp.arange(bd)[:, None] // HEAD_DIM == jnp.arange(bd)[None, :] // HEAD_DIM
               ).astype(jnp.bfloat16)
    h = x.reshape(t, d)
    for layer in range(depth):
        i = layer // 2
        router = _router_weights(w_router_g[layer], b_router_g[layer], w_router_e[layer],
                                 b_router_e[layer])
        if layer % 2 == 0:
            a, *qkv = _even_in(h, norm_mix[layer], w_in_even[i], cos, sin, sgu_norm[i],
                               sgu_w[i], sgu_b[i], qn_dil[i], kn_dil[i], ones_bd)
            outs, lses = [], []
            for g, (window, dil) in enumerate(DIL_GROUPS):
                o, lse = _dilated_group_attention(qkv[g], batch, seq, dil, window)
                outs.append(o)
                lses.append(lse)
            stage = [pltpu.VMEM((TM * DIL_GW // LANES, LANES), jnp.float32)] * 4
            res = _out_proj(_even_out_kernel, [a] + outs + lses, w_out_even[i], h,
                            norm_ffn[layer], router, stage)
        else:
            q, kv = _odd_in(h, norm_mix[layer], w_in_odd[i], cos, sin, qn_swa[i], kn_swa[i],
                            ones_bd)
            o = _swa_attention(q, kv, sinks[i], batch, seq)
            res = _out_proj(_odd_out_kernel, [o], w_out_odd[i], h, norm_ffn[layer], router)
        h_mid, hn, idx, gates, counts = res
        h = _moe(h_mid, hn, idx, gates, counts, layer, w_gate, w_up, w_down)
    return h.reshape(batch, seq, d)
```
